```python
import math
import jax
import jax.numpy as jnp
from jax import lax
import numpy as np

D_MODEL = 1024
BATCH = 8
SEQ = 4096
DEPTH = 4

GRID_W = 64
CTX_LEN = 256
N_MIXERS = 2
NORM_EPS = 1e-6

MLA_HEADS = 8
QK_NOPE = 128
QK_ROPE = 64
V_HEAD = 128
QK_HEAD = QK_NOPE + QK_ROPE
Q_LORA = 384
KV_LORA = 256
ROPE_BASE = 10000.0
AXIS_PAIRS = QK_ROPE // 4
Q_BLOCK = 128
SM_SCALE = QK_HEAD ** -0.5

RWKV_HEAD = 64
RWKV_HEADS = D_MODEL // RWKV_HEAD
DECAY_LORA = 64
AAA_LORA = 64
MV_LORA = 32
GATE_LORA = 160
LN_X_EPS = 64e-5

D_FF = 2816
N_EXPERTS = 8
TOP_K = 2
D_FF_EXPERT = 3584

N_MLA = (DEPTH + 1) // 2
N_RWKV = DEPTH // 2
N_VRES = max(N_RWKV - 1, 0)
N_DENSE = (DEPTH + 1) // 2
N_MOE = DEPTH // 2

kernel_name = "hybrid_mla_rwkv7_moe_dit_prefix"


def rms_norm(x, g):
    xf = x.astype(jnp.float32)
    y = xf * lax.rsqrt(jnp.mean(xf * xf, axis=-1, keepdims=True) + NORM_EPS)
    return (y * g.astype(jnp.float32)).astype(x.dtype)


def modulate(h, shift, scale):
    return h * (1 + scale) + shift


def axial_rope_tables(rows):
    row_ids = jnp.repeat(jnp.arange(rows, dtype=jnp.float32), GRID_W)
    col_ids = jnp.tile(jnp.arange(GRID_W, dtype=jnp.float32), rows)
    inv_freq = 1.0 / (ROPE_BASE ** (jnp.arange(AXIS_PAIRS, dtype=jnp.float32) / AXIS_PAIRS))
    ang_r = row_ids[:, None] * inv_freq[None, :]
    ang_c = col_ids[:, None] * inv_freq[None, :]
    return (jnp.cos(ang_r), jnp.sin(ang_r), jnp.cos(ang_c), jnp.sin(ang_c))


def _rotate(z, cos, sin):
    m = cos.shape[-1]
    z1, z2 = z[..., :m], z[..., m:]
    cos = cos[:, None, :]
    sin = sin[:, None, :]
    return jnp.concatenate([z1 * cos - z2 * sin, z2 * cos + z1 * sin], axis=-1)


def apply_axial_rope(z, tables):
    cr, sr, cc, sc = tables
    zf = z.astype(jnp.float32)
    half = QK_ROPE // 2
    out = jnp.concatenate([_rotate(zf[..., :half], cr, sr), _rotate(zf[..., half:], cc, sc)], axis=-1)
    return out.astype(z.dtype)


def mla_queries(h, p, rope):
    B, T, _ = h.shape
    cq = rms_norm(h @ p["wqa"], p["qa_norm"])
    q = rms_norm((cq @ p["wqb"]).reshape(B, T, MLA_HEADS, QK_HEAD), p["q_norm"])
    if rope is not None:
        q = jnp.concatenate([q[..., :QK_NOPE], apply_axial_rope(q[..., QK_NOPE:], rope)], axis=-1)
    return q


def mla_keys_values(h, p, rope):
    B, T, _ = h.shape
    kv_a = h @ p["wkva"]
    ckv = rms_norm(kv_a[..., :KV_LORA], p["kva_norm"])
    k_rope = jnp.broadcast_to(kv_a[..., None, KV_LORA:], (B, T, MLA_HEADS, QK_ROPE))
    kv = (ckv @ p["wkvb"]).reshape(B, T, MLA_HEADS, QK_NOPE + V_HEAD)
    k = rms_norm(jnp.concatenate([kv[..., :QK_NOPE], k_rope], axis=-1), p["k_norm"])
    if rope is not None:
        k = jnp.concatenate([k[..., :QK_NOPE], apply_axial_rope(k[..., QK_NOPE:], rope)], axis=-1)
    return k, kv[..., QK_NOPE:]


def softmax_attend(q, k, v):
    s = jnp.einsum("bqhd,bkhd->bhqk", q, k, preferred_element_type=jnp.float32) * SM_SCALE
    pr = jax.nn.softmax(s, axis=-1).astype(v.dtype)
    return jnp.einsum("bhqk,bkhd->bqhd", pr, v)


def mla_mixer(hl, hc, p, rope, need_ctx_out):
    B, T, _ = hl.shape
    k_c, v_c = mla_keys_values(hc, p, None)
    k_l, v_l = mla_keys_values(hl, p, rope)
    q_l = mla_queries(hl, p, rope)
    k_all = jnp.concatenate([k_c, k_l], axis=1)
    v_all = jnp.concatenate([v_c, v_l], axis=1)
    nblk = T // Q_BLOCK
    qb = jnp.moveaxis(q_l.reshape(B, nblk, Q_BLOCK, MLA_HEADS, QK_HEAD), 1, 0)
    ob = lax.map(lambda qq: softmax_attend(qq, k_all, v_all), qb)
    out_l = jnp.moveaxis(ob, 0, 1).reshape(B, T, MLA_HEADS * V_HEAD) @ p["wo"]
    out_c = None
    if need_ctx_out:
        q_c = mla_queries(hc, p, None)
        out_c = softmax_attend(q_c, k_c, v_c).reshape(B, hc.shape[1], MLA_HEADS * V_HEAD) @ p["wo"]
    return out_l, out_c


def token_shift_centred(h):
    hp = jnp.pad(h, ((0, 0), (1, 1), (0, 0)))
    return 0.5 * (hp[:, :-2] + hp[:, 2:]) - h


def rwkv_features(h, p, vres, v_first, need_out):
    B, T, D = h.shape
    heads = lambda z: z.reshape(B, T, RWKV_HEADS, RWKV_HEAD)
    xx = token_shift_centred(h)
    mix = p["mix"]
    xw = h + xx * mix[1]
    xk = h + xx * mix[2]
    xv = h + xx * mix[3]
    xa = h + xx * mix[4]
    k = xk @ p["wk"]
    v = xv @ p["wv"]
    if vres is not None:
        v = v + (v_first - v) * jax.nn.sigmoid(vres["v0"] + (xv @ vres["v1"]) @ vres["v2"])
    kf = k.astype(jnp.float32)
    kk = heads(kf * p["k_k"])
    kk = kk / jnp.maximum(jnp.sqrt(jnp.sum(kk * kk, axis=-1, keepdims=True)), 1e-12)
    decay, a, kd = [], [], []
    for d in range(2):
        w_log = -jax.nn.softplus(-(p["w0"][d] + jnp.tanh(xw @ p["w1"][d]) @ p["w2"][d]).astype(jnp.float32)) - 0.5
        decay.append(heads(jnp.exp(-jnp.exp(w_log))))
        ad = jax.nn.sigmoid((p["a0"][d] + (xa @ p["a1"][d]) @ p["a2"][d]).astype(jnp.float32))
        a.append(heads(ad))
        kd.append(heads(kf * (1 + (ad - 1) * p["k_a"])))
    f = {"v": v, "vh": heads(v.astype(jnp.float32)), "kk": kk, "decay": decay, "a": a, "k": kd}
    if need_out:
        xr = h + xx * mix[0]
        xg = h + xx * mix[5]
        f["r"] = heads((xr @ p["wr"]).astype(jnp.float32))
        f["g"] = jax.nn.sigmoid(xg @ p["g1"]) @ p["g2"]
        f["k_bonus"] = 0.5 * (kd[0] + kd[1])
    return f


def wkv_scan(S0, decay, k, v, kk, a, r, reverse):
    emit = r is not None
    tm = lambda z: jnp.moveaxis(z.astype(jnp.float32), 1, 0)
    xs = (tm(decay), tm(k), tm(v), tm(-kk), tm(kk * a)) + ((tm(r),) if emit else ())

    def step(S, inp):
        w_t, k_t, v_t, a_t, b_t = inp[:5]
        sa = jnp.einsum("bhvk,bhk->bhv", S, a_t)
        S = S * w_t[:, :, None, :] + sa[..., None] * b_t[:, :, None, :] + v_t[..., None] * k_t[:, :, None, :]
        y = jnp.einsum("bhvk,bhk->bhv", S, inp[5]) if emit else None
        return S, y

    S, ys = lax.scan(step, S0, xs, reverse=reverse)
    return S, (jnp.moveaxis(ys, 0, 1) if emit else None)


def rwkv_output(y, f, p, dtype):
    B, T = y.shape[:2]
    mu = jnp.mean(y, axis=-1, keepdims=True)
    var = jnp.mean(jnp.square(y - mu), axis=-1, keepdims=True)
    yn = ((y - mu) * lax.rsqrt(var + LN_X_EPS)).reshape(B, T, D_MODEL) * p["ln_w"] + p["ln_b"]
    bonus = jnp.sum(f["r"] * f["k_bonus"] * p["r_k"], axis=-1, keepdims=True) * f["vh"]
    o = ((yn + bonus.reshape(B, T, D_MODEL)) * f["g"]).astype(dtype)
    return o @ p["wo"]


def rwkv_mixer(hl, hc, p, vres, v_first_l, v_first_c, need_ctx_out):
    B = hl.shape[0]
    fl = rwkv_features(hl, p, vres, v_first_l, True)
    fc = rwkv_features(hc, p, vres, v_first_c, need_ctx_out)
    y_l = 0.0
    y_c = 0.0
    for d, rev in ((0, False), (1, True)):
        S0 = jnp.zeros((B, RWKV_HEADS, RWKV_HEAD, RWKV_HEAD), jnp.float32)
        S_c, yc = wkv_scan(S0, fc["decay"][d], fc["k"][d], fc["vh"], fc["kk"], fc["a"][d], fc.get("r"), rev)
        _, yl = wkv_scan(S_c, fl["decay"][d], fl["k"][d], fl["vh"], fl["kk"], fl["a"][d], fl["r"], rev)
        y_l = y_l + yl
        if need_ctx_out:
            y_c = y_c + yc
    out_l = rwkv_output(y_l, fl, p, hl.dtype)
    out_c = rwkv_output(y_c, fc, p, hc.dtype) if need_ctx_out else None
    return out_l, out_c, fl["v"], fc["v"]


def swiglu(h, w1, w3, w2):
    return (jax.nn.silu(h @ w1) * (h @ w3)) @ w2


def moe_swiglu(h, router, w1, w3, w2):
    logits = (h @ router).astype(jnp.float32)
    top_val, top_idx = lax.top_k(logits, TOP_K)
    top_w = jax.nn.softmax(top_val, axis=-1)
    gates = jnp.sum(jax.nn.one_hot(top_idx, N_EXPERTS, dtype=jnp.float32) * top_w[..., None], axis=1)
    y = jnp.zeros_like(h)
    for e in range(N_EXPERTS):
        y = y + gates[:, e:e + 1].astype(h.dtype) * swiglu(h, w1[e], w3[e], w2[e])
    return y


def setup_inputs(seed: int = 0) -> dict:
    key = jax.random.key(seed)
    keys = jax.random.split(key, 48)
    counter = iter(range(48))
    D = D_MODEL

    def nrm(shape, scale):
        return scale * jax.random.normal(keys[next(counter)], shape, jnp.float32)

    def uni(shape, lo, hi):
        return jax.random.uniform(keys[next(counter)], shape, jnp.float32, lo, hi)

    def gain(shape):
        return 1.0 + nrm(shape, 0.02)

    return {
        "x": nrm((BATCH, SEQ, D), 1.0),
        "c": nrm((BATCH, D), 1.0),
        "ctx": nrm((BATCH, CTX_LEN, D), 1.0),
        "c_ctx": nrm((D,), 1.0),
        "ada_w": nrm((DEPTH, D, 6 * D), 0.5 * D ** -0.5),
        "ada_b": nrm((DEPTH, 6 * D), 0.02),
        "norm1_g": gain((DEPTH, D)),
        "norm2_g": gain((DEPTH, D)),
        "mla_wqa": nrm((N_MLA, D, Q_LORA), D ** -0.5),
        "mla_qa_norm": gain((N_MLA, Q_LORA)),
        "mla_wqb": nrm((N_MLA, Q_LORA, MLA_HEADS * QK_HEAD), Q_LORA ** -0.5),
        "mla_wkva": nrm((N_MLA, D, KV_LORA + QK_ROPE), D ** -0.5),
        "mla_kva_norm": gain((N_MLA, KV_LORA)),
        "mla_wkvb": nrm((N_MLA, KV_LORA, MLA_HEADS * (QK_NOPE + V_HEAD)), KV_LORA ** -0.5),
        "mla_q_norm": gain((N_MLA, QK_HEAD)),
        "mla_k_norm": gain((N_MLA, QK_HEAD)),
        "mla_wo": nrm((N_MLA, MLA_HEADS * V_HEAD, D), (MLA_HEADS * V_HEAD) ** -0.5),
        "rwkv_mix": uni((N_RWKV, 6, D), 0.0, 1.0),
        "rwkv_wr": nrm((N_RWKV, D, D), D ** -0.5),
        "rwkv_wk": nrm((N_RWKV, D, D), D ** -0.5),
        "rwkv_wv": nrm((N_RWKV, D, D), D ** -0.5),
        "rwkv_wo": nrm((N_RWKV, D, D), D ** -0.5),
        "rwkv_w0": uni((N_RWKV, 2, D), -5.5, 0.5),
        "rwkv_w1": nrm((N_RWKV, 2, D, DECAY_LORA), D ** -0.5),
        "rwkv_w2": nrm((N_RWKV, 2, DECAY_LORA, D), 0.1 * DECAY_LORA ** -0.5),
        "rwkv_a0": nrm((N_RWKV, 2, D), 0.1),
        "rwkv_a1": nrm((N_RWKV, 2, D, AAA_LORA), D ** -0.5),
        "rwkv_a2": nrm((N_RWKV, 2, AAA_LORA, D), 0.1 * AAA_LORA ** -0.5),
        "rwkv_g1": nrm((N_RWKV, D, GATE_LORA), D ** -0.5),
        "rwkv_g2": nrm((N_RWKV, GATE_LORA, D), GATE_LORA ** -0.5),
        "rwkv_k_k": 0.85 + nrm((N_RWKV, D), 0.02),
        "rwkv_k_a": gain((N_RWKV, D)),
        "rwkv_r_k": nrm((N_RWKV, RWKV_HEADS, RWKV_HEAD), 0.1),
        "rwkv_ln_w": gain((N_RWKV, D)),
        "rwkv_ln_b": nrm((N_RWKV, D), 0.02),
        "rwkv_v0": 1.0 + nrm((N_VRES, D), 0.1),
        "rwkv_v1": nrm((N_VRES, D, MV_LORA), D ** -0.5),
        "rwkv_v2": nrm((N_VRES, MV_LORA, D), 0.1 * MV_LORA ** -0.5),
        "ffn_w1": nrm((N_DENSE, D, D_FF), D ** -0.5),
        "ffn_w3": nrm((N_DENSE, D, D_FF), D ** -0.5),
        "ffn_w2": nrm((N_DENSE, D_FF, D), D_FF ** -0.5),
        "moe_router": nrm((N_MOE, D, N_EXPERTS), D ** -0.5),
        "moe_w1": nrm((N_MOE, N_EXPERTS, D, D_FF_EXPERT), D ** -0.5),
        "moe_w3": nrm((N_MOE, N_EXPERTS, D, D_FF_EXPERT), D ** -0.5),
        "moe_w2": nrm((N_MOE, N_EXPERTS, D_FF_EXPERT, D), D_FF_EXPERT ** -0.5),
    }


def reference(x, c, ctx, c_ctx, ada_w, ada_b, norm1_g, norm2_g,
              mla_wqa, mla_qa_norm, mla_wqb, mla_wkva, mla_kva_norm, mla_wkvb, mla_q_norm, mla_k_norm, mla_wo,
              rwkv_mix, rwkv_wr, rwkv_wk, rwkv_wv, rwkv_wo, rwkv_w0, rwkv_w1, rwkv_w2, rwkv_a0, rwkv_a1, rwkv_a2,
              rwkv_g1, rwkv_g2, rwkv_k_k, rwkv_k_a, rwkv_r_k, rwkv_ln_w, rwkv_ln_b, rwkv_v0, rwkv_v1, rwkv_v2,
              ffn_w1, ffn_w3, ffn_w2, moe_router, moe_w1, moe_w3, moe_w2):
    B, T, D = x.shape
    L = ctx.shape[1]
    rows = T // GRID_W
    rope = axial_rope_tables(rows)
    sc = jax.nn.silu(c)
    sc_ctx = jax.nn.silu(c_ctx)
    cs = ctx
    v_first_l = None
    v_first_c = None
    for i in range(DEPTH):
        last = i == DEPTH - 1
        j = i // N_MIXERS
        mod_l = (sc @ ada_w[i] + ada_b[i]).reshape(B, 6, 1, D)
        mod_c = (sc_ctx @ ada_w[i] + ada_b[i]).reshape(6, D)

        hl = modulate(rms_norm(x, norm1_g[i]), mod_l[:, 0], mod_l[:, 1])
        hc = modulate(rms_norm(cs, norm1_g[i]), mod_c[0], mod_c[1])
        if i % N_MIXERS == 0:
            p = {"wqa": mla_wqa[j], "qa_norm": mla_qa_norm[j], "wqb": mla_wqb[j], "wkva": mla_wkva[j],
                 "kva_norm": mla_kva_norm[j], "wkvb": mla_wkvb[j], "q_norm": mla_q_norm[j],
                 "k_norm": mla_k_norm[j], "wo": mla_wo[j]}
            out_l, out_c = mla_mixer(hl, hc, p, rope, not last)
        else:
            p = {"mix": rwkv_mix[j], "wr": rwkv_wr[j], "wk": rwkv_wk[j], "wv": rwkv_wv[j], "wo": rwkv_wo[j],
                 "w0": rwkv_w0[j], "w1": rwkv_w1[j], "w2": rwkv_w2[j], "a0": rwkv_a0[j], "a1": rwkv_a1[j],
                 "a2": rwkv_a2[j], "g1": rwkv_g1[j], "g2": rwkv_g2[j], "k_k": rwkv_k_k[j], "k_a": rwkv_k_a[j],
                 "r_k": rwkv_r_k[j], "ln_w": rwkv_ln_w[j], "ln_b": rwkv_ln_b[j]}
            vres = None if j == 0 else {"v0": rwkv_v0[j - 1], "v1": rwkv_v1[j - 1], "v2": rwkv_v2[j - 1]}
            out_l, out_c, v_l, v_c = rwkv_mixer(hl, hc, p, vres, v_first_l, v_first_c, not last)
            if j == 0:
                v_first_l, v_first_c = v_l, v_c
        x = x + mod_l[:, 2] * out_l
        if not last:
            cs = cs + mod_c[2] * out_c

        hl = modulate(rms_norm(x, norm2_g[i]), mod_l[:, 3], mod_l[:, 4]).reshape(B * T, D)
        if not last:
            hc = modulate(rms_norm(cs, norm2_g[i]), mod_c[3], mod_c[4]).reshape(B * L, D)
            tokens = jnp.concatenate([hl, hc], axis=0)
        else:
            tokens = hl
        if i % 2 == 0:
            k = i // 2
            y = swiglu(tokens, ffn_w1[k], ffn_w3[k], ffn_w2[k])
        else:
            k = i // 2
            y = moe_swiglu(tokens, moe_router[k], moe_w1[k], moe_w3[k], moe_w2[k])
        x = x + mod_l[:, 5] * y[:B * T].reshape(B, T, D)
        if not last:
            cs = cs + mod_c[5] * y[B * T:].reshape(B, L, D)
    return x
```

```python
import functools

import jax
import jax.numpy as jnp
from jax import lax
from jax.experimental import pallas as pl
from jax.experimental.pallas import tpu as pltpu

F32 = jnp.float32
BF16 = jnp.bfloat16
HIGHEST = lax.Precision.HIGHEST

GRID_W = 64
NORM_EPS = 1e-6
MLA_HEADS = 8
QK_NOPE = 128
QK_ROPE = 64
V_HEAD = 128
QK_HEAD = QK_NOPE + QK_ROPE
Q_LORA = 384
KV_LORA = 256
ROPE_BASE = 10000.0
AXIS_PAIRS = QK_ROPE // 4
SM_SCALE = QK_HEAD ** -0.5
RWKV_HEAD = 64
LN_X_EPS = 64e-5
N_EXPERTS = 8

LANES = 128
QK_PAD = 2 * LANES
CHUNK = 64
VMEM_LIMIT = 56 * 1024 * 1024

_NT = (((1,), (1,)), ((), ()))
_TN = (((0,), (0,)), ((), ()))


def _dot(a, b):
    return jnp.dot(a, b, preferred_element_type=F32)


def _sigmoid(z):
    return 1.0 / (1.0 + jnp.exp(-z))


def _softplus(z):
    return jnp.maximum(z, 0.0) + jnp.log(1.0 + jnp.exp(-jnp.abs(z)))


def _rms(z, width):
    return lax.rsqrt(jnp.sum(z * z, axis=-1, keepdims=True) * (1.0 / width) + NORM_EPS)


def _norm_mod(x, g, shift, scale):
    return (x * _rms(x, x.shape[-1])) * g * (1.0 + scale) + shift


def _rope(z, cos, sin):
    lane = lax.broadcasted_iota(jnp.int32, z.shape, 1)
    partner = jnp.where((lane & 16) == 0, pltpu.roll(z, LANES - 16, axis=1), pltpu.roll(z, 16, axis=1))
    return z * cos + partner * sin


def _head_sum(z, ones4):
    outs = []
    for q in range(z.shape[1] // 256):
        blk = z[:, 256 * q:256 * (q + 1)]
        hi = blk.astype(BF16)
        lo = (blk - hi.astype(F32)).astype(BF16)
        outs.append(_dot(hi, ones4) + _dot(lo, ones4))
    return jnp.concatenate(outs, axis=1)


class _Layout:
    def __init__(self, B, T, L, D):
        self.B, self.T, self.L, self.D = B, T, L, D
        self.n_lat = B * T
        self.n = B * T + B * L

    def mod_spec(self, tm):
        n_lat_tiles = self.n_lat // tm
        T, B, D = self.T, self.B, self.D
        return pl.BlockSpec((None, 6, D), lambda i: (jnp.where(i < n_lat_tiles, (i * tm) // T, B), 0, 0))

    def rope_spec(self, tm):
        n_lat_tiles = self.n_lat // tm
        tpb = self.T // tm
        return pl.BlockSpec((tm, LANES), lambda i: (jnp.where(i < n_lat_tiles, i % tpb, tpb), 0))


def _const_spec(arr):
    nd = arr.ndim
    return pl.BlockSpec(arr.shape, lambda i: (0,) * nd, pipeline_mode=pl.Buffered(1))


def _rowwise(body, name, n_rows, tm, row_ins, tile_ins, const_ins, outs):
    in_specs = [pl.BlockSpec((tm, a.shape[1]), lambda i: (i, 0)) for a in row_ins]
    in_specs += [spec for (_, spec) in tile_ins]
    in_specs += [_const_spec(a) for a in const_ins]
    out_specs = [pl.BlockSpec((tm, w), lambda i: (i, 0)) for (w, _) in outs]
    out_shape = [jax.ShapeDtypeStruct((n_rows, w), dt) for (w, dt) in outs]
    return pl.pallas_call(
        body, grid=(n_rows // tm,), in_specs=in_specs, out_specs=out_specs, out_shape=out_shape,
        compiler_params=pltpu.CompilerParams(dimension_semantics=("parallel",), vmem_limit_bytes=VMEM_LIMIT),
        name=name,
    )(*row_ins, *[a for (a, _) in tile_ins], *const_ins)


def _ada_body(c_ref, w_ref, b_ref, o_ref):
    c = c_ref[...]
    o_ref[...] = jnp.dot(c * _sigmoid(c), w_ref[...], precision=HIGHEST, preferred_element_type=F32) + b_ref[...]


def _ada_call(cc, ada_w, ada_b):
    depth, D, W = ada_w.shape
    tn = 1024
    return pl.pallas_call(
        _ada_body, grid=(depth, W // tn),
        in_specs=[pl.BlockSpec(cc.shape, lambda l, j: (0, 0)),
                  pl.BlockSpec((None, D, tn), lambda l, j: (l, 0, j)),
                  pl.BlockSpec((None, 1, tn), lambda l, j: (l, 0, j))],
        out_specs=pl.BlockSpec((None, cc.shape[0], tn), lambda l, j: (l, 0, j)),
        out_shape=jax.ShapeDtypeStruct((depth, cc.shape[0], W), F32),
        compiler_params=pltpu.CompilerParams(dimension_semantics=("parallel", "parallel"), vmem_limit_bytes=VMEM_LIMIT),
        name="ada",
    )(cc, ada_w, ada_b.reshape(depth, 1, W))


def _norm_body(x_ref, mod_ref, g_ref, o_ref, *, si):
    h = _norm_mod(x_ref[...], g_ref[...], mod_ref[si:si + 1, :], mod_ref[si + 1:si + 2, :])
    o_ref[...] = h.astype(o_ref.dtype)


def _mla_a_body(x_ref, mod_ref, g_ref, wa_ref, qan_ref, kvn_ref, cq_ref, ckv_ref, kr_ref):
    h = _norm_mod(x_ref[...], g_ref[...], mod_ref[0:1, :], mod_ref[1:2, :]).astype(BF16)
    acc = _dot(h, wa_ref[...])
    qa = acc[:, 0:Q_LORA]
    kv = acc[:, Q_LORA:Q_LORA + KV_LORA]
    cq_ref[...] = (qa * _rms(qa, Q_LORA) * qan_ref[...]).astype(BF16)
    ckv_ref[...] = (kv * _rms(kv, KV_LORA) * kvn_ref[...]).astype(BF16)
    kr_ref[...] = acc[:, Q_LORA + KV_LORA:]


def _mla_q_body(cq_ref, cos_ref, sin_ref, wqb_ref, gq_ref, q_ref):
    acc = _dot(cq_ref[...], wqb_ref[...])
    cos = cos_ref[...]
    sin = sin_ref[...]
    g = gq_ref[...]
    for h in range(MLA_HEADS):
        a = acc[:, QK_PAD * h:QK_PAD * (h + 1)]
        an = a * _rms(a, QK_HEAD) * g
        q_ref[:, QK_PAD * h:QK_PAD * h + LANES] = an[:, :LANES].astype(BF16)
        q_ref[:, QK_PAD * h + LANES:QK_PAD * (h + 1)] = _rope(an[:, LANES:], cos, sin).astype(BF16)


def _mla_kv_body(ckv_ref, kr_ref, cos_ref, sin_ref, wk_ref, wv_ref, gk_ref, k_ref, v_ref):
    ckv = ckv_ref[...]
    kn = _dot(ckv, wk_ref[...])
    v_ref[...] = _dot(ckv, wv_ref[...]).astype(BF16)
    kr = kr_ref[...]
    ss_rope = jnp.sum(kr * kr, axis=-1, keepdims=True)
    g = gk_ref[...]
    g_nope = g[:, :LANES]
    kr_rot = _rope(kr * g[:, LANES:], cos_ref[...], sin_ref[...])
    for h in range(MLA_HEADS):
        a = kn[:, LANES * h:LANES * (h + 1)]
        inv = lax.rsqrt((jnp.sum(a * a, axis=-1, keepdims=True) + ss_rope) * (1.0 / QK_HEAD) + NORM_EPS)
        k_ref[:, QK_PAD * h:QK_PAD * h + LANES] = (a * inv * g_nope).astype(BF16)
        k_ref[:, QK_PAD * h + LANES:QK_PAD * (h + 1)] = (kr_rot * inv).astype(BF16)


def _attn_body(*refs, nseg):
    q = refs[0][...]
    o_ref = refs[-1]
    ss = [lax.dot_general(q, refs[1 + 2 * i][...], _NT, preferred_element_type=F32) for i in range(nseg)]
    m = functools.reduce(jnp.maximum, [jnp.max(s, axis=-1, keepdims=True) for s in ss])
    ps = [jnp.exp(s - m) for s in ss]
    l = functools.reduce(lambda a, b: a + b, [jnp.sum(p, axis=-1, keepdims=True) for p in ps])
    acc = functools.reduce(lambda a, b: a + b,
                           [_dot(ps[i].astype(BF16), refs[2 + 2 * i][...]) for i in range(nseg)])
    o_ref[...] = (acc / l).astype(o_ref.dtype)


def _attention(q, k, v, lay, tq):
    B, T, L = lay.B, lay.T, lay.L
    nq = T // tq
    ctx0 = (B * T) // L
    params = pltpu.CompilerParams(dimension_semantics=("parallel", "parallel", "arbitrary"),
                                  vmem_limit_bytes=VMEM_LIMIT)
    o_lat = pl.pallas_call(
        functools.partial(_attn_body, nseg=2), grid=(B, MLA_HEADS, nq),
        in_specs=[pl.BlockSpec((tq, QK_PAD), lambda b, h, i: (b * nq + i, h)),
                  pl.BlockSpec((L, QK_PAD), lambda b, h, i: (ctx0 + b, h)),
                  pl.BlockSpec((L, V_HEAD), lambda b, h, i: (ctx0 + b, h)),
                  pl.BlockSpec((T, QK_PAD), lambda b, h, i: (b, h)),
                  pl.BlockSpec((T, V_HEAD), lambda b, h, i: (b, h))],
        out_specs=pl.BlockSpec((tq, V_HEAD), lambda b, h, i: (b * nq + i, h)),
        out_shape=jax.ShapeDtypeStruct((B * T, MLA_HEADS * V_HEAD), BF16),
        compiler_params=params, name="attn_latent",
    )(q, k, v, k, v)
    o_ctx = pl.pallas_call(
        functools.partial(_attn_body, nseg=1), grid=(B, MLA_HEADS, 1),
        in_specs=[pl.BlockSpec((L, QK_PAD), lambda b, h, i: (ctx0 + b, h)),
                  pl.BlockSpec((L, QK_PAD), lambda b, h, i: (ctx0 + b, h)),
                  pl.BlockSpec((L, V_HEAD), lambda b, h, i: (ctx0 + b, h))],
        out_specs=pl.BlockSpec((L, V_HEAD), lambda b, h, i: (b, h)),
        out_shape=jax.ShapeDtypeStruct((B * L, MLA_HEADS * V_HEAD), BF16),
        compiler_params=params, name="attn_ctx",
    )(q, k, v)
    return jnp.concatenate([o_lat, o_ctx], axis=0)


def _proj_resid_body(a_ref, x_ref, mod_ref, w_ref, o_ref):
    o_ref[...] = x_ref[...] + mod_ref[2:3, :] * _dot(a_ref[...], w_ref[...])


def _router_body(x_ref, mod_ref, g_ref, router_ref, h_ref, gates_ref):
    hf = _norm_mod(x_ref[...], g_ref[...], mod_ref[3:4, :], mod_ref[4:5, :])
    h_ref[...] = hf.astype(BF16)
    logits = jnp.dot(hf, router_ref[...], precision=HIGHEST, preferred_element_type=F32)
    lane = lax.broadcasted_iota(jnp.int32, logits.shape, 1)
    lanef = lane.astype(F32)
    neg = jnp.float32(-1e30)
    lg = jnp.where(lane < N_EXPERTS, logits, neg)
    m1 = jnp.max(lg, axis=-1, keepdims=True)
    i1 = jnp.min(jnp.where(lg == m1, lanef, float(LANES)), axis=-1, keepdims=True)
    lg2 = jnp.where(lanef == i1, neg, lg)
    m2 = jnp.max(lg2, axis=-1, keepdims=True)
    i2 = jnp.min(jnp.where(lg2 == m2, lanef, float(LANES)), axis=-1, keepdims=True)
    e2 = jnp.exp(m2 - m1)
    w1 = 1.0 / (1.0 + e2)
    w2 = e2 / (1.0 + e2)
    gates_ref[...] = jnp.where(lanef == i1, w1, 0.0) + jnp.where(lanef == i2, w2, 0.0)


def _ffn_body(*refs, moe):
    if moe:
        h_ref, gates_ref, w1_ref, w3_ref, w2_ref, x_ref, mod_ref, o_ref, acc_ref = refs
    else:
        h_ref, w1_ref, w3_ref, w2_ref, x_ref, mod_ref, o_ref, acc_ref = refs
    e = pl.program_id(1)
    f = pl.program_id(2)

    @pl.when((e == 0) & (f == 0))
    def _():
        acc_ref[...] = jnp.zeros_like(acc_ref)

    h = h_ref[...]
    a1 = _dot(h, w1_ref[...])
    a3 = _dot(h, w3_ref[...])
    hid = a1 * _sigmoid(a1) * a3
    if moe:
        gates = gates_ref[...]
        lane = lax.broadcasted_iota(jnp.int32, gates.shape, 1)
        hid = hid * jnp.sum(jnp.where(lane == e, gates, 0.0), axis=-1, keepdims=True)
    acc_ref[...] += _dot(hid.astype(BF16), w2_ref[...])

    @pl.when((e == pl.num_programs(1) - 1) & (f == pl.num_programs(2) - 1))
    def _():
        o_ref[...] = x_ref[...] + mod_ref[5:6, :] * acc_ref[...]


def _ffn(h, gates, w1, w3, w2, xf, modtab, lay, tm, tf):
    E, D, F = w1.shape
    moe = gates is not None
    n_lat_tiles = lay.n_lat // tm
    T, B = lay.T, lay.B
    row = lambda m, e, f: (m, 0)
    in_specs = [pl.BlockSpec((tm, D), row)]
    args = [h]
    if moe:
        in_specs.append(pl.BlockSpec((tm, LANES), row))
        args.append(gates)
    in_specs += [pl.BlockSpec((None, D, tf), lambda m, e, f: (e, 0, f)),
                 pl.BlockSpec((None, D, tf), lambda m, e, f: (e, 0, f)),
                 pl.BlockSpec((None, tf, D), lambda m, e, f: (e, f, 0)),
                 pl.BlockSpec((tm, D), row),
                 pl.BlockSpec((None, 6, D), lambda m, e, f: (jnp.where(m < n_lat_tiles, (m * tm) // T, B), 0, 0))]
    args += [w1, w3, w2, xf, modtab]
    return pl.pallas_call(
        functools.partial(_ffn_body, moe=moe), grid=(lay.n // tm, E, F // tf),
        in_specs=in_specs, out_specs=pl.BlockSpec((tm, D), row),
        out_shape=jax.ShapeDtypeStruct((lay.n, D), F32),
        scratch_shapes=[pltpu.VMEM((tm, D), F32)],
        compiler_params=pltpu.CompilerParams(dimension_semantics=("parallel", "arbitrary", "arbitrary"),
                                             vmem_limit_bytes=VMEM_LIMIT),
        name="moe_ffn" if moe else "dense_ffn",
    )(*args)


def _rwkv_in_body(*refs, vres):
    if vres:
        (h_ref, xx_ref, mix_ref, wr_ref, wk_ref, wv_ref, w1_ref, a1_ref, g1_ref, v1_ref,
         r_ref, k_ref, v_ref, tw_ref, ya_ref, sg_ref, yv_ref) = refs
    else:
        (h_ref, xx_ref, mix_ref, wr_ref, wk_ref, wv_ref, w1_ref, a1_ref, g1_ref,
         r_ref, k_ref, v_ref, tw_ref, ya_ref, sg_ref) = refs
    h = h_ref[...]
    xx = xx_ref[...]

    def mixed(j):
        return (h + xx * mix_ref[j:j + 1, :]).astype(BF16)

    r_ref[...] = _dot(mixed(0), wr_ref[...])
    tw_ref[...] = jnp.tanh(_dot(mixed(1), w1_ref[...]))
    k_ref[...] = _dot(mixed(2), wk_ref[...])
    xv = mixed(3)
    v_ref[...] = _dot(xv, wv_ref[...])
    if vres:
        yv_ref[...] = _dot(xv, v1_ref[...])
    ya_ref[...] = _dot(mixed(4), a1_ref[...])
    sg_ref[...] = _sigmoid(_dot(mixed(5), g1_ref[...]))


def _rwkv_feat_body(*refs, vres):
    if vres:
        (k_ref, v_ref, tw_ref, ya_ref, sg_ref, yv_ref, vf_ref, w2_ref, a2_ref, g2_ref, w0_ref, a0_ref, kk_ref_w,
         ones_ref, v2_ref, v0_ref, lw_ref, ad_ref, kk_ref, vo_ref, g_ref) = refs
    else:
        (k_ref, v_ref, tw_ref, ya_ref, sg_ref, w2_ref, a2_ref, g2_ref, w0_ref, a0_ref, kk_ref_w,
         ones_ref, lw_ref, ad_ref, kk_ref, vo_ref, g_ref) = refs
    dec = w0_ref[...] + _dot(tw_ref[...].astype(BF16), w2_ref[...])
    w_log = -_softplus(-dec) - 0.5
    lw_ref[...] = -jnp.exp(w_log)
    ad_ref[...] = _sigmoid(a0_ref[...] + _dot(ya_ref[...].astype(BF16), a2_ref[...]))
    g_ref[...] = _dot(sg_ref[...].astype(BF16), g2_ref[...])
    kkr = k_ref[...] * kk_ref_w[...]
    ss = _head_sum(kkr * kkr, ones_ref[...])
    kk_ref[...] = kkr / jnp.maximum(jnp.sqrt(ss), 1e-12)
    v = v_ref[...]
    if vres:
        v = v + (vf_ref[...] - v) * _sigmoid(v0_ref[...] + _dot(yv_ref[...].astype(BF16), v2_ref[...]))
    vo_ref[...] = v


def _scan_body(lw_ref, ad_ref, kf_ref, kk_ref, v_ref, r_ref, ka_ref, y_ref, s_ref):
    d = pl.program_id(1)
    s = pl.program_id(2)
    C = CHUNK
    D = kf_ref.shape[1]

    @pl.when(s == 0)
    def _():
        s_ref[...] = jnp.zeros_like(s_ref)

    sg = jnp.where(d == 0, 1, -1)
    ri = lax.broadcasted_iota(jnp.int32, (C, C), 0)
    ci = lax.broadcasted_iota(jnp.int32, (C, C), 1)
    dm = (ri - ci) * sg
    xr = ri ^ ci
    eye = (ri == ci).astype(F32)
    r2 = lax.broadcasted_iota(jnp.int32, (C, 2 * C), 0)
    c2 = lax.broadcasted_iota(jnp.int32, (C, 2 * C), 1) & (C - 1)
    dm2 = (r2 - c2) * sg
    strict2 = dm2 > 0
    incl2 = dm2 >= 0

    lw = lw_ref[...]
    ad = ad_ref[...]
    kf = kf_ref[...]
    kk = kk_ref[...]
    r = r_ref[...]
    c = jnp.dot((dm >= 0).astype(F32), lw, precision=HIGHEST, preferred_element_type=F32)
    pc = jnp.where(d == 0, c[C - 1:C, :], c[0:1, :])
    kd = kf * (1.0 + (ad - 1.0) * ka_ref[...])
    b = kk * ad
    einv = jnp.exp(-c)
    a_t = (-kk * jnp.exp(c - lw)).astype(BF16)
    r_t = (r * jnp.exp(c)).astype(BF16)
    b_t = (b * einv).astype(BF16)
    k_t = (kd * einv).astype(BF16)
    epc = jnp.exp(pc - c)
    b_p = (b * epc).astype(BF16)
    k_p = (kd * epc).astype(BF16)
    d_p = jnp.exp(pc)
    vb = v_ref[...].astype(BF16)
    zeros_cc = jnp.zeros((C, C), BF16)

    for h in range(D // C):
        sl = slice(C * h, C * (h + 1))
        ar = jnp.concatenate([a_t[:, sl], r_t[:, sl]], axis=0)
        bk = jnp.concatenate([b_t[:, sl], k_t[:, sl]], axis=0)
        v = vb[:, sl]
        sc = lax.dot_general(ar, bk, _NT, preferred_element_type=F32)
        top = jnp.where(strict2, sc[0:C], 0.0)
        bot = jnp.where(incl2, sc[C:], 0.0)
        lab = top[:, 0:C]
        t = eye + jnp.where(xr == 1, lab, 0.0)
        for lb in range(1, 6):
            off = jnp.where((xr >> lb) == 1, lab, 0.0).astype(BF16)
            tb = t.astype(BF16)
            t = t + _dot(tb, _dot(off, tb).astype(BF16))
        st = s_ref[h]
        ars = lax.dot_general(ar, st.astype(BF16), _NT, preferred_element_type=F32)
        x = ars[0:C] + _dot(top.astype(BF16), jnp.concatenate([zeros_cc, v], axis=0))
        u = _dot(t.astype(BF16), x.astype(BF16))
        uv = jnp.concatenate([u.astype(BF16), v], axis=0)
        y_ref[:, sl] = ars[C:] + _dot(bot.astype(BF16), uv)
        bkp = jnp.concatenate([b_p[:, sl], k_p[:, sl]], axis=0)
        s_ref[h] = st * d_p[:, sl] + lax.dot_general(uv, bkp, _TN, preferred_element_type=F32)


def _scan(lw, ad, kf, kk, v, r, k_a, lay):
    B, T, L, D = lay.B, lay.T, lay.L, lay.D
    nc_ctx = L // CHUNK
    nc_lat = T // CHUNK
    ctx_base = (B * T) // CHUNK

    def chunk_index(b, d, s):
        ctx_i = jnp.where(d == 0, s, nc_ctx - 1 - s)
        lat_i = jnp.where(d == 0, s - nc_ctx, nc_lat - 1 - (s - nc_ctx))
        return jnp.where(s < nc_ctx, ctx_base + b * nc_ctx + ctx_i, b * nc_lat + lat_i)

    per_dir = pl.BlockSpec((CHUNK, D), lambda b, d, s: (chunk_index(b, d, s), d))
    shared = pl.BlockSpec((CHUNK, D), lambda b, d, s: (chunk_index(b, d, s), 0))
    return pl.pallas_call(
        _scan_body, grid=(B, 2, nc_ctx + nc_lat),
        in_specs=[per_dir, per_dir, shared, shared, shared, shared,
                  pl.BlockSpec((1, D), lambda b, d, s: (0, 0))],
        out_specs=per_dir,
        out_shape=jax.ShapeDtypeStruct((lay.n, 2 * D), F32),
        scratch_shapes=[pltpu.VMEM((D // CHUNK, CHUNK, CHUNK), F32)],
        compiler_params=pltpu.CompilerParams(dimension_semantics=("parallel", "parallel", "arbitrary"),
                                             vmem_limit_bytes=VMEM_LIMIT),
        name="wkv_scan",
    )(lw, ad, kf, kk, v, r, k_a)


def _rwkv_post_body(y2_ref, r_ref, kf_ref, ad_ref, v_ref, g_ref, x_ref, mod_ref,
                    ka_ref, rk_ref, lnw_ref, lnb_ref, ones_ref, wo_ref, o_ref):
    D = x_ref.shape[1]
    ones4 = ones_ref[...]
    y2 = y2_ref[...]
    y = y2[:, :D] + y2[:, D:]
    inv_n = 1.0 / RWKV_HEAD
    yc = y - _head_sum(y, ones4) * inv_n
    var = _head_sum(yc * yc, ones4) * inv_n
    yn = yc * lax.rsqrt(var + LN_X_EPS) * lnw_ref[...] + lnb_ref[...]
    kf = kf_ref[...]
    ad = ad_ref[...]
    ka = ka_ref[...]
    k_bonus = 0.5 * (kf * (1.0 + (ad[:, :D] - 1.0) * ka) + kf * (1.0 + (ad[:, D:] - 1.0) * ka))
    bonus = _head_sum(r_ref[...] * k_bonus * rk_ref[...], ones4) * v_ref[...]
    o = ((yn + bonus) * g_ref[...]).astype(BF16)
    o_ref[...] = x_ref[...] + mod_ref[2:3, :] * _dot(o, wo_ref[...])


def _token_shift(h, lay):
    def centred(z):
        zp = jnp.pad(z, ((0, 0), (1, 1), (0, 0)))
        return 0.5 * (zp[:, :-2] + zp[:, 2:]) - z
    B, T, L, D = lay.B, lay.T, lay.L, lay.D
    hl = centred(h[:lay.n_lat].reshape(B, T, D)).reshape(B * T, D)
    hc = centred(h[lay.n_lat:].reshape(B, L, D)).reshape(B * L, D)
    return jnp.concatenate([hl, hc], axis=0)


def _rope_tables(T, tm):
    pos = jnp.arange(T)
    inv_freq = 1.0 / (ROPE_BASE ** (jnp.arange(AXIS_PAIRS, dtype=F32) / AXIS_PAIRS))
    ang_r = (pos // GRID_W).astype(F32)[:, None] * inv_freq[None, :]
    ang_c = (pos % GRID_W).astype(F32)[:, None] * inv_freq[None, :]
    pad1 = jnp.ones((T, LANES - QK_ROPE), F32)
    pad0 = jnp.zeros((T, LANES - QK_ROPE), F32)
    cos = jnp.concatenate([jnp.cos(ang_r), jnp.cos(ang_r), jnp.cos(ang_c), jnp.cos(ang_c), pad1], axis=1)
    sin = jnp.concatenate([-jnp.sin(ang_r), jnp.sin(ang_r), -jnp.sin(ang_c), jnp.sin(ang_c), pad0], axis=1)
    cos = jnp.concatenate([cos, jnp.ones((tm, LANES), F32)], axis=0)
    sin = jnp.concatenate([sin, jnp.zeros((tm, LANES), F32)], axis=0)
    return cos, sin


def _pad_cols(w, width):
    return jnp.pad(w, ((0, 0), (0, width - w.shape[1])))


def _pad_rows(w, height):
    return jnp.pad(w, ((0, height - w.shape[0]), (0, 0)))


def _block_diag2(w0, w1):
    z0 = jnp.zeros_like(w0)
    z1 = jnp.zeros_like(w1)
    return jnp.concatenate([jnp.concatenate([w0, z1], axis=1), jnp.concatenate([z0, w1], axis=1)], axis=0)


def kernel(x, c, ctx, c_ctx, ada_w, ada_b, norm1_g, norm2_g, mla_wqa, mla_qa_norm, mla_wqb, mla_wkva, mla_kva_norm, mla_wkvb, mla_q_norm, mla_k_norm, mla_wo, rwkv_mix, rwkv_wr, rwkv_wk, rwkv_wv, rwkv_wo, rwkv_w0, rwkv_w1, rwkv_w2, rwkv_a0, rwkv_a1, rwkv_a2, rwkv_g1, rwkv_g2, rwkv_k_k, rwkv_k_a, rwkv_r_k, rwkv_ln_w, rwkv_ln_b, rwkv_v0, rwkv_v1, rwkv_v2, ffn_w1, ffn_w3, ffn_w2, moe_router, moe_w1, moe_w3, moe_w2):
    B, T, D = x.shape
    L = ctx.shape[1]
    depth = ada_w.shape[0]
    lay = _Layout(B, T, L, D)
    n = lay.n
    tm = min(512, T)
    tm_small = min(256, T)
    assert T % tm == 0 and (B * L) % tm == 0 and T % GRID_W == 0 and L % CHUNK == 0 and B + 1 <= 16

    xf = jnp.concatenate([x.reshape(B * T, D), ctx.reshape(B * L, D)], axis=0)
    cc = jnp.concatenate([c, c_ctx[None, :], jnp.zeros((16 - B - 1, D), F32)], axis=0)
    modall = _ada_call(cc, ada_w, ada_b)[:, :B + 1].reshape(depth, B + 1, 6, D)
    cos_t, sin_t = _rope_tables(T, tm)
    ones4 = jnp.kron(jnp.eye(4, dtype=F32), jnp.ones((RWKV_HEAD, RWKV_HEAD), F32)).astype(BF16)
    row1 = lambda z: z.reshape(1, -1)

    v_first = None
    for i in range(depth):
        j = i // 2
        modtab = modall[i]
        mod_t = (modtab, lay.mod_spec(tm))
        mod_s = (modtab, lay.mod_spec(tm_small))
        if i % 2 == 0:
            wa = jnp.concatenate([mla_wqa[j], _pad_cols(mla_wkva[j], KV_LORA + LANES)], axis=1).astype(BF16)
            wqb = mla_wqb[j].reshape(Q_LORA, MLA_HEADS, QK_HEAD)
            wqb = jnp.pad(wqb, ((0, 0), (0, 0), (0, QK_PAD - QK_HEAD))).reshape(Q_LORA, MLA_HEADS * QK_PAD).astype(BF16)
            wkvb = mla_wkvb[j].reshape(KV_LORA, MLA_HEADS, QK_NOPE + V_HEAD)
            wk = wkvb[:, :, :QK_NOPE].reshape(KV_LORA, MLA_HEADS * QK_NOPE).astype(BF16)
            wv = wkvb[:, :, QK_NOPE:].reshape(KV_LORA, MLA_HEADS * V_HEAD).astype(BF16)
            gq = _pad_cols(row1(mla_q_norm[j]) * SM_SCALE, QK_PAD)
            gk = _pad_cols(row1(mla_k_norm[j]), QK_PAD)
            cq, ckv, kr = _rowwise(
                _mla_a_body, "mla_a", n, tm, [xf], [mod_t],
                [row1(norm1_g[i]), wa, row1(mla_qa_norm[j]), row1(mla_kva_norm[j])],
                [(Q_LORA, BF16), (KV_LORA, BF16), (LANES, F32)])
            rope_in = [(cos_t, lay.rope_spec(tm)), (sin_t, lay.rope_spec(tm))]
            (q,) = _rowwise(_mla_q_body, "mla_q", n, tm, [cq], rope_in, [wqb, gq], [(MLA_HEADS * QK_PAD, BF16)])
            k, v = _rowwise(_mla_kv_body, "mla_kv", n, tm, [ckv, kr], rope_in, [wk, wv, gk],
                            [(MLA_HEADS * QK_PAD, BF16), (MLA_HEADS * V_HEAD, BF16)])
            o = _attention(q, k, v, lay, tq=min(512, T))
            (xf,) = _rowwise(_proj_resid_body, "mla_out", n, tm, [o, xf], [mod_t], [mla_wo[j].astype(BF16)], [(D, F32)])
        else:
            vres = j > 0
            (h,) = _rowwise(functools.partial(_norm_body, si=0), "norm1", n, tm, [xf], [mod_t],
                            [row1(norm1_g[i])], [(D, F32)])
            xx = _token_shift(h, lay)
            mix = _pad_rows(rwkv_mix[j], 8)
            w1c = jnp.concatenate([rwkv_w1[j, 0], rwkv_w1[j, 1]], axis=1).astype(BF16)
            a1c = jnp.concatenate([rwkv_a1[j, 0], rwkv_a1[j, 1]], axis=1).astype(BF16)
            g1p = _pad_cols(rwkv_g1[j], 2 * LANES).astype(BF16)
            consts = [mix, rwkv_wr[j].astype(BF16), rwkv_wk[j].astype(BF16), rwkv_wv[j].astype(BF16), w1c, a1c, g1p]
            outs = [(D, F32), (D, F32), (D, F32), (LANES, F32), (LANES, F32), (2 * LANES, F32)]
            if vres:
                consts.append(_pad_cols(rwkv_v1[j - 1], LANES).astype(BF16))
                outs.append((LANES, F32))
            res = _rowwise(functools.partial(_rwkv_in_body, vres=vres), "rwkv_in", n, tm_small, [h, xx], [], consts, outs)
            r, kf, v_raw, tw, ya, sgate = res[:6]
            w2c = _block_diag2(rwkv_w2[j, 0], rwkv_w2[j, 1]).astype(BF16)
            a2c = _block_diag2(rwkv_a2[j, 0], rwkv_a2[j, 1]).astype(BF16)
            g2p = _pad_rows(rwkv_g2[j], 2 * LANES).astype(BF16)
            rows = [kf, v_raw, tw, ya, sgate]
            consts = [w2c, a2c, g2p, rwkv_w0[j].reshape(1, 2 * D), rwkv_a0[j].reshape(1, 2 * D), row1(rwkv_k_k[j]), ones4]
            if vres:
                rows += [res[6], v_first]
                consts += [_pad_rows(rwkv_v2[j - 1], LANES).astype(BF16), row1(rwkv_v0[j - 1])]
            lw, ad, kk, v_out, gate = _rowwise(
                functools.partial(_rwkv_feat_body, vres=vres), "rwkv_feat", n, tm_small, rows, [], consts,
                [(2 * D, F32), (2 * D, F32), (D, F32), (D, F32), (D, F32)])
            if j == 0:
                v_first = v_out
            k_a = row1(rwkv_k_a[j])
            y2 = _scan(lw, ad, kf, kk, v_out, r, k_a, lay)
            (xf,) = _rowwise(
                _rwkv_post_body, "rwkv_out", n, tm_small, [y2, r, kf, ad, v_out, gate, xf], [mod_s],
                [k_a, row1(rwkv_r_k[j]), row1(rwkv_ln_w[j]), row1(rwkv_ln_b[j]), ones4, rwkv_wo[j].astype(BF16)],
                [(D, F32)])

        if i % 2 == 0:
            (h,) = _rowwise(functools.partial(_norm_body, si=3), "norm2", n, tm, [xf], [mod_t],
                            [row1(norm2_g[i])], [(D, BF16)])
            w1 = ffn_w1[j][None].astype(BF16)
            w3 = ffn_w3[j][None].astype(BF16)
            w2 = ffn_w2[j][None].astype(BF16)
            xf = _ffn(h, None, w1, w3, w2, xf, modtab, lay, tm, tf=w1.shape[2] // 2)
        else:
            h, gates = _rowwise(_router_body, "router", n, tm, [xf], [mod_t],
                                [row1(norm2_g[i]), _pad_cols(moe_router[j], LANES)], [(D, BF16), (LANES, F32)])
            xf = _ffn(h, gates, moe_w1[j].astype(BF16), moe_w3[j].astype(BF16), moe_w2[j].astype(BF16),
                      xf, modtab, lay, tm, tf=moe_w1.shape[3] // 4)
    return xf[:B * T].reshape(B, T, D)
```

```python
import functools

import jax
import jax.numpy as jnp
from jax import lax
from jax.experimental import pallas as pl
from jax.experimental.pallas import tpu as pltpu

F32 = jnp.float32
BF16 = jnp.bfloat16
HIGHEST = lax.Precision.HIGHEST

GRID_W = 64
NORM_EPS = 1e-6
MLA_HEADS = 8
QK_NOPE = 128
QK_ROPE = 64
V_HEAD = 128
QK_HEAD = QK_NOPE + QK_ROPE
Q_LORA = 384
KV_LORA = 256
ROPE_BASE = 10000.0
AXIS_PAIRS = QK_ROPE // 4
SM_SCALE = QK_HEAD ** -0.5
RWKV_HEAD = 64
LN_X_EPS = 64e-5
N_EXPERTS = 8

LANES = 128
QK_PAD = 2 * LANES
CHUNK = 64
VMEM_LIMIT = 56 * 1024 * 1024

_NT = (((1,), (1,)), ((), ()))
_TN = (((0,), (0,)), ((), ()))


def _dot(a, b):
    return jnp.dot(a, b, preferred_element_type=F32)


def _sigmoid(z):
    return 1.0 / (1.0 + jnp.exp(-z))


def _softplus(z):
    return jnp.maximum(z, 0.0) + jnp.log(1.0 + jnp.exp(-jnp.abs(z)))


def _rms(z, width):
    return lax.rsqrt(jnp.sum(z * z, axis=-1, keepdims=True) * (1.0 / width) + NORM_EPS)


def _norm_mod(x, g, shift, scale):
    return (x * _rms(x, x.shape[-1])) * g * (1.0 + scale) + shift


def _rope(z, cos, sin):
    lane = lax.broadcasted_iota(jnp.int32, z.shape, 1)
    partner = jnp.where((lane & 16) == 0, pltpu.roll(z, LANES - 16, axis=1), pltpu.roll(z, 16, axis=1))
    return z * cos + partner * sin


def _head_sum(z, ones4):
    outs = []
    for q in range(z.shape[1] // 256):
        blk = z[:, 256 * q:256 * (q + 1)]
        hi = blk.astype(BF16)
        lo = (blk - hi.astype(F32)).astype(BF16)
        outs.append(_dot(hi, ones4) + _dot(lo, ones4))
    return jnp.concatenate(outs, axis=1)


class _Layout:
    def __init__(self, B, T, L, D):
        self.B, self.T, self.L, self.D = B, T, L, D
        self.n_lat = B * T
        self.n = B * T + B * L

    def mod_spec(self, tm):
        n_lat_tiles = self.n_lat // tm
        T, B, D = self.T, self.B, self.D
        return pl.BlockSpec((None, 6, D), lambda i: (jnp.where(i < n_lat_tiles, (i * tm) // T, B), 0, 0))

    def rope_spec(self, tm):
        n_lat_tiles = self.n_lat // tm
        tpb = self.T // tm
        return pl.BlockSpec((tm, LANES), lambda i: (jnp.where(i < n_lat_tiles, i % tpb, tpb), 0))


def _const_spec(arr):
    nd = arr.ndim
    return pl.BlockSpec(arr.shape, lambda i: (0,) * nd, pipeline_mode=pl.Buffered(1))


def _rowwise(body, name, n_rows, tm, row_ins, tile_ins, const_ins, outs):
    in_specs = [pl.BlockSpec((tm, a.shape[1]), lambda i: (i, 0)) for a in row_ins]
    in_specs += [spec for (_, spec) in tile_ins]
    in_specs += [_const_spec(a) for a in const_ins]
    out_specs = [pl.BlockSpec((tm, w), lambda i: (i, 0)) for (w, _) in outs]
    out_shape = [jax.ShapeDtypeStruct((n_rows, w), dt) for (w, dt) in outs]
    return pl.pallas_call(
        body, grid=(n_rows // tm,), in_specs=in_specs, out_specs=out_specs, out_shape=out_shape,
        compiler_params=pltpu.CompilerParams(dimension_semantics=("parallel",), vmem_limit_bytes=VMEM_LIMIT),
        name=name,
    )(*row_ins, *[a for (a, _) in tile_ins], *const_ins)


def _ada_body(c_ref, w_ref, b_ref, o_ref):
    c = c_ref[...]
    o_ref[...] = jnp.dot(c * _sigmoid(c), w_ref[...], precision=HIGHEST, preferred_element_type=F32) + b_ref[...]


def _ada_call(cc, ada_w, ada_b):
    depth, D, W = ada_w.shape
    tn = 1024
    return pl.pallas_call(
        _ada_body, grid=(depth, W // tn),
        in_specs=[pl.BlockSpec(cc.shape, lambda l, j: (0, 0)),
                  pl.BlockSpec((None, D, tn), lambda l, j: (l, 0, j)),
                  pl.BlockSpec((None, 1, tn), lambda l, j: (l, 0, j))],
        out_specs=pl.BlockSpec((None, cc.shape[0], tn), lambda l, j: (l, 0, j)),
        out_shape=jax.ShapeDtypeStruct((depth, cc.shape[0], W), F32),
        compiler_params=pltpu.CompilerParams(dimension_semantics=("parallel", "parallel"), vmem_limit_bytes=VMEM_LIMIT),
        name="ada",
    )(cc, ada_w, ada_b.reshape(depth, 1, W))


def _norm_body(x_ref, mod_ref, g_ref, o_ref, *, si):
    h = _norm_mod(x_ref[...], g_ref[...], mod_ref[si:si + 1, :], mod_ref[si + 1:si + 2, :])
    o_ref[...] = h.astype(o_ref.dtype)


def _mla_a_body(x_ref, mod_ref, g_ref, wa_ref, qan_ref, kvn_ref, cq_ref, ckv_ref, kr_ref):
    h = _norm_mod(x_ref[...], g_ref[...], mod_ref[0:1, :], mod_ref[1:2, :]).astype(BF16)
    acc = _dot(h, wa_ref[...])
    qa = acc[:, 0:Q_LORA]
    kv = acc[:, Q_LORA:Q_LORA + KV_LORA]
    cq_ref[...] = (qa * _rms(qa, Q_LORA) * qan_ref[...]).astype(BF16)
    ckv_ref[...] = (kv * _rms(kv, KV_LORA) * kvn_ref[...]).astype(BF16)
    kr_ref[...] = acc[:, Q_LORA + KV_LORA:]


def _mla_q_body(cq_ref, cos_ref, sin_ref, wqb_ref, gq_ref, q_ref):
    acc = _dot(cq_ref[...], wqb_ref[...])
    cos = cos_ref[...]
    sin = sin_ref[...]
    g = gq_ref[...]
    for h in range(MLA_HEADS):
        a = acc[:, QK_PAD * h:QK_PAD * (h + 1)]
        an = a * _rms(a, QK_HEAD) * g
        q_ref[:, QK_PAD * h:QK_PAD * h + LANES] = an[:, :LANES].astype(BF16)
        q_ref[:, QK_PAD * h + LANES:QK_PAD * (h + 1)] = _rope(an[:, LANES:], cos, sin).astype(BF16)


def _mla_kv_body(ckv_ref, kr_ref, cos_ref, sin_ref, wk_ref, wv_ref, gk_ref, k_ref, v_ref):
    ckv = ckv_ref[...]
    kn = _dot(ckv, wk_ref[...])
    v_ref[...] = _dot(ckv, wv_ref[...]).astype(BF16)
    kr = kr_ref[...]
    ss_rope = jnp.sum(kr * kr, axis=-1, keepdims=True)
    g = gk_ref[...]
    g_nope = g[:, :LANES]
    kr_rot = _rope(kr * g[:, LANES:], cos_ref[...], sin_ref[...])
    for h in range(MLA_HEADS):
        a = kn[:, LANES * h:LANES * (h + 1)]
        inv = lax.rsqrt((jnp.sum(a * a, axis=-1, keepdims=True) + ss_rope) * (1.0 / QK_HEAD) + NORM_EPS)
        k_ref[:, QK_PAD * h:QK_PAD * h + LANES] = (a * inv * g_nope).astype(BF16)
        k_ref[:, QK_PAD * h + LANES:QK_PAD * (h + 1)] = (kr_rot * inv).astype(BF16)


def _attn_body(*refs, nseg):
    q = refs[0][...]
    o_ref = refs[-1]
    ss = [lax.dot_general(q, refs[1 + 2 * i][...], _NT, preferred_element_type=F32) for i in range(nseg)]
    m = functools.reduce(jnp.maximum, [jnp.max(s, axis=-1, keepdims=True) for s in ss])
    ps = [jnp.exp(s - m) for s in ss]
    l = functools.reduce(lambda a, b: a + b, [jnp.sum(p, axis=-1, keepdims=True) for p in ps])
    acc = functools.reduce(lambda a, b: a + b,
                           [_dot(ps[i].astype(BF16), refs[2 + 2 * i][...]) for i in range(nseg)])
    o_ref[...] = (acc / l).astype(o_ref.dtype)


def _attention(q, k, v, lay, tq):
    B, T, L = lay.B, lay.T, lay.L
    nq = T // tq
    ctx0 = (B * T) // L
    params = pltpu.CompilerParams(dimension_semantics=("parallel", "parallel", "arbitrary"),
                                  vmem_limit_bytes=VMEM_LIMIT)
    o_lat = pl.pallas_call(
        functools.partial(_attn_body, nseg=2), grid=(B, MLA_HEADS, nq),
        in_specs=[pl.BlockSpec((tq, QK_PAD), lambda b, h, i: (b * nq + i, h)),
                  pl.BlockSpec((L, QK_PAD), lambda b, h, i: (ctx0 + b, h)),
                  pl.BlockSpec((L, V_HEAD), lambda b, h, i: (ctx0 + b, h)),
                  pl.BlockSpec((T, QK_PAD), lambda b, h, i: (b, h)),
                  pl.BlockSpec((T, V_HEAD), lambda b, h, i: (b, h))],
        out_specs=pl.BlockSpec((tq, V_HEAD), lambda b, h, i: (b * nq + i, h)),
        out_shape=jax.ShapeDtypeStruct((B * T, MLA_HEADS * V_HEAD), BF16),
        compiler_params=params, name="attn_latent",
    )(q, k, v, k, v)
    o_ctx = pl.pallas_call(
        functools.partial(_attn_body, nseg=1), grid=(B, MLA_HEADS, 1),
        in_specs=[pl.BlockSpec((L, QK_PAD), lambda b, h, i: (ctx0 + b, h)),
                  pl.BlockSpec((L, QK_PAD), lambda b, h, i: (ctx0 + b, h)),
                  pl.BlockSpec((L, V_HEAD), lambda b, h, i: (ctx0 + b, h))],
        out_specs=pl.BlockSpec((L, V_HEAD), lambda b, h, i: (b, h)),
        out_shape=jax.ShapeDtypeStruct((B * L, MLA_HEADS * V_HEAD), BF16),
        compiler_params=params, name="attn_ctx",
    )(q, k, v)
    return jnp.concatenate([o_lat, o_ctx], axis=0)


def _proj_resid_body(a_ref, x_ref, mod_ref, w_ref, o_ref):
    o_ref[...] = x_ref[...] + mod_ref[2:3, :] * _dot(a_ref[...], w_ref[...])


def _router_body(x_ref, mod_ref, g_ref, router_ref, h_ref, gates_ref):
    hf = _norm_mod(x_ref[...], g_ref[...], mod_ref[3:4, :], mod_ref[4:5, :])
    h_ref[...] = hf.astype(BF16)
    logits = jnp.dot(hf, router_ref[...], precision=HIGHEST, preferred_element_type=F32)
    lane = lax.broadcasted_iota(jnp.int32, logits.shape, 1)
    lanef = lane.astype(F32)
    neg = jnp.float32(-1e30)
    lg = jnp.where(lane < N_EXPERTS, logits, neg)
    m1 = jnp.max(lg, axis=-1, keepdims=True)
    i1 = jnp.min(jnp.where(lg == m1, lanef, float(LANES)), axis=-1, keepdims=True)
    lg2 = jnp.where(lanef == i1, neg, lg)
    m2 = jnp.max(lg2, axis=-1, keepdims=True)
    i2 = jnp.min(jnp.where(lg2 == m2, lanef, float(LANES)), axis=-1, keepdims=True)
    e2 = jnp.exp(m2 - m1)
    w1 = 1.0 / (1.0 + e2)
    w2 = e2 / (1.0 + e2)
    gates_ref[...] = jnp.where(lanef == i1, w1, 0.0) + jnp.where(lanef == i2, w2, 0.0)


def _ffn_body(*refs, moe):
    if moe:
        h_ref, gates_ref, w1_ref, w3_ref, w2_ref, x_ref, mod_ref, o_ref, acc_ref = refs
    else:
        h_ref, w1_ref, w3_ref, w2_ref, x_ref, mod_ref, o_ref, acc_ref = refs
    e = pl.program_id(1)
    f = pl.program_id(2)

    @pl.when((e == 0) & (f == 0))
    def _():
        acc_ref[...] = jnp.zeros_like(acc_ref)

    h = h_ref[...]
    a1 = _dot(h, w1_ref[...])
    a3 = _dot(h, w3_ref[...])
    hid = a1 * _sigmoid(a1) * a3
    if moe:
        gates = gates_ref[...]
        lane = lax.broadcasted_iota(jnp.int32, gates.shape, 1)
        hid = hid * jnp.sum(jnp.where(lane == e, gates, 0.0), axis=-1, keepdims=True)
    acc_ref[...] += _dot(hid.astype(BF16), w2_ref[...])

    @pl.when((e == pl.num_programs(1) - 1) & (f == pl.num_programs(2) - 1))
    def _():
        o_ref[...] = x_ref[...] + mod_ref[5:6, :] * acc_ref[...]


def _ffn(h, gates, w1, w3, w2, xf, modtab, lay, tm, tf):
    E, D, F = w1.shape
    moe = gates is not None
    n_lat_tiles = lay.n_lat // tm
    T, B = lay.T, lay.B
    row = lambda m, e, f: (m, 0)
    in_specs = [pl.BlockSpec((tm, D), row)]
    args = [h]
    if moe:
        in_specs.append(pl.BlockSpec((tm, LANES), row))
        args.append(gates)
    in_specs += [pl.BlockSpec((None, D, tf), lambda m, e, f: (e, 0, f)),
                 pl.BlockSpec((None, D, tf), lambda m, e, f: (e, 0, f)),
                 pl.BlockSpec((None, tf, D), lambda m, e, f: (e, f, 0)),
                 pl.BlockSpec((tm, D), row),
                 pl.BlockSpec((None, 6, D), lambda m, e, f: (jnp.where(m < n_lat_tiles, (m * tm) // T, B), 0, 0))]
    args += [w1, w3, w2, xf, modtab]
    return pl.pallas_call(
        functools.partial(_ffn_body, moe=moe), grid=(lay.n // tm, E, F // tf),
        in_specs=in_specs, out_specs=pl.BlockSpec((tm, D), row),
        out_shape=jax.ShapeDtypeStruct((lay.n, D), F32),
        scratch_shapes=[pltpu.VMEM((tm, D), F32)],
        compiler_params=pltpu.CompilerParams(dimension_semantics=("parallel", "arbitrary", "arbitrary"),
                                             vmem_limit_bytes=VMEM_LIMIT),
        name="moe_ffn" if moe else "dense_ffn",
    )(*args)


def _rwkv_in_body(*refs, vres):
    if vres:
        (h_ref, xx_ref, mix_ref, wr_ref, wk_ref, wv_ref, w1_ref, a1_ref, g1_ref, v1_ref,
         r_ref, k_ref, v_ref, tw_ref, ya_ref, sg_ref, yv_ref) = refs
    else:
        (h_ref, xx_ref, mix_ref, wr_ref, wk_ref, wv_ref, w1_ref, a1_ref, g1_ref,
         r_ref, k_ref, v_ref, tw_ref, ya_ref, sg_ref) = refs
    h = h_ref[...]
    xx = xx_ref[...]

    def mixed(j):
        return (h + xx * mix_ref[j:j + 1, :]).astype(BF16)

    r_ref[...] = _dot(mixed(0), wr_ref[...])
    tw_ref[...] = jnp.tanh(_dot(mixed(1), w1_ref[...]))
    k_ref[...] = _dot(mixed(2), wk_ref[...])
    xv = mixed(3)
    v_ref[...] = _dot(xv, wv_ref[...])
    if vres:
        yv_ref[...] = _dot(xv, v1_ref[...])
    ya_ref[...] = _dot(mixed(4), a1_ref[...])
    sg_ref[...] = _sigmoid(_dot(mixed(5), g1_ref[...]))


def _rwkv_feat_body(*refs, vres):
    if vres:
        (k_ref, v_ref, tw_ref, ya_ref, sg_ref, yv_ref, vf_ref, w2_ref, a2_ref, g2_ref, w0_ref, a0_ref, kk_ref_w,
         ones_ref, v2_ref, v0_ref, lw_ref, ad_ref, kk_ref, vo_ref, g_ref) = refs
    else:
        (k_ref, v_ref, tw_ref, ya_ref, sg_ref, w2_ref, a2_ref, g2_ref, w0_ref, a0_ref, kk_ref_w,
         ones_ref, lw_ref, ad_ref, kk_ref, vo_ref, g_ref) = refs
    dec = w0_ref[...] + _dot(tw_ref[...].astype(BF16), w2_ref[...])
    w_log = -_softplus(-dec) - 0.5
    lw_ref[...] = -jnp.exp(w_log)
    ad_ref[...] = _sigmoid(a0_ref[...] + _dot(ya_ref[...].astype(BF16), a2_ref[...]))
    g_ref[...] = _dot(sg_ref[...].astype(BF16), g2_ref[...])
    kkr = k_ref[...] * kk_ref_w[...]
    ss = _head_sum(kkr * kkr, ones_ref[...])
    kk_ref[...] = kkr / jnp.maximum(jnp.sqrt(ss), 1e-12)
    v = v_ref[...]
    if vres:
        v = v + (vf_ref[...] - v) * _sigmoid(v0_ref[...] + _dot(yv_ref[...].astype(BF16), v2_ref[...]))
    vo_ref[...] = v


def _scan_body(lw_ref, ad_ref, kf_ref, kk_ref, v_ref, r_ref, ka_ref, y_ref, s_ref):
    d = pl.program_id(1)
    s = pl.program_id(2)
    C = CHUNK
    D = kf_ref.shape[1]

    @pl.when(s == 0)
    def _():
        s_ref[...] = jnp.zeros_like(s_ref)

    sg = jnp.where(d == 0, 1, -1)
    ri = lax.broadcasted_iota(jnp.int32, (C, C), 0)
    ci = lax.broadcasted_iota(jnp.int32, (C, C), 1)
    dm = (ri - ci) * sg
    xr = ri ^ ci
    eye = (ri == ci).astype(F32)
    r2 = lax.broadcasted_iota(jnp.int32, (C, 2 * C), 0)
    c2 = lax.broadcasted_iota(jnp.int32, (C, 2 * C), 1) & (C - 1)
    dm2 = (r2 - c2) * sg
    strict2 = dm2 > 0
    incl2 = dm2 >= 0

    lw = lw_ref[...]
    ad = ad_ref[...]
    kf = kf_ref[...]
    kk = kk_ref[...]
    r = r_ref[...]
    c = jnp.dot((dm >= 0).astype(F32), lw, precision=HIGHEST, preferred_element_type=F32)
    pc = jnp.where(d == 0, c[C - 1:C, :], c[0:1, :])
    kd = kf * (1.0 + (ad - 1.0) * ka_ref[...])
    b = kk * ad
    einv = jnp.exp(-c)
    a_t = (-kk * jnp.exp(c - lw)).astype(BF16)
    r_t = (r * jnp.exp(c)).astype(BF16)
    b_t = (b * einv).astype(BF16)
    k_t = (kd * einv).astype(BF16)
    epc = jnp.exp(pc - c)
    b_p = (b * epc).astype(BF16)
    k_p = (kd * epc).astype(BF16)
    d_p = jnp.exp(pc)
    vb = v_ref[...].astype(BF16)
    zeros_cc = jnp.zeros((C, C), BF16)

    heads = range(D // C)
    sls = [slice(C * h, C * (h + 1)) for h in heads]
    ar = [jnp.concatenate([a_t[:, sl], r_t[:, sl]], axis=0) for sl in sls]
    bk = [jnp.concatenate([b_t[:, sl], k_t[:, sl]], axis=0) for sl in sls]
    sc = [lax.dot_general(ar[h], bk[h], _NT, preferred_element_type=F32) for h in heads]
    st = [s_ref[h] for h in heads]
    ars = [lax.dot_general(ar[h], st[h].astype(BF16), _NT, preferred_element_type=F32) for h in heads]
    top = [jnp.where(strict2, sc[h][0:C], 0.0) for h in heads]
    bot = [jnp.where(incl2, sc[h][C:], 0.0).astype(BF16) for h in heads]
    lab = [top[h][:, 0:C] for h in heads]
    zv = [jnp.concatenate([zeros_cc, vb[:, sl]], axis=0) for sl in sls]
    x = [ars[h][0:C] + _dot(top[h].astype(BF16), zv[h]) for h in heads]
    t = [eye + jnp.where(xr == 1, lab[h], 0.0) for h in heads]
    for lb in range(1, 6):
        level = (xr >> lb) == 1
        tb = [t[h].astype(BF16) for h in heads]
        ot = [_dot(jnp.where(level, lab[h], 0.0).astype(BF16), tb[h]).astype(BF16) for h in heads]
        t = [t[h] + _dot(tb[h], ot[h]) for h in heads]
    u = [_dot(t[h].astype(BF16), x[h].astype(BF16)) for h in heads]
    uv = [jnp.concatenate([u[h].astype(BF16), vb[:, sls[h]]], axis=0) for h in heads]
    for h in heads:
        y_ref[:, sls[h]] = ars[h][C:] + _dot(bot[h], uv[h])
    for h in heads:
        bkp = jnp.concatenate([b_p[:, sls[h]], k_p[:, sls[h]]], axis=0)
        s_ref[h] = st[h] * d_p[:, sls[h]] + lax.dot_general(uv[h], bkp, _TN, preferred_element_type=F32)


def _scan(lw, ad, kf, kk, v, r, k_a, lay):
    B, T, L, D = lay.B, lay.T, lay.L, lay.D
    nc_ctx = L // CHUNK
    nc_lat = T // CHUNK
    ctx_base = (B * T) // CHUNK

    def chunk_index(b, d, s):
        ctx_i = jnp.where(d == 0, s, nc_ctx - 1 - s)
        lat_i = jnp.where(d == 0, s - nc_ctx, nc_lat - 1 - (s - nc_ctx))
        return jnp.where(s < nc_ctx, ctx_base + b * nc_ctx + ctx_i, b * nc_lat + lat_i)

    per_dir = pl.BlockSpec((CHUNK, D), lambda b, d, s: (chunk_index(b, d, s), d))
    shared = pl.BlockSpec((CHUNK, D), lambda b, d, s: (chunk_index(b, d, s), 0))
    return pl.pallas_call(
        _scan_body, grid=(B, 2, nc_ctx + nc_lat),
        in_specs=[per_dir, per_dir, shared, shared, shared, shared,
                  pl.BlockSpec((1, D), lambda b, d, s: (0, 0))],
        out_specs=per_dir,
        out_shape=jax.ShapeDtypeStruct((lay.n, 2 * D), F32),
        scratch_shapes=[pltpu.VMEM((D // CHUNK, CHUNK, CHUNK), F32)],
        compiler_params=pltpu.CompilerParams(dimension_semantics=("parallel", "parallel", "arbitrary"),
                                             vmem_limit_bytes=VMEM_LIMIT),
        name="wkv_scan",
    )(lw, ad, kf, kk, v, r, k_a)


def _rwkv_post_body(y2_ref, r_ref, kf_ref, ad_ref, v_ref, g_ref, x_ref, mod_ref,
                    ka_ref, rk_ref, lnw_ref, lnb_ref, ones_ref, wo_ref, o_ref):
    D = x_ref.shape[1]
    ones4 = ones_ref[...]
    y2 = y2_ref[...]
    y = y2[:, :D] + y2[:, D:]
    inv_n = 1.0 / RWKV_HEAD
    yc = y - _head_sum(y, ones4) * inv_n
    var = _head_sum(yc * yc, ones4) * inv_n
    yn = yc * lax.rsqrt(var + LN_X_EPS) * lnw_ref[...] + lnb_ref[...]
    kf = kf_ref[...]
    ad = ad_ref[...]
    ka = ka_ref[...]
    k_bonus = 0.5 * (kf * (1.0 + (ad[:, :D] - 1.0) * ka) + kf * (1.0 + (ad[:, D:] - 1.0) * ka))
    bonus = _head_sum(r_ref[...] * k_bonus * rk_ref[...], ones4) * v_ref[...]
    o = ((yn + bonus) * g_ref[...]).astype(BF16)
    o_ref[...] = x_ref[...] + mod_ref[2:3, :] * _dot(o, wo_ref[...])


def _token_shift(h, lay):
    def centred(z):
        zp = jnp.pad(z, ((0, 0), (1, 1), (0, 0)))
        return 0.5 * (zp[:, :-2] + zp[:, 2:]) - z
    B, T, L, D = lay.B, lay.T, lay.L, lay.D
    hl = centred(h[:lay.n_lat].reshape(B, T, D)).reshape(B * T, D)
    hc = centred(h[lay.n_lat:].reshape(B, L, D)).reshape(B * L, D)
    return jnp.concatenate([hl, hc], axis=0)


def _rope_tables(T, tm):
    pos = jnp.arange(T)
    inv_freq = 1.0 / (ROPE_BASE ** (jnp.arange(AXIS_PAIRS, dtype=F32) / AXIS_PAIRS))
    ang_r = (pos // GRID_W).astype(F32)[:, None] * inv_freq[None, :]
    ang_c = (pos % GRID_W).astype(F32)[:, None] * inv_freq[None, :]
    pad1 = jnp.ones((T, LANES - QK_ROPE), F32)
    pad0 = jnp.zeros((T, LANES - QK_ROPE), F32)
    cos = jnp.concatenate([jnp.cos(ang_r), jnp.cos(ang_r), jnp.cos(ang_c), jnp.cos(ang_c), pad1], axis=1)
    sin = jnp.concatenate([-jnp.sin(ang_r), jnp.sin(ang_r), -jnp.sin(ang_c), jnp.sin(ang_c), pad0], axis=1)
    cos = jnp.concatenate([cos, jnp.ones((tm, LANES), F32)], axis=0)
    sin = jnp.concatenate([sin, jnp.zeros((tm, LANES), F32)], axis=0)
    return cos, sin


def _pad_cols(w, width):
    return jnp.pad(w, ((0, 0), (0, width - w.shape[1])))


def _pad_rows(w, height):
    return jnp.pad(w, ((0, height - w.shape[0]), (0, 0)))


def _block_diag2(w0, w1):
    z0 = jnp.zeros_like(w0)
    z1 = jnp.zeros_like(w1)
    return jnp.concatenate([jnp.concatenate([w0, z1], axis=1), jnp.concatenate([z0, w1], axis=1)], axis=0)


def kernel(x, c, ctx, c_ctx, ada_w, ada_b, norm1_g, norm2_g, mla_wqa, mla_qa_norm, mla_wqb, mla_wkva, mla_kva_norm, mla_wkvb, mla_q_norm, mla_k_norm, mla_wo, rwkv_mix, rwkv_wr, rwkv_wk, rwkv_wv, rwkv_wo, rwkv_w0, rwkv_w1, rwkv_w2, rwkv_a0, rwkv_a1, rwkv_a2, rwkv_g1, rwkv_g2, rwkv_k_k, rwkv_k_a, rwkv_r_k, rwkv_ln_w, rwkv_ln_b, rwkv_v0, rwkv_v1, rwkv_v2, ffn_w1, ffn_w3, ffn_w2, moe_router, moe_w1, moe_w3, moe_w2):
    B, T, D = x.shape
    L = ctx.shape[1]
    depth = ada_w.shape[0]
    lay = _Layout(B, T, L, D)
    n = lay.n
    tm = min(512, T)
    tm_small = min(256, T)
    assert T % tm == 0 and (B * L) % tm == 0 and T % GRID_W == 0 and L % CHUNK == 0 and B + 1 <= 16

    xf = jnp.concatenate([x.reshape(B * T, D), ctx.reshape(B * L, D)], axis=0)
    cc = jnp.concatenate([c, c_ctx[None, :], jnp.zeros((16 - B - 1, D), F32)], axis=0)
    modall = _ada_call(cc, ada_w, ada_b)[:, :B + 1].reshape(depth, B + 1, 6, D)
    cos_t, sin_t = _rope_tables(T, tm)
    ones4 = jnp.kron(jnp.eye(4, dtype=F32), jnp.ones((RWKV_HEAD, RWKV_HEAD), F32)).astype(BF16)
    row1 = lambda z: z.reshape(1, -1)

    v_first = None
    for i in range(depth):
        j = i // 2
        modtab = modall[i]
        mod_t = (modtab, lay.mod_spec(tm))
        mod_s = (modtab, lay.mod_spec(tm_small))
        if i % 2 == 0:
            wa = jnp.concatenate([mla_wqa[j], _pad_cols(mla_wkva[j], KV_LORA + LANES)], axis=1).astype(BF16)
            wqb = mla_wqb[j].reshape(Q_LORA, MLA_HEADS, QK_HEAD)
            wqb = jnp.pad(wqb, ((0, 0), (0, 0), (0, QK_PAD - QK_HEAD))).reshape(Q_LORA, MLA_HEADS * QK_PAD).astype(BF16)
            wkvb = mla_wkvb[j].reshape(KV_LORA, MLA_HEADS, QK_NOPE + V_HEAD)
            wk = wkvb[:, :, :QK_NOPE].reshape(KV_LORA, MLA_HEADS * QK_NOPE).astype(BF16)
            wv = wkvb[:, :, QK_NOPE:].reshape(KV_LORA, MLA_HEADS * V_HEAD).astype(BF16)
            gq = _pad_cols(row1(mla_q_norm[j]) * SM_SCALE, QK_PAD)
            gk = _pad_cols(row1(mla_k_norm[j]), QK_PAD)
            cq, ckv, kr = _rowwise(
                _mla_a_body, "mla_a", n, tm, [xf], [mod_t],
                [row1(norm1_g[i]), wa, row1(mla_qa_norm[j]), row1(mla_kva_norm[j])],
                [(Q_LORA, BF16), (KV_LORA, BF16), (LANES, F32)])
            rope_in = [(cos_t, lay.rope_spec(tm)), (sin_t, lay.rope_spec(tm))]
            (q,) = _rowwise(_mla_q_body, "mla_q", n, tm, [cq], rope_in, [wqb, gq], [(MLA_HEADS * QK_PAD, BF16)])
            k, v = _rowwise(_mla_kv_body, "mla_kv", n, tm, [ckv, kr], rope_in, [wk, wv, gk],
                            [(MLA_HEADS * QK_PAD, BF16), (MLA_HEADS * V_HEAD, BF16)])
            o = _attention(q, k, v, lay, tq=min(512, T))
            (xf,) = _rowwise(_proj_resid_body, "mla_out", n, tm, [o, xf], [mod_t], [mla_wo[j].astype(BF16)], [(D, F32)])
        else:
            vres = j > 0
            (h,) = _rowwise(functools.partial(_norm_body, si=0), "norm1", n, tm, [xf], [mod_t],
                            [row1(norm1_g[i])], [(D, F32)])
            xx = _token_shift(h, lay)
            mix = _pad_rows(rwkv_mix[j], 8)
            w1c = jnp.concatenate([rwkv_w1[j, 0], rwkv_w1[j, 1]], axis=1).astype(BF16)
            a1c = jnp.concatenate([rwkv_a1[j, 0], rwkv_a1[j, 1]], axis=1).astype(BF16)
            g1p = _pad_cols(rwkv_g1[j], 2 * LANES).astype(BF16)
            consts = [mix, rwkv_wr[j].astype(BF16), rwkv_wk[j].astype(BF16), rwkv_wv[j].astype(BF16), w1c, a1c, g1p]
            outs = [(D, F32), (D, F32), (D, F32), (LANES, F32), (LANES, F32), (2 * LANES, F32)]
            if vres:
                consts.append(_pad_cols(rwkv_v1[j - 1], LANES).astype(BF16))
                outs.append((LANES, F32))
            res = _rowwise(functools.partial(_rwkv_in_body, vres=vres), "rwkv_in", n, tm_small, [h, xx], [], consts, outs)
            r, kf, v_raw, tw, ya, sgate = res[:6]
            w2c = _block_diag2(rwkv_w2[j, 0], rwkv_w2[j, 1]).astype(BF16)
            a2c = _block_diag2(rwkv_a2[j, 0], rwkv_a2[j, 1]).astype(BF16)
            g2p = _pad_rows(rwkv_g2[j], 2 * LANES).astype(BF16)
            rows = [kf, v_raw, tw, ya, sgate]
            consts = [w2c, a2c, g2p, rwkv_w0[j].reshape(1, 2 * D), rwkv_a0[j].reshape(1, 2 * D), row1(rwkv_k_k[j]), ones4]
            if vres:
                rows += [res[6], v_first]
                consts += [_pad_rows(rwkv_v2[j - 1], LANES).astype(BF16), row1(rwkv_v0[j - 1])]
            lw, ad, kk, v_out, gate = _rowwise(
                functools.partial(_rwkv_feat_body, vres=vres), "rwkv_feat", n, tm_small, rows, [], consts,
                [(2 * D, F32), (2 * D, F32), (D, F32), (D, F32), (D, F32)])
            if j == 0:
                v_first = v_out
            k_a = row1(rwkv_k_a[j])
            y2 = _scan(lw, ad, kf, kk, v_out, r, k_a, lay)
            (xf,) = _rowwise(
                _rwkv_post_body, "rwkv_out", n, tm_small, [y2, r, kf, ad, v_out, gate, xf], [mod_s],
                [k_a, row1(rwkv_r_k[j]), row1(rwkv_ln_w[j]), row1(rwkv_ln_b[j]), ones4, rwkv_wo[j].astype(BF16)],
                [(D, F32)])

        if i % 2 == 0:
            (h,) = _rowwise(functools.partial(_norm_body, si=3), "norm2", n, tm, [xf], [mod_t],
                            [row1(norm2_g[i])], [(D, BF16)])
            w1 = ffn_w1[j][None].astype(BF16)
            w3 = ffn_w3[j][None].astype(BF16)
            w2 = ffn_w2[j][None].astype(BF16)
            xf = _ffn(h, None, w1, w3, w2, xf, modtab, lay, tm, tf=w1.shape[2] // 2)
        else:
            h, gates = _rowwise(_router_body, "router", n, tm, [xf], [mod_t],
                                [row1(norm2_g[i]), _pad_cols(moe_router[j], LANES)], [(D, BF16), (LANES, F32)])
            xf = _ffn(h, gates, moe_w1[j].astype(BF16), moe_w3[j].astype(BF16), moe_w2[j].astype(BF16),
                      xf, modtab, lay, tm, tf=moe_w1.shape[3] // 4)
    return xf[:B * T].reshape(B, T, D)
```

```python
import functools

import jax
import jax.numpy as jnp
from jax import lax
from jax.experimental import pallas as pl
from jax.experimental.pallas import tpu as pltpu

F32 = jnp.float32
BF16 = jnp.bfloat16
HIGHEST = lax.Precision.HIGHEST

GRID_W = 64
NORM_EPS = 1e-6
MLA_HEADS = 8
QK_NOPE = 128
QK_ROPE = 64
V_HEAD = 128
QK_HEAD = QK_NOPE + QK_ROPE
Q_LORA = 384
KV_LORA = 256
ROPE_BASE = 10000.0
AXIS_PAIRS = QK_ROPE // 4
SM_SCALE = QK_HEAD ** -0.5
RWKV_HEAD = 64
LN_X_EPS = 64e-5
N_EXPERTS = 8

LANES = 128
QK_PAD = 2 * LANES
CHUNK = 64
MOE_TM = 512
VMEM_LIMIT = 56 * 1024 * 1024

_NT = (((1,), (1,)), ((), ()))
_TN = (((0,), (0,)), ((), ()))


def _dot(a, b):
    return jnp.dot(a, b, preferred_element_type=F32)


def _sigmoid(z):
    return 1.0 / (1.0 + jnp.exp(-z))


def _softplus(z):
    return jnp.maximum(z, 0.0) + jnp.log(1.0 + jnp.exp(-jnp.abs(z)))


def _rms(z, width):
    return lax.rsqrt(jnp.sum(z * z, axis=-1, keepdims=True) * (1.0 / width) + NORM_EPS)


def _norm_mod(x, g, shift, scale):
    return (x * _rms(x, x.shape[-1])) * g * (1.0 + scale) + shift


def _rope(z, cos, sin):
    lane = lax.broadcasted_iota(jnp.int32, z.shape, 1)
    partner = jnp.where((lane & 16) == 0, pltpu.roll(z, LANES - 16, axis=1), pltpu.roll(z, 16, axis=1))
    return z * cos + partner * sin


def _head_sum(z, ones4):
    outs = []
    for q in range(z.shape[1] // 256):
        blk = z[:, 256 * q:256 * (q + 1)]
        hi = blk.astype(BF16)
        lo = (blk - hi.astype(F32)).astype(BF16)
        outs.append(_dot(hi, ones4) + _dot(lo, ones4))
    return jnp.concatenate(outs, axis=1)


class _Layout:
    def __init__(self, B, T, L, D):
        self.B, self.T, self.L, self.D = B, T, L, D
        self.n_lat = B * T
        self.n = B * T + B * L

    def mod_spec(self, tm):
        n_lat_tiles = self.n_lat // tm
        T, B, D = self.T, self.B, self.D
        return pl.BlockSpec((None, 6, D), lambda i: (jnp.where(i < n_lat_tiles, (i * tm) // T, B), 0, 0))

    def rope_spec(self, tm):
        n_lat_tiles = self.n_lat // tm
        tpb = self.T // tm
        return pl.BlockSpec((tm, LANES), lambda i: (jnp.where(i < n_lat_tiles, i % tpb, tpb), 0))


def _const_spec(arr):
    nd = arr.ndim
    return pl.BlockSpec(arr.shape, lambda i: (0,) * nd, pipeline_mode=pl.Buffered(1))


def _rowwise(body, name, n_rows, tm, row_ins, tile_ins, const_ins, outs):
    in_specs = [pl.BlockSpec((tm, a.shape[1]), lambda i: (i, 0)) for a in row_ins]
    in_specs += [spec for (_, spec) in tile_ins]
    in_specs += [_const_spec(a) for a in const_ins]
    out_specs = [pl.BlockSpec((tm, w), lambda i: (i, 0)) for (w, _) in outs]
    out_shape = [jax.ShapeDtypeStruct((n_rows, w), dt) for (w, dt) in outs]
    return pl.pallas_call(
        body, grid=(n_rows // tm,), in_specs=in_specs, out_specs=out_specs, out_shape=out_shape,
        compiler_params=pltpu.CompilerParams(dimension_semantics=("parallel",), vmem_limit_bytes=VMEM_LIMIT),
        name=name,
    )(*row_ins, *[a for (a, _) in tile_ins], *const_ins)


def _ada_body(c_ref, w_ref, b_ref, o_ref):
    c = c_ref[...]
    o_ref[...] = jnp.dot(c * _sigmoid(c), w_ref[...], precision=HIGHEST, preferred_element_type=F32) + b_ref[...]


def _ada_call(cc, ada_w, ada_b):
    depth, D, W = ada_w.shape
    tn = 1024
    return pl.pallas_call(
        _ada_body, grid=(depth, W // tn),
        in_specs=[pl.BlockSpec(cc.shape, lambda l, j: (0, 0)),
                  pl.BlockSpec((None, D, tn), lambda l, j: (l, 0, j)),
                  pl.BlockSpec((None, 1, tn), lambda l, j: (l, 0, j))],
        out_specs=pl.BlockSpec((None, cc.shape[0], tn), lambda l, j: (l, 0, j)),
        out_shape=jax.ShapeDtypeStruct((depth, cc.shape[0], W), F32),
        compiler_params=pltpu.CompilerParams(dimension_semantics=("parallel", "parallel"), vmem_limit_bytes=VMEM_LIMIT),
        name="ada",
    )(cc, ada_w, ada_b.reshape(depth, 1, W))


def _norm_body(x_ref, mod_ref, g_ref, o_ref, *, si):
    h = _norm_mod(x_ref[...], g_ref[...], mod_ref[si:si + 1, :], mod_ref[si + 1:si + 2, :])
    o_ref[...] = h.astype(o_ref.dtype)


def _mla_a_body(x_ref, mod_ref, g_ref, wa_ref, qan_ref, kvn_ref, cq_ref, ckv_ref, kr_ref):
    h = _norm_mod(x_ref[...], g_ref[...], mod_ref[0:1, :], mod_ref[1:2, :]).astype(BF16)
    acc = _dot(h, wa_ref[...])
    qa = acc[:, 0:Q_LORA]
    kv = acc[:, Q_LORA:Q_LORA + KV_LORA]
    cq_ref[...] = (qa * _rms(qa, Q_LORA) * qan_ref[...]).astype(BF16)
    ckv_ref[...] = (kv * _rms(kv, KV_LORA) * kvn_ref[...]).astype(BF16)
    kr_ref[...] = acc[:, Q_LORA + KV_LORA:]


def _mla_q_body(cq_ref, cos_ref, sin_ref, wqb_ref, gq_ref, q_ref):
    acc = _dot(cq_ref[...], wqb_ref[...])
    cos = cos_ref[...]
    sin = sin_ref[...]
    g = gq_ref[...]
    for h in range(MLA_HEADS):
        a = acc[:, QK_PAD * h:QK_PAD * (h + 1)]
        an = a * _rms(a, QK_HEAD) * g
        q_ref[:, QK_PAD * h:QK_PAD * h + LANES] = an[:, :LANES].astype(BF16)
        q_ref[:, QK_PAD * h + LANES:QK_PAD * (h + 1)] = _rope(an[:, LANES:], cos, sin).astype(BF16)


def _mla_kv_body(ckv_ref, kr_ref, cos_ref, sin_ref, wk_ref, wv_ref, gk_ref, k_ref, v_ref):
    ckv = ckv_ref[...]
    kn = _dot(ckv, wk_ref[...])
    v_ref[...] = _dot(ckv, wv_ref[...]).astype(BF16)
    kr = kr_ref[...]
    ss_rope = jnp.sum(kr * kr, axis=-1, keepdims=True)
    g = gk_ref[...]
    g_nope = g[:, :LANES]
    kr_rot = _rope(kr * g[:, LANES:], cos_ref[...], sin_ref[...])
    for h in range(MLA_HEADS):
        a = kn[:, LANES * h:LANES * (h + 1)]
        inv = lax.rsqrt((jnp.sum(a * a, axis=-1, keepdims=True) + ss_rope) * (1.0 / QK_HEAD) + NORM_EPS)
        k_ref[:, QK_PAD * h:QK_PAD * h + LANES] = (a * inv * g_nope).astype(BF16)
        k_ref[:, QK_PAD * h + LANES:QK_PAD * (h + 1)] = (kr_rot * inv).astype(BF16)


def _attn_body(*refs, nseg):
    q = refs[0][...]
    o_ref = refs[-1]
    ss = [lax.dot_general(q, refs[1 + 2 * i][...], _NT, preferred_element_type=F32) for i in range(nseg)]
    m = functools.reduce(jnp.maximum, [jnp.max(s, axis=-1, keepdims=True) for s in ss])
    ps = [jnp.exp(s - m) for s in ss]
    l = functools.reduce(lambda a, b: a + b, [jnp.sum(p, axis=-1, keepdims=True) for p in ps])
    acc = functools.reduce(lambda a, b: a + b,
                           [_dot(ps[i].astype(BF16), refs[2 + 2 * i][...]) for i in range(nseg)])
    o_ref[...] = (acc / l).astype(o_ref.dtype)


def _attention(q, k, v, lay, tq):
    B, T, L = lay.B, lay.T, lay.L
    nq = T // tq
    ctx0 = (B * T) // L
    params = pltpu.CompilerParams(dimension_semantics=("parallel", "parallel", "arbitrary"),
                                  vmem_limit_bytes=VMEM_LIMIT)
    o_lat = pl.pallas_call(
        functools.partial(_attn_body, nseg=2), grid=(B, MLA_HEADS, nq),
        in_specs=[pl.BlockSpec((tq, QK_PAD), lambda b, h, i: (b * nq + i, h)),
                  pl.BlockSpec((L, QK_PAD), lambda b, h, i: (ctx0 + b, h)),
                  pl.BlockSpec((L, V_HEAD), lambda b, h, i: (ctx0 + b, h)),
                  pl.BlockSpec((T, QK_PAD), lambda b, h, i: (b, h)),
                  pl.BlockSpec((T, V_HEAD), lambda b, h, i: (b, h))],
        out_specs=pl.BlockSpec((tq, V_HEAD), lambda b, h, i: (b * nq + i, h)),
        out_shape=jax.ShapeDtypeStruct((B * T, MLA_HEADS * V_HEAD), BF16),
        compiler_params=params, name="attn_latent",
    )(q, k, v, k, v)
    o_ctx = pl.pallas_call(
        functools.partial(_attn_body, nseg=1), grid=(B, MLA_HEADS, 1),
        in_specs=[pl.BlockSpec((L, QK_PAD), lambda b, h, i: (ctx0 + b, h)),
                  pl.BlockSpec((L, QK_PAD), lambda b, h, i: (ctx0 + b, h)),
                  pl.BlockSpec((L, V_HEAD), lambda b, h, i: (ctx0 + b, h))],
        out_specs=pl.BlockSpec((L, V_HEAD), lambda b, h, i: (b, h)),
        out_shape=jax.ShapeDtypeStruct((B * L, MLA_HEADS * V_HEAD), BF16),
        compiler_params=params, name="attn_ctx",
    )(q, k, v)
    return jnp.concatenate([o_lat, o_ctx], axis=0)


def _proj_resid_body(a_ref, x_ref, mod_ref, w_ref, o_ref):
    o_ref[...] = x_ref[...] + mod_ref[2:3, :] * _dot(a_ref[...], w_ref[...])


def _router_body(x_ref, mod_ref, g_ref, router_ref, h_ref, route_ref):
    hf = _norm_mod(x_ref[...], g_ref[...], mod_ref[3:4, :], mod_ref[4:5, :])
    h_ref[...] = hf
    logits = jnp.dot(hf, router_ref[...], precision=HIGHEST, preferred_element_type=F32)
    lane = lax.broadcasted_iota(jnp.int32, logits.shape, 1)
    lanef = lane.astype(F32)
    neg = jnp.float32(-1e30)
    lg = jnp.where(lane < N_EXPERTS, logits, neg)
    m1 = jnp.max(lg, axis=-1, keepdims=True)
    i1 = jnp.min(jnp.where(lg == m1, lanef, float(LANES)), axis=-1, keepdims=True)
    lg2 = jnp.where(lanef == i1, neg, lg)
    m2 = jnp.max(lg2, axis=-1, keepdims=True)
    i2 = jnp.min(jnp.where(lg2 == m2, lanef, float(LANES)), axis=-1, keepdims=True)
    e2 = jnp.exp(m2 - m1)
    w1 = 1.0 / (1.0 + e2)
    w2 = e2 / (1.0 + e2)
    route_ref[...] = jnp.where(lane == 0, i1, jnp.where(lane == 1, i2, jnp.where(lane == 2, w1, jnp.where(lane == 3, w2, 0.0))))


def _route_plan(route, n, tm_g):
    i1 = route[:, 0].astype(jnp.int32)
    i2 = route[:, 1].astype(jnp.int32)
    experts = jnp.arange(N_EXPERTS, dtype=jnp.int32)
    onehot = ((i1[:, None] == experts) | (i2[:, None] == experts)).astype(jnp.int32)
    rank = jnp.cumsum(onehot, axis=0) - onehot
    tiles_e = (jnp.sum(onehot, axis=0) + tm_g - 1) // tm_g
    tile_end = jnp.cumsum(tiles_e)
    start = (tile_end - tiles_e) * tm_g
    pos1 = start[i1] + jnp.take_along_axis(rank, i1[:, None], axis=1)[:, 0]
    pos2 = start[i2] + jnp.take_along_axis(rank, i2[:, None], axis=1)[:, 0]
    n_tiles = -(-2 * n // tm_g) + N_EXPERTS
    tile_ids = jnp.arange(n_tiles, dtype=jnp.int32)
    tile_expert = jnp.minimum(jnp.sum((tile_ids[:, None] >= tile_end[None, :]).astype(jnp.int32), axis=1), N_EXPERTS - 1)
    return pos1, pos2, tile_expert.astype(jnp.int32), tile_end[-1:].astype(jnp.int32), n_tiles


def _row_copy(src, src_row, dst, dst_row, sem):
    return pltpu.make_async_copy(src.at[pl.ds(src_row, 1), :], dst.at[pl.ds(dst_row, 1), :], sem)


def _moe_scatter_body(pos_hbm, h_hbm, xs_in, xs_out, idx, isem, dsem, *, ts):
    del xs_in
    i = pl.program_id(0)
    fetch = pltpu.make_async_copy(pos_hbm.at[i], idx, isem)
    fetch.start()
    fetch.wait()

    def issue(t, carry):
        _row_copy(h_hbm, i * ts + t, xs_out, idx[t], dsem).start()
        _row_copy(h_hbm, i * ts + t, xs_out, idx[ts + t], dsem).start()
        return carry

    lax.fori_loop(0, ts, issue, 0)

    def drain(t, carry):
        _row_copy(h_hbm, 0, xs_out, 0, dsem).wait()
        _row_copy(h_hbm, 0, xs_out, 0, dsem).wait()
        return carry

    lax.fori_loop(0, ts, drain, 0)


def _moe_scatter(pos1, pos2, h, n_rows_sorted):
    n, D = h.shape
    ts = 2048 if n % 2048 == 0 else 512
    pos = jnp.concatenate([pos1.reshape(n // ts, ts), pos2.reshape(n // ts, ts)], axis=1)
    any_spec = pl.BlockSpec(memory_space=pl.ANY)
    return pl.pallas_call(
        functools.partial(_moe_scatter_body, ts=ts), grid=(n // ts,),
        in_specs=[any_spec, any_spec, any_spec], out_specs=any_spec,
        out_shape=jax.ShapeDtypeStruct((n_rows_sorted, D), F32),
        scratch_shapes=[pltpu.SMEM((2 * ts,), jnp.int32), pltpu.SemaphoreType.DMA, pltpu.SemaphoreType.DMA],
        input_output_aliases={2: 0},
        compiler_params=pltpu.CompilerParams(dimension_semantics=("arbitrary",)),
        name="moe_scatter",
    )(pos, h, jnp.zeros((n_rows_sorted, D), F32))


def _gffn_body(te_ref, nu_ref, xs_ref, w1_ref, w3_ref, w2_ref, ys_ref, hb_ref, acc_ref):
    del te_ref
    m = pl.program_id(0)
    f = pl.program_id(1)
    last = pl.num_programs(1) - 1
    used = m < nu_ref[0]

    @pl.when(used & (f == 0))
    def _():
        hb_ref[...] = xs_ref[...].astype(BF16)
        acc_ref[...] = jnp.zeros_like(acc_ref)

    @pl.when(used)
    def _():
        h = hb_ref[...]
        a1 = _dot(h, w1_ref[...])
        a3 = _dot(h, w3_ref[...])
        acc_ref[...] += _dot((a1 * _sigmoid(a1) * a3).astype(BF16), w2_ref[...])

    @pl.when(used & (f == last))
    def _():
        ys_ref[...] = acc_ref[...]

    @pl.when(jnp.logical_not(used) & (f == last))
    def _():
        ys_ref[...] = jnp.zeros_like(ys_ref)


def _gffn(tile_expert, n_used, xs, w1, w3, w2, tm_g, tf):
    E, D, F = w1.shape
    n_tiles = xs.shape[0] // tm_g
    nf = F // tf
    fidx = lambda m, f, nu: jnp.where(m < nu[0], f, nf - 1)
    grid_spec = pltpu.PrefetchScalarGridSpec(
        num_scalar_prefetch=2, grid=(n_tiles, nf),
        in_specs=[pl.BlockSpec((tm_g, D), lambda m, f, te, nu: (m, 0)),
                  pl.BlockSpec((None, D, tf), lambda m, f, te, nu: (te[m], 0, fidx(m, f, nu))),
                  pl.BlockSpec((None, D, tf), lambda m, f, te, nu: (te[m], 0, fidx(m, f, nu))),
                  pl.BlockSpec((None, tf, D), lambda m, f, te, nu: (te[m], fidx(m, f, nu), 0))],
        out_specs=pl.BlockSpec((tm_g, D), lambda m, f, te, nu: (m, 0)),
        scratch_shapes=[pltpu.VMEM((tm_g, D), BF16), pltpu.VMEM((tm_g, D), F32)])
    return pl.pallas_call(
        _gffn_body, grid_spec=grid_spec, out_shape=jax.ShapeDtypeStruct(xs.shape, F32),
        compiler_params=pltpu.CompilerParams(dimension_semantics=("arbitrary", "arbitrary"),
                                             vmem_limit_bytes=VMEM_LIMIT),
        name="moe_gffn",
    )(tile_expert, n_used, xs, w1, w3, w2)


def _moe_combine_body(pos_hbm, ys_hbm, route_ref, x_ref, mod_ref, o_ref, idx, ya, yb, isem, dsem, *, tm):
    i = pl.program_id(0)
    fetch = pltpu.make_async_copy(pos_hbm.at[i], idx, isem)
    fetch.start()
    fetch.wait()

    def issue(t, carry):
        _row_copy(ys_hbm, idx[t], ya, t, dsem).start()
        _row_copy(ys_hbm, idx[tm + t], yb, t, dsem).start()
        return carry

    lax.fori_loop(0, tm, issue, 0)

    def drain(t, carry):
        _row_copy(ys_hbm, 0, ya, 0, dsem).wait()
        _row_copy(ys_hbm, 0, yb, 0, dsem).wait()
        return carry

    lax.fori_loop(0, tm, drain, 0)
    route = route_ref[...]
    y = route[:, 2:3] * ya[...] + route[:, 3:4] * yb[...]
    o_ref[...] = x_ref[...] + mod_ref[5:6, :] * y


def _moe_combine(pos1, pos2, ys, route, xf, modtab, lay, tm):
    n, D = xf.shape
    pos = jnp.concatenate([pos1.reshape(n // tm, tm), pos2.reshape(n // tm, tm)], axis=1)
    any_spec = pl.BlockSpec(memory_space=pl.ANY)
    row = lambda i: (i, 0)
    return pl.pallas_call(
        functools.partial(_moe_combine_body, tm=tm), grid=(n // tm,),
        in_specs=[any_spec, any_spec, pl.BlockSpec((tm, LANES), row), pl.BlockSpec((tm, D), row), lay.mod_spec(tm)],
        out_specs=pl.BlockSpec((tm, D), row),
        out_shape=jax.ShapeDtypeStruct((n, D), F32),
        scratch_shapes=[pltpu.SMEM((2 * tm,), jnp.int32), pltpu.VMEM((tm, D), F32), pltpu.VMEM((tm, D), F32),
                        pltpu.SemaphoreType.DMA, pltpu.SemaphoreType.DMA],
        compiler_params=pltpu.CompilerParams(dimension_semantics=("arbitrary",), vmem_limit_bytes=VMEM_LIMIT),
        name="moe_combine",
    )(pos, ys, route, xf, modtab)


def _ffn_body(h_ref, w1_ref, w3_ref, w2_ref, x_ref, mod_ref, o_ref, acc_ref):
    f = pl.program_id(1)

    @pl.when(f == 0)
    def _():
        acc_ref[...] = jnp.zeros_like(acc_ref)

    h = h_ref[...]
    a1 = _dot(h, w1_ref[...])
    a3 = _dot(h, w3_ref[...])
    acc_ref[...] += _dot((a1 * _sigmoid(a1) * a3).astype(BF16), w2_ref[...])

    @pl.when(f == pl.num_programs(1) - 1)
    def _():
        o_ref[...] = x_ref[...] + mod_ref[5:6, :] * acc_ref[...]


def _ffn(h, w1, w3, w2, xf, modtab, lay, tm, tf):
    D, F = w1.shape
    n_lat_tiles = lay.n_lat // tm
    T, B = lay.T, lay.B
    row = lambda m, f: (m, 0)
    return pl.pallas_call(
        _ffn_body, grid=(lay.n // tm, F // tf),
        in_specs=[pl.BlockSpec((tm, D), row),
                  pl.BlockSpec((D, tf), lambda m, f: (0, f)),
                  pl.BlockSpec((D, tf), lambda m, f: (0, f)),
                  pl.BlockSpec((tf, D), lambda m, f: (f, 0)),
                  pl.BlockSpec((tm, D), row),
                  pl.BlockSpec((None, 6, D), lambda m, f: (jnp.where(m < n_lat_tiles, (m * tm) // T, B), 0, 0))],
        out_specs=pl.BlockSpec((tm, D), row),
        out_shape=jax.ShapeDtypeStruct((lay.n, D), F32),
        scratch_shapes=[pltpu.VMEM((tm, D), F32)],
        compiler_params=pltpu.CompilerParams(dimension_semantics=("parallel", "arbitrary"),
                                             vmem_limit_bytes=VMEM_LIMIT),
        name="dense_ffn",
    )(h, w1, w3, w2, xf, modtab)


def _rwkv_in_body(*refs, vres):
    if vres:
        (h_ref, xx_ref, mix_ref, wr_ref, wk_ref, wv_ref, w1_ref, a1_ref, g1_ref, v1_ref,
         r_ref, k_ref, v_ref, tw_ref, ya_ref, sg_ref, yv_ref) = refs
    else:
        (h_ref, xx_ref, mix_ref, wr_ref, wk_ref, wv_ref, w1_ref, a1_ref, g1_ref,
         r_ref, k_ref, v_ref, tw_ref, ya_ref, sg_ref) = refs
    h = h_ref[...]
    xx = xx_ref[...]

    def mixed(j):
        return (h + xx * mix_ref[j:j + 1, :]).astype(BF16)

    r_ref[...] = _dot(mixed(0), wr_ref[...])
    tw_ref[...] = jnp.tanh(_dot(mixed(1), w1_ref[...]))
    k_ref[...] = _dot(mixed(2), wk_ref[...])
    xv = mixed(3)
    v_ref[...] = _dot(xv, wv_ref[...])
    if vres:
        yv_ref[...] = _dot(xv, v1_ref[...])
    ya_ref[...] = _dot(mixed(4), a1_ref[...])
    sg_ref[...] = _sigmoid(_dot(mixed(5), g1_ref[...]))


def _rwkv_feat_body(*refs, vres):
    if vres:
        (k_ref, v_ref, tw_ref, ya_ref, sg_ref, yv_ref, vf_ref, w2_ref, a2_ref, g2_ref, w0_ref, a0_ref, kk_ref_w,
         ones_ref, v2_ref, v0_ref, lw_ref, ad_ref, kk_ref, vo_ref, g_ref) = refs
    else:
        (k_ref, v_ref, tw_ref, ya_ref, sg_ref, w2_ref, a2_ref, g2_ref, w0_ref, a0_ref, kk_ref_w,
         ones_ref, lw_ref, ad_ref, kk_ref, vo_ref, g_ref) = refs
    dec = w0_ref[...] + _dot(tw_ref[...].astype(BF16), w2_ref[...])
    w_log = -_softplus(-dec) - 0.5
    lw_ref[...] = -jnp.exp(w_log)
    ad_ref[...] = _sigmoid(a0_ref[...] + _dot(ya_ref[...].astype(BF16), a2_ref[...]))
    g_ref[...] = _dot(sg_ref[...].astype(BF16), g2_ref[...])
    kkr = k_ref[...] * kk_ref_w[...]
    ss = _head_sum(kkr * kkr, ones_ref[...])
    kk_ref[...] = kkr / jnp.maximum(jnp.sqrt(ss), 1e-12)
    v = v_ref[...]
    if vres:
        v = v + (vf_ref[...] - v) * _sigmoid(v0_ref[...] + _dot(yv_ref[...].astype(BF16), v2_ref[...]))
    vo_ref[...] = v


def _scan_body(lw_ref, ad_ref, kf_ref, kk_ref, v_ref, r_ref, ka_ref, y_ref, s_ref):
    d = pl.program_id(1)
    s = pl.program_id(2)
    C = CHUNK
    D = kf_ref.shape[1]

    @pl.when(s == 0)
    def _():
        s_ref[...] = jnp.zeros_like(s_ref)

    sg = jnp.where(d == 0, 1, -1)
    ri = lax.broadcasted_iota(jnp.int32, (C, C), 0)
    ci = lax.broadcasted_iota(jnp.int32, (C, C), 1)
    dm = (ri - ci) * sg
    xr = ri ^ ci
    eye = (ri == ci).astype(F32)
    r2 = lax.broadcasted_iota(jnp.int32, (C, 2 * C), 0)
    c2 = lax.broadcasted_iota(jnp.int32, (C, 2 * C), 1) & (C - 1)
    dm2 = (r2 - c2) * sg
    strict2 = dm2 > 0
    incl2 = dm2 >= 0

    lw = lw_ref[...]
    ad = ad_ref[...]
    kf = kf_ref[...]
    kk = kk_ref[...]
    r = r_ref[...]
    c = jnp.dot((dm >= 0).astype(F32), lw, precision=HIGHEST, preferred_element_type=F32)
    pc = jnp.where(d == 0, c[C - 1:C, :], c[0:1, :])
    kd = kf * (1.0 + (ad - 1.0) * ka_ref[...])
    b = kk * ad
    einv = jnp.exp(-c)
    a_t = (-kk * jnp.exp(c - lw)).astype(BF16)
    r_t = (r * jnp.exp(c)).astype(BF16)
    b_t = (b * einv).astype(BF16)
    k_t = (kd * einv).astype(BF16)
    epc = jnp.exp(pc - c)
    b_p = (b * epc).astype(BF16)
    k_p = (kd * epc).astype(BF16)
    d_p = jnp.exp(pc)
    vb = v_ref[...].astype(BF16)
    zeros_cc = jnp.zeros((C, C), BF16)

    heads = range(D // C)
    sls = [slice(C * h, C * (h + 1)) for h in heads]
    ar = [jnp.concatenate([a_t[:, sl], r_t[:, sl]], axis=0) for sl in sls]
    bk = [jnp.concatenate([b_t[:, sl], k_t[:, sl]], axis=0) for sl in sls]
    sc = [lax.dot_general(ar[h], bk[h], _NT, preferred_element_type=F32) for h in heads]
    st = [s_ref[h] for h in heads]
    ars = [lax.dot_general(ar[h], st[h].astype(BF16), _NT, preferred_element_type=F32) for h in heads]
    top = [jnp.where(strict2, sc[h][0:C], 0.0) for h in heads]
    bot = [jnp.where(incl2, sc[h][C:], 0.0).astype(BF16) for h in heads]
    lab = [top[h][:, 0:C] for h in heads]
    zv = [jnp.concatenate([zeros_cc, vb[:, sl]], axis=0) for sl in sls]
    x = [ars[h][0:C] + _dot(top[h].astype(BF16), zv[h]) for h in heads]
    t = [eye + jnp.where(xr == 1, lab[h], 0.0) for h in heads]
    for lb in range(1, 6):
        level = (xr >> lb) == 1
        tb = [t[h].astype(BF16) for h in heads]
        ot = [_dot(jnp.where(level, lab[h], 0.0).astype(BF16), tb[h]).astype(BF16) for h in heads]
        t = [t[h] + _dot(tb[h], ot[h]) for h in heads]
    u = [_dot(t[h].astype(BF16), x[h].astype(BF16)) for h in heads]
    uv = [jnp.concatenate([u[h].astype(BF16), vb[:, sls[h]]], axis=0) for h in heads]
    for h in heads:
        y_ref[:, sls[h]] = ars[h][C:] + _dot(bot[h], uv[h])
    for h in heads:
        bkp = jnp.concatenate([b_p[:, sls[h]], k_p[:, sls[h]]], axis=0)
        s_ref[h] = st[h] * d_p[:, sls[h]] + lax.dot_general(uv[h], bkp, _TN, preferred_element_type=F32)


def _scan(lw, ad, kf, kk, v, r, k_a, lay):
    B, T, L, D = lay.B, lay.T, lay.L, lay.D
    nc_ctx = L // CHUNK
    nc_lat = T // CHUNK
    ctx_base = (B * T) // CHUNK

    def chunk_index(b, d, s):
        ctx_i = jnp.where(d == 0, s, nc_ctx - 1 - s)
        lat_i = jnp.where(d == 0, s - nc_ctx, nc_lat - 1 - (s - nc_ctx))
        return jnp.where(s < nc_ctx, ctx_base + b * nc_ctx + ctx_i, b * nc_lat + lat_i)

    per_dir = pl.BlockSpec((CHUNK, D), lambda b, d, s: (chunk_index(b, d, s), d))
    shared = pl.BlockSpec((CHUNK, D), lambda b, d, s: (chunk_index(b, d, s), 0))
    return pl.pallas_call(
        _scan_body, grid=(B, 2, nc_ctx + nc_lat),
        in_specs=[per_dir, per_dir, shared, shared, shared, shared,
                  pl.BlockSpec((1, D), lambda b, d, s: (0, 0))],
        out_specs=per_dir,
        out_shape=jax.ShapeDtypeStruct((lay.n, 2 * D), F32),
        scratch_shapes=[pltpu.VMEM((D // CHUNK, CHUNK, CHUNK), F32)],
        compiler_params=pltpu.CompilerParams(dimension_semantics=("parallel", "parallel", "arbitrary"),
                                             vmem_limit_bytes=VMEM_LIMIT),
        name="wkv_scan",
    )(lw, ad, kf, kk, v, r, k_a)


def _rwkv_post_body(y2_ref, r_ref, kf_ref, ad_ref, v_ref, g_ref, x_ref, mod_ref,
                    ka_ref, rk_ref, lnw_ref, lnb_ref, ones_ref, wo_ref, o_ref):
    D = x_ref.shape[1]
    ones4 = ones_ref[...]
    y2 = y2_ref[...]
    y = y2[:, :D] + y2[:, D:]
    inv_n = 1.0 / RWKV_HEAD
    yc = y - _head_sum(y, ones4) * inv_n
    var = _head_sum(yc * yc, ones4) * inv_n
    yn = yc * lax.rsqrt(var + LN_X_EPS) * lnw_ref[...] + lnb_ref[...]
    kf = kf_ref[...]
    ad = ad_ref[...]
    ka = ka_ref[...]
    k_bonus = 0.5 * (kf * (1.0 + (ad[:, :D] - 1.0) * ka) + kf * (1.0 + (ad[:, D:] - 1.0) * ka))
    bonus = _head_sum(r_ref[...] * k_bonus * rk_ref[...], ones4) * v_ref[...]
    o = ((yn + bonus) * g_ref[...]).astype(BF16)
    o_ref[...] = x_ref[...] + mod_ref[2:3, :] * _dot(o, wo_ref[...])


def _token_shift(h, lay):
    def centred(z):
        zp = jnp.pad(z, ((0, 0), (1, 1), (0, 0)))
        return 0.5 * (zp[:, :-2] + zp[:, 2:]) - z
    B, T, L, D = lay.B, lay.T, lay.L, lay.D
    hl = centred(h[:lay.n_lat].reshape(B, T, D)).reshape(B * T, D)
    hc = centred(h[lay.n_lat:].reshape(B, L, D)).reshape(B * L, D)
    return jnp.concatenate([hl, hc], axis=0)


def _rope_tables(T, tm):
    pos = jnp.arange(T)
    inv_freq = 1.0 / (ROPE_BASE ** (jnp.arange(AXIS_PAIRS, dtype=F32) / AXIS_PAIRS))
    ang_r = (pos // GRID_W).astype(F32)[:, None] * inv_freq[None, :]
    ang_c = (pos % GRID_W).astype(F32)[:, None] * inv_freq[None, :]
    pad1 = jnp.ones((T, LANES - QK_ROPE), F32)
    pad0 = jnp.zeros((T, LANES - QK_ROPE), F32)
    cos = jnp.concatenate([jnp.cos(ang_r), jnp.cos(ang_r), jnp.cos(ang_c), jnp.cos(ang_c), pad1], axis=1)
    sin = jnp.concatenate([-jnp.sin(ang_r), jnp.sin(ang_r), -jnp.sin(ang_c), jnp.sin(ang_c), pad0], axis=1)
    cos = jnp.concatenate([cos, jnp.ones((tm, LANES), F32)], axis=0)
    sin = jnp.concatenate([sin, jnp.zeros((tm, LANES), F32)], axis=0)
    return cos, sin


def _pad_cols(w, width):
    return jnp.pad(w, ((0, 0), (0, width - w.shape[1])))


def _pad_rows(w, height):
    return jnp.pad(w, ((0, height - w.shape[0]), (0, 0)))


def _block_diag2(w0, w1):
    z0 = jnp.zeros_like(w0)
    z1 = jnp.zeros_like(w1)
    return jnp.concatenate([jnp.concatenate([w0, z1], axis=1), jnp.concatenate([z0, w1], axis=1)], axis=0)


def kernel(x, c, ctx, c_ctx, ada_w, ada_b, norm1_g, norm2_g, mla_wqa, mla_qa_norm, mla_wqb, mla_wkva, mla_kva_norm, mla_wkvb, mla_q_norm, mla_k_norm, mla_wo, rwkv_mix, rwkv_wr, rwkv_wk, rwkv_wv, rwkv_wo, rwkv_w0, rwkv_w1, rwkv_w2, rwkv_a0, rwkv_a1, rwkv_a2, rwkv_g1, rwkv_g2, rwkv_k_k, rwkv_k_a, rwkv_r_k, rwkv_ln_w, rwkv_ln_b, rwkv_v0, rwkv_v1, rwkv_v2, ffn_w1, ffn_w3, ffn_w2, moe_router, moe_w1, moe_w3, moe_w2):
    B, T, D = x.shape
    L = ctx.shape[1]
    depth = ada_w.shape[0]
    lay = _Layout(B, T, L, D)
    n = lay.n
    tm = min(512, T)
    tm_small = min(256, T)
    assert T % tm == 0 and (B * L) % tm == 0 and T % GRID_W == 0 and L % CHUNK == 0 and B + 1 <= 16

    xf = jnp.concatenate([x.reshape(B * T, D), ctx.reshape(B * L, D)], axis=0)
    cc = jnp.concatenate([c, c_ctx[None, :], jnp.zeros((16 - B - 1, D), F32)], axis=0)
    modall = _ada_call(cc, ada_w, ada_b)[:, :B + 1].reshape(depth, B + 1, 6, D)
    cos_t, sin_t = _rope_tables(T, tm)
    ones4 = jnp.kron(jnp.eye(4, dtype=F32), jnp.ones((RWKV_HEAD, RWKV_HEAD), F32)).astype(BF16)
    row1 = lambda z: z.reshape(1, -1)

    v_first = None
    for i in range(depth):
        j = i // 2
        modtab = modall[i]
        mod_t = (modtab, lay.mod_spec(tm))
        mod_s = (modtab, lay.mod_spec(tm_small))
        if i % 2 == 0:
            wa = jnp.concatenate([mla_wqa[j], _pad_cols(mla_wkva[j], KV_LORA + LANES)], axis=1).astype(BF16)
            wqb = mla_wqb[j].reshape(Q_LORA, MLA_HEADS, QK_HEAD)
            wqb = jnp.pad(wqb, ((0, 0), (0, 0), (0, QK_PAD - QK_HEAD))).reshape(Q_LORA, MLA_HEADS * QK_PAD).astype(BF16)
            wkvb = mla_wkvb[j].reshape(KV_LORA, MLA_HEADS, QK_NOPE + V_HEAD)
            wk = wkvb[:, :, :QK_NOPE].reshape(KV_LORA, MLA_HEADS * QK_NOPE).astype(BF16)
            wv = wkvb[:, :, QK_NOPE:].reshape(KV_LORA, MLA_HEADS * V_HEAD).astype(BF16)
            gq = _pad_cols(row1(mla_q_norm[j]) * SM_SCALE, QK_PAD)
            gk = _pad_cols(row1(mla_k_norm[j]), QK_PAD)
            cq, ckv, kr = _rowwise(
                _mla_a_body, "mla_a", n, tm, [xf], [mod_t],
                [row1(norm1_g[i]), wa, row1(mla_qa_norm[j]), row1(mla_kva_norm[j])],
                [(Q_LORA, BF16), (KV_LORA, BF16), (LANES, F32)])
            rope_in = [(cos_t, lay.rope_spec(tm)), (sin_t, lay.rope_spec(tm))]
            (q,) = _rowwise(_mla_q_body, "mla_q", n, tm, [cq], rope_in, [wqb, gq], [(MLA_HEADS * QK_PAD, BF16)])
            k, v = _rowwise(_mla_kv_body, "mla_kv", n, tm, [ckv, kr], rope_in, [wk, wv, gk],
                            [(MLA_HEADS * QK_PAD, BF16), (MLA_HEADS * V_HEAD, BF16)])
            o = _attention(q, k, v, lay, tq=min(512, T))
            (xf,) = _rowwise(_proj_resid_body, "mla_out", n, tm, [o, xf], [mod_t], [mla_wo[j].astype(BF16)], [(D, F32)])
        else:
            vres = j > 0
            (h,) = _rowwise(functools.partial(_norm_body, si=0), "norm1", n, tm, [xf], [mod_t],
                            [row1(norm1_g[i])], [(D, F32)])
            xx = _token_shift(h, lay)
            mix = _pad_rows(rwkv_mix[j], 8)
            w1c = jnp.concatenate([rwkv_w1[j, 0], rwkv_w1[j, 1]], axis=1).astype(BF16)
            a1c = jnp.concatenate([rwkv_a1[j, 0], rwkv_a1[j, 1]], axis=1).astype(BF16)
            g1p = _pad_cols(rwkv_g1[j], 2 * LANES).astype(BF16)
            consts = [mix, rwkv_wr[j].astype(BF16), rwkv_wk[j].astype(BF16), rwkv_wv[j].astype(BF16), w1c, a1c, g1p]
            outs = [(D, F32), (D, F32), (D, F32), (LANES, F32), (LANES, F32), (2 * LANES, F32)]
            if vres:
                consts.append(_pad_cols(rwkv_v1[j - 1], LANES).astype(BF16))
                outs.append((LANES, F32))
            res = _rowwise(functools.partial(_rwkv_in_body, vres=vres), "rwkv_in", n, tm_small, [h, xx], [], consts, outs)
            r, kf, v_raw, tw, ya, sgate = res[:6]
            w2c = _block_diag2(rwkv_w2[j, 0], rwkv_w2[j, 1]).astype(BF16)
            a2c = _block_diag2(rwkv_a2[j, 0], rwkv_a2[j, 1]).astype(BF16)
            g2p = _pad_rows(rwkv_g2[j], 2 * LANES).astype(BF16)
            rows = [kf, v_raw, tw, ya, sgate]
            consts = [w2c, a2c, g2p, rwkv_w0[j].reshape(1, 2 * D), rwkv_a0[j].reshape(1, 2 * D), row1(rwkv_k_k[j]), ones4]
            if vres:
                rows += [res[6], v_first]
                consts += [_pad_rows(rwkv_v2[j - 1], LANES).astype(BF16), row1(rwkv_v0[j - 1])]
            lw, ad, kk, v_out, gate = _rowwise(
                functools.partial(_rwkv_feat_body, vres=vres), "rwkv_feat", n, tm_small, rows, [], consts,
                [(2 * D, F32), (2 * D, F32), (D, F32), (D, F32), (D, F32)])
            if j == 0:
                v_first = v_out
            k_a = row1(rwkv_k_a[j])
            y2 = _scan(lw, ad, kf, kk, v_out, r, k_a, lay)
            (xf,) = _rowwise(
                _rwkv_post_body, "rwkv_out", n, tm_small, [y2, r, kf, ad, v_out, gate, xf], [mod_s],
                [k_a, row1(rwkv_r_k[j]), row1(rwkv_ln_w[j]), row1(rwkv_ln_b[j]), ones4, rwkv_wo[j].astype(BF16)],
                [(D, F32)])

        if i % 2 == 0:
            (h,) = _rowwise(functools.partial(_norm_body, si=3), "norm2", n, tm, [xf], [mod_t],
                            [row1(norm2_g[i])], [(D, BF16)])
            xf = _ffn(h, ffn_w1[j].astype(BF16), ffn_w3[j].astype(BF16), ffn_w2[j].astype(BF16),
                      xf, modtab, lay, tm, tf=ffn_w1.shape[2] // 2)
        else:
            h, route = _rowwise(_router_body, "router", n, tm, [xf], [mod_t],
                                [row1(norm2_g[i]), _pad_cols(moe_router[j], LANES)], [(D, F32), (LANES, F32)])
            pos1, pos2, tile_expert, n_used, n_tiles = _route_plan(route, n, MOE_TM)
            xs = _moe_scatter(pos1, pos2, h, n_tiles * MOE_TM)
            ys = _gffn(tile_expert, n_used, xs, moe_w1[j].astype(BF16), moe_w3[j].astype(BF16),
                       moe_w2[j].astype(BF16), MOE_TM, tf=moe_w1.shape[3] // 4)
            xf = _moe_combine(pos1, pos2, ys, route, xf, modtab, lay, tm)
    return xf[:B * T].reshape(B, T, D)
```

```python
import functools

import jax
import jax.numpy as jnp
from jax import lax
from jax.experimental import pallas as pl
from jax.experimental.pallas import tpu as pltpu

F32 = jnp.float32
BF16 = jnp.bfloat16
HIGHEST = lax.Precision.HIGHEST

GRID_W = 64
NORM_EPS = 1e-6
MLA_HEADS = 8
QK_NOPE = 128
QK_ROPE = 64
V_HEAD = 128
QK_HEAD = QK_NOPE + QK_ROPE
Q_LORA = 384
KV_LORA = 256
ROPE_BASE = 10000.0
AXIS_PAIRS = QK_ROPE // 4
SM_SCALE = QK_HEAD ** -0.5
RWKV_HEAD = 64
LN_X_EPS = 64e-5
N_EXPERTS = 8

LANES = 128
QK_PAD = 2 * LANES
V_PAD = 2 * LANES
CHUNK = 64
MOE_TM = 512
IDX_ROW = 1024
ATTN_SUB = 256
VMEM_LIMIT = 56 * 1024 * 1024

_NT = (((1,), (1,)), ((), ()))
_TN = (((0,), (0,)), ((), ()))


def _dot(a, b):
    return jnp.dot(a, b, preferred_element_type=F32)


def _sigmoid(z):
    return 1.0 / (1.0 + jnp.exp(-z))


def _softplus(z):
    return jnp.maximum(z, 0.0) + jnp.log(1.0 + jnp.exp(-jnp.abs(z)))


def _rms(z, width):
    return lax.rsqrt(jnp.sum(z * z, axis=-1, keepdims=True) * (1.0 / width) + NORM_EPS)


def _norm_mod(x, g, shift, scale):
    return (x * _rms(x, x.shape[-1])) * g * (1.0 + scale) + shift


def _rope(z, cos, sin):
    lane = lax.broadcasted_iota(jnp.int32, z.shape, 1)
    partner = jnp.where((lane & 16) == 0, pltpu.roll(z, LANES - 16, axis=1), pltpu.roll(z, 16, axis=1))
    return z * cos + partner * sin


def _head_sum(z, ones4):
    outs = []
    for q in range(z.shape[1] // 256):
        blk = z[:, 256 * q:256 * (q + 1)]
        hi = blk.astype(BF16)
        lo = (blk - hi.astype(F32)).astype(BF16)
        outs.append(_dot(hi, ones4) + _dot(lo, ones4))
    return jnp.concatenate(outs, axis=1)


class _Layout:
    def __init__(self, B, T, L, D):
        self.B, self.T, self.L, self.D = B, T, L, D
        self.n_lat = B * T
        self.n = B * T + B * L

    def mod_spec(self, tm):
        n_lat_tiles = self.n_lat // tm
        T, B, D = self.T, self.B, self.D
        return pl.BlockSpec((None, 6, D), lambda i: (jnp.where(i < n_lat_tiles, (i * tm) // T, B), 0, 0))

    def rope_spec(self, tm):
        n_lat_tiles = self.n_lat // tm
        tpb = self.T // tm
        return pl.BlockSpec((tm, LANES), lambda i: (jnp.where(i < n_lat_tiles, i % tpb, tpb), 0))


def _const_spec(arr):
    nd = arr.ndim
    return pl.BlockSpec(arr.shape, lambda i: (0,) * nd, pipeline_mode=pl.Buffered(1))


def _rowwise(body, name, n_rows, tm, row_ins, tile_ins, const_ins, outs):
    in_specs = [pl.BlockSpec((tm, a.shape[1]), lambda i: (i, 0)) for a in row_ins]
    in_specs += [spec for (_, spec) in tile_ins]
    in_specs += [_const_spec(a) for a in const_ins]
    out_specs = [pl.BlockSpec((tm, w), lambda i: (i, 0)) for (w, _) in outs]
    out_shape = [jax.ShapeDtypeStruct((n_rows, w), dt) for (w, dt) in outs]
    return pl.pallas_call(
        body, grid=(n_rows // tm,), in_specs=in_specs, out_specs=out_specs, out_shape=out_shape,
        compiler_params=pltpu.CompilerParams(dimension_semantics=("parallel",), vmem_limit_bytes=VMEM_LIMIT),
        name=name,
    )(*row_ins, *[a for (a, _) in tile_ins], *const_ins)


def _ada_body(c_ref, w_ref, b_ref, o_ref):
    c = c_ref[...]
    o_ref[...] = jnp.dot(c * _sigmoid(c), w_ref[...], precision=HIGHEST, preferred_element_type=F32) + b_ref[...]


def _ada_call(cc, ada_w, ada_b):
    depth, D, W = ada_w.shape
    tn = 1024
    return pl.pallas_call(
        _ada_body, grid=(depth, W // tn),
        in_specs=[pl.BlockSpec(cc.shape, lambda l, j: (0, 0)),
                  pl.BlockSpec((None, D, tn), lambda l, j: (l, 0, j)),
                  pl.BlockSpec((None, 1, tn), lambda l, j: (l, 0, j))],
        out_specs=pl.BlockSpec((None, cc.shape[0], tn), lambda l, j: (l, 0, j)),
        out_shape=jax.ShapeDtypeStruct((depth, cc.shape[0], W), F32),
        compiler_params=pltpu.CompilerParams(dimension_semantics=("parallel", "parallel"), vmem_limit_bytes=VMEM_LIMIT),
        name="ada",
    )(cc, ada_w, ada_b.reshape(depth, 1, W))


def _norm_body(x_ref, mod_ref, g_ref, o_ref, *, si):
    h = _norm_mod(x_ref[...], g_ref[...], mod_ref[si:si + 1, :], mod_ref[si + 1:si + 2, :])
    o_ref[...] = h.astype(o_ref.dtype)


def _mla_a_body(x_ref, mod_ref, g_ref, wa_ref, qan_ref, kvn_ref, cq_ref, ckv_ref, kr_ref):
    h = _norm_mod(x_ref[...], g_ref[...], mod_ref[0:1, :], mod_ref[1:2, :]).astype(BF16)
    acc = _dot(h, wa_ref[...])
    qa = acc[:, 0:Q_LORA]
    kv = acc[:, Q_LORA:Q_LORA + KV_LORA]
    cq_ref[...] = (qa * _rms(qa, Q_LORA) * qan_ref[...]).astype(BF16)
    ckv_ref[...] = (kv * _rms(kv, KV_LORA) * kvn_ref[...]).astype(BF16)
    kr_ref[...] = acc[:, Q_LORA + KV_LORA:]


def _mla_q_body(cq_ref, cos_ref, sin_ref, wqb_ref, gq_ref, q_ref):
    acc = _dot(cq_ref[...], wqb_ref[...])
    cos = cos_ref[...]
    sin = sin_ref[...]
    g = gq_ref[...]
    for h in range(MLA_HEADS):
        a = acc[:, QK_PAD * h:QK_PAD * (h + 1)]
        an = a * _rms(a, QK_HEAD) * g
        q_ref[:, QK_PAD * h:QK_PAD * h + LANES] = an[:, :LANES].astype(BF16)
        q_ref[:, QK_PAD * h + LANES:QK_PAD * (h + 1)] = _rope(an[:, LANES:], cos, sin).astype(BF16)


def _mla_kv_body(ckv_ref, kr_ref, cos_ref, sin_ref, wk_ref, wv_ref, gk_ref, k_ref, v_ref):
    ckv = ckv_ref[...]
    kn = _dot(ckv, wk_ref[...])
    vals = _dot(ckv, wv_ref[...]).astype(BF16)
    lane = lax.broadcasted_iota(jnp.int32, (vals.shape[0], LANES), 1)
    ones_col = jnp.where(lane == 0, 1.0, 0.0).astype(BF16)
    for h in range(MLA_HEADS):
        v_ref[:, V_PAD * h:V_PAD * h + V_HEAD] = vals[:, V_HEAD * h:V_HEAD * (h + 1)]
        v_ref[:, V_PAD * h + V_HEAD:V_PAD * (h + 1)] = ones_col
    kr = kr_ref[...]
    ss_rope = jnp.sum(kr * kr, axis=-1, keepdims=True)
    g = gk_ref[...]
    g_nope = g[:, :LANES]
    kr_rot = _rope(kr * g[:, LANES:], cos_ref[...], sin_ref[...])
    for h in range(MLA_HEADS):
        a = kn[:, LANES * h:LANES * (h + 1)]
        inv = lax.rsqrt((jnp.sum(a * a, axis=-1, keepdims=True) + ss_rope) * (1.0 / QK_HEAD) + NORM_EPS)
        k_ref[:, QK_PAD * h:QK_PAD * h + LANES] = (a * inv * g_nope).astype(BF16)
        k_ref[:, QK_PAD * h + LANES:QK_PAD * (h + 1)] = (kr_rot * inv).astype(BF16)


def _attn_body(*refs, nseg, sub):
    q_ref = refs[0]
    o_ref = refs[-1]
    n_sub = q_ref.shape[0] // sub

    def scores(j):
        q = q_ref[sub * j:sub * (j + 1), :]
        return [lax.dot_general(q, refs[1 + 2 * i][...], _NT, preferred_element_type=F32) for i in range(nseg)]

    def finish(j, ss):
        m = functools.reduce(jnp.maximum, [jnp.max(s, axis=-1, keepdims=True) for s in ss])
        acc = functools.reduce(lambda a, b: a + b,
                               [_dot(jnp.exp((ss[i] - m).astype(BF16)), refs[2 + 2 * i][...]) for i in range(nseg)])
        o_ref[sub * j:sub * (j + 1), :] = (acc[:, :V_HEAD] / acc[:, V_HEAD:V_HEAD + 1]).astype(o_ref.dtype)

    pending = scores(0)
    for j in range(n_sub):
        following = scores(j + 1) if j + 1 < n_sub else None
        finish(j, pending)
        pending = following


def _attention(q, k, v, lay, tq):
    B, T, L = lay.B, lay.T, lay.L
    nq = T // tq
    ctx0 = (B * T) // L
    params = pltpu.CompilerParams(dimension_semantics=("parallel", "parallel", "arbitrary"),
                                  vmem_limit_bytes=VMEM_LIMIT)
    o_lat = pl.pallas_call(
        functools.partial(_attn_body, nseg=2, sub=min(ATTN_SUB, tq)), grid=(B, MLA_HEADS, nq),
        in_specs=[pl.BlockSpec((tq, QK_PAD), lambda b, h, i: (b * nq + i, h)),
                  pl.BlockSpec((L, QK_PAD), lambda b, h, i: (ctx0 + b, h)),
                  pl.BlockSpec((L, V_PAD), lambda b, h, i: (ctx0 + b, h)),
                  pl.BlockSpec((T, QK_PAD), lambda b, h, i: (b, h)),
                  pl.BlockSpec((T, V_PAD), lambda b, h, i: (b, h))],
        out_specs=pl.BlockSpec((tq, V_HEAD), lambda b, h, i: (b * nq + i, h)),
        out_shape=jax.ShapeDtypeStruct((B * T, MLA_HEADS * V_HEAD), BF16),
        compiler_params=params, name="attn_latent",
    )(q, k, v, k, v)
    o_ctx = pl.pallas_call(
        functools.partial(_attn_body, nseg=1, sub=min(ATTN_SUB, L)), grid=(B, MLA_HEADS, 1),
        in_specs=[pl.BlockSpec((L, QK_PAD), lambda b, h, i: (ctx0 + b, h)),
                  pl.BlockSpec((L, QK_PAD), lambda b, h, i: (ctx0 + b, h)),
                  pl.BlockSpec((L, V_PAD), lambda b, h, i: (ctx0 + b, h))],
        out_specs=pl.BlockSpec((L, V_HEAD), lambda b, h, i: (b, h)),
        out_shape=jax.ShapeDtypeStruct((B * L, MLA_HEADS * V_HEAD), BF16),
        compiler_params=params, name="attn_ctx",
    )(q, k, v)
    return jnp.concatenate([o_lat, o_ctx], axis=0)


def _proj_resid_body(a_ref, x_ref, mod_ref, w_ref, o_ref):
    o_ref[...] = x_ref[...] + mod_ref[2:3, :] * _dot(a_ref[...], w_ref[...])


def _router_body(x_ref, mod_ref, g_ref, router_ref, h_ref, route_ref):
    hf = _norm_mod(x_ref[...], g_ref[...], mod_ref[3:4, :], mod_ref[4:5, :])
    h_ref[...] = hf
    logits = jnp.dot(hf, router_ref[...], precision=HIGHEST, preferred_element_type=F32)
    lane = lax.broadcasted_iota(jnp.int32, logits.shape, 1)
    lanef = lane.astype(F32)
    neg = jnp.float32(-1e30)
    lg = jnp.where(lane < N_EXPERTS, logits, neg)
    m1 = jnp.max(lg, axis=-1, keepdims=True)
    i1 = jnp.min(jnp.where(lg == m1, lanef, float(LANES)), axis=-1, keepdims=True)
    lg2 = jnp.where(lanef == i1, neg, lg)
    m2 = jnp.max(lg2, axis=-1, keepdims=True)
    i2 = jnp.min(jnp.where(lg2 == m2, lanef, float(LANES)), axis=-1, keepdims=True)
    e2 = jnp.exp(m2 - m1)
    w1 = 1.0 / (1.0 + e2)
    w2 = e2 / (1.0 + e2)
    route_ref[...] = jnp.where(lane == 0, i1, jnp.where(lane == 1, i2, jnp.where(lane == 2, w1, jnp.where(lane == 3, w2, 0.0))))


def _route_plan(route, n, tm_g):
    i1 = route[:, 0].astype(jnp.int32)
    i2 = route[:, 1].astype(jnp.int32)
    experts = jnp.arange(N_EXPERTS, dtype=jnp.int32)
    onehot = ((i1[:, None] == experts) | (i2[:, None] == experts)).astype(jnp.int32)
    rank = jnp.cumsum(onehot, axis=0) - onehot
    tiles_e = (jnp.sum(onehot, axis=0) + tm_g - 1) // tm_g
    tile_end = jnp.cumsum(tiles_e)
    start = (tile_end - tiles_e) * tm_g
    pos1 = start[i1] + jnp.take_along_axis(rank, i1[:, None], axis=1)[:, 0]
    pos2 = start[i2] + jnp.take_along_axis(rank, i2[:, None], axis=1)[:, 0]
    n_tiles = -(-2 * n // tm_g) + N_EXPERTS
    tile_ids = jnp.arange(n_tiles, dtype=jnp.int32)
    tile_expert = jnp.minimum(jnp.sum((tile_ids[:, None] >= tile_end[None, :]).astype(jnp.int32), axis=1), N_EXPERTS - 1)
    tok = jnp.arange(n, dtype=jnp.int32)
    src = jnp.zeros((n_tiles * tm_g,), jnp.int32).at[pos1].set(tok).at[pos2].set(tok)
    src = _pad_cols(src.reshape(n_tiles, tm_g), IDX_ROW)
    return pos1, pos2, src, tile_expert.astype(jnp.int32), tile_end[-1:].astype(jnp.int32)


def _row_copy(src, src_row, dst, dst_row, sem):
    return pltpu.make_async_copy(src.at[pl.ds(src_row, 1), :], dst.at[pl.ds(dst_row, 1), :], sem)


def _gffn_body(te_ref, nu_ref, src_hbm, h_hbm, w1_ref, w3_ref, w2_ref, ys_ref,
               idx, xg, hb_ref, acc_ref, isem, dsem, *, tm):
    del te_ref
    m = pl.program_id(0)
    f = pl.program_id(1)
    last = pl.num_programs(1) - 1
    slot = m & 1
    used = m < nu_ref[0]

    def start_gather(tile, s):
        fetch = pltpu.make_async_copy(src_hbm.at[tile], idx.at[s], isem)
        fetch.start()
        fetch.wait()

        def issue(t, carry):
            _row_copy(h_hbm, idx[s, t], xg.at[s], t, dsem.at[s]).start()
            return carry

        lax.fori_loop(0, tm, issue, 0)

    @pl.when((m == 0) & (f == 0))
    def _():
        start_gather(0, 0)

    @pl.when(f == 0)
    def _():
        def drain(t, carry):
            _row_copy(h_hbm, 0, xg.at[slot], 0, dsem.at[slot]).wait()
            return carry

        lax.fori_loop(0, tm, drain, 0)
        hb_ref[...] = xg[slot].astype(BF16)
        acc_ref[...] = jnp.zeros_like(acc_ref)

    @pl.when((f == 0) & (m + 1 < pl.num_programs(0)))
    def _():
        start_gather(m + 1, 1 - slot)

    @pl.when(used)
    def _():
        h = hb_ref[...]
        a1 = _dot(h, w1_ref[...])
        a3 = _dot(h, w3_ref[...])
        acc_ref[...] += _dot((a1 * _sigmoid(a1) * a3).astype(BF16), w2_ref[...])

    @pl.when(f == last)
    def _():
        ys_ref[...] = acc_ref[...]


def _gffn(tile_expert, n_used, src, h, w1, w3, w2, tm_g, tf):
    E, D, F = w1.shape
    n_tiles = src.shape[0]
    nf = F // tf
    fidx = lambda m, f, nu: jnp.where(m < nu[0], f, nf - 1)
    any_spec = pl.BlockSpec(memory_space=pl.ANY)
    grid_spec = pltpu.PrefetchScalarGridSpec(
        num_scalar_prefetch=2, grid=(n_tiles, nf),
        in_specs=[any_spec, any_spec,
                  pl.BlockSpec((None, D, tf), lambda m, f, te, nu: (te[m], 0, fidx(m, f, nu))),
                  pl.BlockSpec((None, D, tf), lambda m, f, te, nu: (te[m], 0, fidx(m, f, nu))),
                  pl.BlockSpec((None, tf, D), lambda m, f, te, nu: (te[m], fidx(m, f, nu), 0))],
        out_specs=pl.BlockSpec((tm_g, D), lambda m, f, te, nu: (m, 0)),
        scratch_shapes=[pltpu.SMEM((2, IDX_ROW), jnp.int32), pltpu.VMEM((2, tm_g, D), F32),
                        pltpu.VMEM((tm_g, D), BF16), pltpu.VMEM((tm_g, D), F32),
                        pltpu.SemaphoreType.DMA, pltpu.SemaphoreType.DMA((2,))])
    return pl.pallas_call(
        functools.partial(_gffn_body, tm=tm_g), grid_spec=grid_spec,
        out_shape=jax.ShapeDtypeStruct((n_tiles * tm_g, D), F32),
        compiler_params=pltpu.CompilerParams(dimension_semantics=("arbitrary", "arbitrary"),
                                             vmem_limit_bytes=VMEM_LIMIT),
        name="moe_gffn",
    )(tile_expert, n_used, src, h, w1, w3, w2)


def _moe_combine_body(pos_hbm, ys_hbm, route_ref, x_ref, mod_ref, o_ref, idx, ya, yb, isem, dsem, *, tm):
    i = pl.program_id(0)
    fetch = pltpu.make_async_copy(pos_hbm.at[i], idx, isem)
    fetch.start()
    fetch.wait()

    def issue(t, carry):
        _row_copy(ys_hbm, idx[t], ya, t, dsem).start()
        _row_copy(ys_hbm, idx[tm + t], yb, t, dsem).start()
        return carry

    lax.fori_loop(0, tm, issue, 0)

    def drain(t, carry):
        _row_copy(ys_hbm, 0, ya, 0, dsem).wait()
        _row_copy(ys_hbm, 0, yb, 0, dsem).wait()
        return carry

    lax.fori_loop(0, tm, drain, 0)
    route = route_ref[...]
    y = route[:, 2:3] * ya[...] + route[:, 3:4] * yb[...]
    o_ref[...] = x_ref[...] + mod_ref[5:6, :] * y


def _moe_combine(pos1, pos2, ys, route, xf, modtab, lay, tm):
    n, D = xf.shape
    pos = jnp.concatenate([pos1.reshape(n // tm, tm), pos2.reshape(n // tm, tm)], axis=1)
    any_spec = pl.BlockSpec(memory_space=pl.ANY)
    row = lambda i: (i, 0)
    return pl.pallas_call(
        functools.partial(_moe_combine_body, tm=tm), grid=(n // tm,),
        in_specs=[any_spec, any_spec, pl.BlockSpec((tm, LANES), row), pl.BlockSpec((tm, D), row), lay.mod_spec(tm)],
        out_specs=pl.BlockSpec((tm, D), row),
        out_shape=jax.ShapeDtypeStruct((n, D), F32),
        scratch_shapes=[pltpu.SMEM((2 * tm,), jnp.int32), pltpu.VMEM((tm, D), F32), pltpu.VMEM((tm, D), F32),
                        pltpu.SemaphoreType.DMA, pltpu.SemaphoreType.DMA],
        compiler_params=pltpu.CompilerParams(dimension_semantics=("arbitrary",), vmem_limit_bytes=VMEM_LIMIT),
        name="moe_combine",
    )(pos, ys, route, xf, modtab)


def _ffn_body(h_ref, w1_ref, w3_ref, w2_ref, x_ref, mod_ref, o_ref, acc_ref):
    f = pl.program_id(1)

    @pl.when(f == 0)
    def _():
        acc_ref[...] = jnp.zeros_like(acc_ref)

    h = h_ref[...]
    a1 = _dot(h, w1_ref[...])
    a3 = _dot(h, w3_ref[...])
    acc_ref[...] += _dot((a1 * _sigmoid(a1) * a3).astype(BF16), w2_ref[...])

    @pl.when(f == pl.num_programs(1) - 1)
    def _():
        o_ref[...] = x_ref[...] + mod_ref[5:6, :] * acc_ref[...]


def _ffn(h, w1, w3, w2, xf, modtab, lay, tm, tf):
    D, F = w1.shape
    n_lat_tiles = lay.n_lat // tm
    T, B = lay.T, lay.B
    row = lambda m, f: (m, 0)
    return pl.pallas_call(
        _ffn_body, grid=(lay.n // tm, F // tf),
        in_specs=[pl.BlockSpec((tm, D), row),
                  pl.BlockSpec((D, tf), lambda m, f: (0, f)),
                  pl.BlockSpec((D, tf), lambda m, f: (0, f)),
                  pl.BlockSpec((tf, D), lambda m, f: (f, 0)),
                  pl.BlockSpec((tm, D), row),
                  pl.BlockSpec((None, 6, D), lambda m, f: (jnp.where(m < n_lat_tiles, (m * tm) // T, B), 0, 0))],
        out_specs=pl.BlockSpec((tm, D), row),
        out_shape=jax.ShapeDtypeStruct((lay.n, D), F32),
        scratch_shapes=[pltpu.VMEM((tm, D), F32)],
        compiler_params=pltpu.CompilerParams(dimension_semantics=("parallel", "arbitrary"),
                                             vmem_limit_bytes=VMEM_LIMIT),
        name="dense_ffn",
    )(h, w1, w3, w2, xf, modtab)


def _rwkv_in_body(*refs, vres):
    if vres:
        (h_ref, xx_ref, mix_ref, wr_ref, wk_ref, wv_ref, w1_ref, a1_ref, g1_ref, v1_ref,
         r_ref, k_ref, v_ref, tw_ref, ya_ref, sg_ref, yv_ref) = refs
    else:
        (h_ref, xx_ref, mix_ref, wr_ref, wk_ref, wv_ref, w1_ref, a1_ref, g1_ref,
         r_ref, k_ref, v_ref, tw_ref, ya_ref, sg_ref) = refs
    h = h_ref[...]
    xx = xx_ref[...]

    def mixed(j):
        return (h + xx * mix_ref[j:j + 1, :]).astype(BF16)

    r_ref[...] = _dot(mixed(0), wr_ref[...])
    tw_ref[...] = jnp.tanh(_dot(mixed(1), w1_ref[...]))
    k_ref[...] = _dot(mixed(2), wk_ref[...])
    xv = mixed(3)
    v_ref[...] = _dot(xv, wv_ref[...])
    if vres:
        yv_ref[...] = _dot(xv, v1_ref[...])
    ya_ref[...] = _dot(mixed(4), a1_ref[...])
    sg_ref[...] = _sigmoid(_dot(mixed(5), g1_ref[...]))


def _rwkv_feat_body(*refs, vres):
    if vres:
        (k_ref, v_ref, tw_ref, ya_ref, sg_ref, yv_ref, vf_ref, w2_ref, a2_ref, g2_ref, w0_ref, a0_ref, kk_ref_w,
         ones_ref, v2_ref, v0_ref, lw_ref, ad_ref, kk_ref, vo_ref, g_ref) = refs
    else:
        (k_ref, v_ref, tw_ref, ya_ref, sg_ref, w2_ref, a2_ref, g2_ref, w0_ref, a0_ref, kk_ref_w,
         ones_ref, lw_ref, ad_ref, kk_ref, vo_ref, g_ref) = refs
    dec = w0_ref[...] + _dot(tw_ref[...].astype(BF16), w2_ref[...])
    w_log = -_softplus(-dec) - 0.5
    lw_ref[...] = -jnp.exp(w_log)
    ad_ref[...] = _sigmoid(a0_ref[...] + _dot(ya_ref[...].astype(BF16), a2_ref[...]))
    g_ref[...] = _dot(sg_ref[...].astype(BF16), g2_ref[...])
    kkr = k_ref[...] * kk_ref_w[...]
    ss = _head_sum(kkr * kkr, ones_ref[...])
    kk_ref[...] = kkr / jnp.maximum(jnp.sqrt(ss), 1e-12)
    v = v_ref[...]
    if vres:
        v = v + (vf_ref[...] - v) * _sigmoid(v0_ref[...] + _dot(yv_ref[...].astype(BF16), v2_ref[...]))
    vo_ref[...] = v


def _scan_body(lw_ref, ad_ref, kf_ref, kk_ref, v_ref, r_ref, ka_ref, y_ref, s_ref):
    d = pl.program_id(1)
    s = pl.program_id(2)
    C = CHUNK
    D = kf_ref.shape[1]

    @pl.when(s == 0)
    def _():
        s_ref[...] = jnp.zeros_like(s_ref)

    sg = jnp.where(d == 0, 1, -1)
    ri = lax.broadcasted_iota(jnp.int32, (C, C), 0)
    ci = lax.broadcasted_iota(jnp.int32, (C, C), 1)
    dm = (ri - ci) * sg
    xr = ri ^ ci
    eye = (ri == ci).astype(F32)
    r2 = lax.broadcasted_iota(jnp.int32, (C, 2 * C), 0)
    c2 = lax.broadcasted_iota(jnp.int32, (C, 2 * C), 1) & (C - 1)
    dm2 = (r2 - c2) * sg
    strict2 = dm2 > 0
    incl2 = dm2 >= 0

    lw = lw_ref[...]
    ad = ad_ref[...]
    kf = kf_ref[...]
    kk = kk_ref[...]
    r = r_ref[...]
    c = jnp.dot((dm >= 0).astype(F32), lw, precision=HIGHEST, preferred_element_type=F32)
    pc = jnp.where(d == 0, c[C - 1:C, :], c[0:1, :])
    kd = kf * (1.0 + (ad - 1.0) * ka_ref[...])
    b = kk * ad
    einv = jnp.exp(-c)
    a_t = (-kk * jnp.exp(c - lw)).astype(BF16)
    r_t = (r * jnp.exp(c)).astype(BF16)
    b_t = (b * einv).astype(BF16)
    k_t = (kd * einv).astype(BF16)
    epc = jnp.exp(pc - c)
    b_p = (b * epc).astype(BF16)
    k_p = (kd * epc).astype(BF16)
    d_p = jnp.exp(pc)
    vb = v_ref[...].astype(BF16)
    zeros_cc = jnp.zeros((C, C), BF16)

    heads = range(D // C)
    sls = [slice(C * h, C * (h + 1)) for h in heads]
    ar = [jnp.concatenate([a_t[:, sl], r_t[:, sl]], axis=0) for sl in sls]
    bk = [jnp.concatenate([b_t[:, sl], k_t[:, sl]], axis=0) for sl in sls]
    sc = [lax.dot_general(ar[h], bk[h], _NT, preferred_element_type=F32) for h in heads]
    st = [s_ref[h] for h in heads]
    ars = [lax.dot_general(ar[h], st[h].astype(BF16), _NT, preferred_element_type=F32) for h in heads]
    top = [jnp.where(strict2, sc[h][0:C], 0.0) for h in heads]
    bot = [jnp.where(incl2, sc[h][C:], 0.0).astype(BF16) for h in heads]
    lab = [top[h][:, 0:C] for h in heads]
    zv = [jnp.concatenate([zeros_cc, vb[:, sl]], axis=0) for sl in sls]
    x = [ars[h][0:C] + _dot(top[h].astype(BF16), zv[h]) for h in heads]
    t = [eye + jnp.where(xr == 1, lab[h], 0.0) for h in heads]
    for lb in range(1, 6):
        level = (xr >> lb) == 1
        tb = [t[h].astype(BF16) for h in heads]
        ot = [_dot(jnp.where(level, lab[h], 0.0).astype(BF16), tb[h]).astype(BF16) for h in heads]
        t = [t[h] + _dot(tb[h], ot[h]) for h in heads]
    u = [_dot(t[h].astype(BF16), x[h].astype(BF16)) for h in heads]
    uv = [jnp.concatenate([u[h].astype(BF16), vb[:, sls[h]]], axis=0) for h in heads]
    for h in heads:
        y_ref[:, sls[h]] = ars[h][C:] + _dot(bot[h], uv[h])
    for h in heads:
        bkp = jnp.concatenate([b_p[:, sls[h]], k_p[:, sls[h]]], axis=0)
        s_ref[h] = st[h] * d_p[:, sls[h]] + lax.dot_general(uv[h], bkp, _TN, preferred_element_type=F32)


def _scan(lw, ad, kf, kk, v, r, k_a, lay):
    B, T, L, D = lay.B, lay.T, lay.L, lay.D
    nc_ctx = L // CHUNK
    nc_lat = T // CHUNK
    ctx_base = (B * T) // CHUNK

    def chunk_index(b, d, s):
        ctx_i = jnp.where(d == 0, s, nc_ctx - 1 - s)
        lat_i = jnp.where(d == 0, s - nc_ctx, nc_lat - 1 - (s - nc_ctx))
        return jnp.where(s < nc_ctx, ctx_base + b * nc_ctx + ctx_i, b * nc_lat + lat_i)

    per_dir = pl.BlockSpec((CHUNK, D), lambda b, d, s: (chunk_index(b, d, s), d))
    shared = pl.BlockSpec((CHUNK, D), lambda b, d, s: (chunk_index(b, d, s), 0))
    return pl.pallas_call(
        _scan_body, grid=(B, 2, nc_ctx + nc_lat),
        in_specs=[per_dir, per_dir, shared, shared, shared, shared,
                  pl.BlockSpec((1, D), lambda b, d, s: (0, 0))],
        out_specs=per_dir,
        out_shape=jax.ShapeDtypeStruct((lay.n, 2 * D), F32),
        scratch_shapes=[pltpu.VMEM((D // CHUNK, CHUNK, CHUNK), F32)],
        compiler_params=pltpu.CompilerParams(dimension_semantics=("parallel", "parallel", "arbitrary"),
                                             vmem_limit_bytes=VMEM_LIMIT),
        name="wkv_scan",
    )(lw, ad, kf, kk, v, r, k_a)


def _rwkv_post_body(y2_ref, r_ref, kf_ref, ad_ref, v_ref, g_ref, x_ref, mod_ref,
                    ka_ref, rk_ref, lnw_ref, lnb_ref, ones_ref, wo_ref, o_ref):
    D = x_ref.shape[1]
    ones4 = ones_ref[...]
    y2 = y2_ref[...]
    y = y2[:, :D] + y2[:, D:]
    inv_n = 1.0 / RWKV_HEAD
    yc = y - _head_sum(y, ones4) * inv_n
    var = _head_sum(yc * yc, ones4) * inv_n
    yn = yc * lax.rsqrt(var + LN_X_EPS) * lnw_ref[...] + lnb_ref[...]
    kf = kf_ref[...]
    ad = ad_ref[...]
    ka = ka_ref[...]
    k_bonus = 0.5 * (kf * (1.0 + (ad[:, :D] - 1.0) * ka) + kf * (1.0 + (ad[:, D:] - 1.0) * ka))
    bonus = _head_sum(r_ref[...] * k_bonus * rk_ref[...], ones4) * v_ref[...]
    o = ((yn + bonus) * g_ref[...]).astype(BF16)
    o_ref[...] = x_ref[...] + mod_ref[2:3, :] * _dot(o, wo_ref[...])


def _token_shift(h, lay):
    def centred(z):
        zp = jnp.pad(z, ((0, 0), (1, 1), (0, 0)))
        return 0.5 * (zp[:, :-2] + zp[:, 2:]) - z
    B, T, L, D = lay.B, lay.T, lay.L, lay.D
    hl = centred(h[:lay.n_lat].reshape(B, T, D)).reshape(B * T, D)
    hc = centred(h[lay.n_lat:].reshape(B, L, D)).reshape(B * L, D)
    return jnp.concatenate([hl, hc], axis=0)


def _rope_tables(T, tm):
    pos = jnp.arange(T)
    inv_freq = 1.0 / (ROPE_BASE ** (jnp.arange(AXIS_PAIRS, dtype=F32) / AXIS_PAIRS))
    ang_r = (pos // GRID_W).astype(F32)[:, None] * inv_freq[None, :]
    ang_c = (pos % GRID_W).astype(F32)[:, None] * inv_freq[None, :]
    pad1 = jnp.ones((T, LANES - QK_ROPE), F32)
    pad0 = jnp.zeros((T, LANES - QK_ROPE), F32)
    cos = jnp.concatenate([jnp.cos(ang_r), jnp.cos(ang_r), jnp.cos(ang_c), jnp.cos(ang_c), pad1], axis=1)
    sin = jnp.concatenate([-jnp.sin(ang_r), jnp.sin(ang_r), -jnp.sin(ang_c), jnp.sin(ang_c), pad0], axis=1)
    cos = jnp.concatenate([cos, jnp.ones((tm, LANES), F32)], axis=0)
    sin = jnp.concatenate([sin, jnp.zeros((tm, LANES), F32)], axis=0)
    return cos, sin


def _pad_cols(w, width):
    return jnp.pad(w, ((0, 0), (0, width - w.shape[1])))


def _pad_rows(w, height):
    return jnp.pad(w, ((0, height - w.shape[0]), (0, 0)))


def _block_diag2(w0, w1):
    z0 = jnp.zeros_like(w0)
    z1 = jnp.zeros_like(w1)
    return jnp.concatenate([jnp.concatenate([w0, z1], axis=1), jnp.concatenate([z0, w1], axis=1)], axis=0)


def kernel(x, c, ctx, c_ctx, ada_w, ada_b, norm1_g, norm2_g, mla_wqa, mla_qa_norm, mla_wqb, mla_wkva, mla_kva_norm, mla_wkvb, mla_q_norm, mla_k_norm, mla_wo, rwkv_mix, rwkv_wr, rwkv_wk, rwkv_wv, rwkv_wo, rwkv_w0, rwkv_w1, rwkv_w2, rwkv_a0, rwkv_a1, rwkv_a2, rwkv_g1, rwkv_g2, rwkv_k_k, rwkv_k_a, rwkv_r_k, rwkv_ln_w, rwkv_ln_b, rwkv_v0, rwkv_v1, rwkv_v2, ffn_w1, ffn_w3, ffn_w2, moe_router, moe_w1, moe_w3, moe_w2):
    B, T, D = x.shape
    L = ctx.shape[1]
    depth = ada_w.shape[0]
    lay = _Layout(B, T, L, D)
    n = lay.n
    tm = min(512, T)
    tm_small = min(256, T)
    assert T % tm == 0 and (B * L) % tm == 0 and T % GRID_W == 0 and L % CHUNK == 0 and B + 1 <= 16

    xf = jnp.concatenate([x.reshape(B * T, D), ctx.reshape(B * L, D)], axis=0)
    cc = jnp.concatenate([c, c_ctx[None, :], jnp.zeros((16 - B - 1, D), F32)], axis=0)
    modall = _ada_call(cc, ada_w, ada_b)[:, :B + 1].reshape(depth, B + 1, 6, D)
    cos_t, sin_t = _rope_tables(T, tm)
    ones4 = jnp.kron(jnp.eye(4, dtype=F32), jnp.ones((RWKV_HEAD, RWKV_HEAD), F32)).astype(BF16)
    row1 = lambda z: z.reshape(1, -1)

    v_first = None
    for i in range(depth):
        j = i // 2
        modtab = modall[i]
        mod_t = (modtab, lay.mod_spec(tm))
        mod_s = (modtab, lay.mod_spec(tm_small))
        if i % 2 == 0:
            wa = jnp.concatenate([mla_wqa[j], _pad_cols(mla_wkva[j], KV_LORA + LANES)], axis=1).astype(BF16)
            wqb = mla_wqb[j].reshape(Q_LORA, MLA_HEADS, QK_HEAD)
            wqb = jnp.pad(wqb, ((0, 0), (0, 0), (0, QK_PAD - QK_HEAD))).reshape(Q_LORA, MLA_HEADS * QK_PAD).astype(BF16)
            wkvb = mla_wkvb[j].reshape(KV_LORA, MLA_HEADS, QK_NOPE + V_HEAD)
            wk = wkvb[:, :, :QK_NOPE].reshape(KV_LORA, MLA_HEADS * QK_NOPE).astype(BF16)
            wv = wkvb[:, :, QK_NOPE:].reshape(KV_LORA, MLA_HEADS * V_HEAD).astype(BF16)
            gq = _pad_cols(row1(mla_q_norm[j]) * SM_SCALE, QK_PAD)
            gk = _pad_cols(row1(mla_k_norm[j]), QK_PAD)
            cq, ckv, kr = _rowwise(
                _mla_a_body, "mla_a", n, tm, [xf], [mod_t],
                [row1(norm1_g[i]), wa, row1(mla_qa_norm[j]), row1(mla_kva_norm[j])],
                [(Q_LORA, BF16), (KV_LORA, BF16), (LANES, F32)])
            rope_in = [(cos_t, lay.rope_spec(tm)), (sin_t, lay.rope_spec(tm))]
            (q,) = _rowwise(_mla_q_body, "mla_q", n, tm, [cq], rope_in, [wqb, gq], [(MLA_HEADS * QK_PAD, BF16)])
            k, v = _rowwise(_mla_kv_body, "mla_kv", n, tm, [ckv, kr], rope_in, [wk, wv, gk],
                            [(MLA_HEADS * QK_PAD, BF16), (MLA_HEADS * V_PAD, BF16)])
            o = _attention(q, k, v, lay, tq=min(1024, T))
            (xf,) = _rowwise(_proj_resid_body, "mla_out", n, tm, [o, xf], [mod_t], [mla_wo[j].astype(BF16)], [(D, F32)])
        else:
            vres = j > 0
            (h,) = _rowwise(functools.partial(_norm_body, si=0), "norm1", n, tm, [xf], [mod_t],
                            [row1(norm1_g[i])], [(D, F32)])
            xx = _token_shift(h, lay)
            mix = _pad_rows(rwkv_mix[j], 8)
            w1c = jnp.concatenate([rwkv_w1[j, 0], rwkv_w1[j, 1]], axis=1).astype(BF16)
            a1c = jnp.concatenate([rwkv_a1[j, 0], rwkv_a1[j, 1]], axis=1).astype(BF16)
            g1p = _pad_cols(rwkv_g1[j], 2 * LANES).astype(BF16)
            consts = [mix, rwkv_wr[j].astype(BF16), rwkv_wk[j].astype(BF16), rwkv_wv[j].astype(BF16), w1c, a1c, g1p]
            outs = [(D, F32), (D, F32), (D, F32), (LANES, F32), (LANES, F32), (2 * LANES, F32)]
            if vres:
                consts.append(_pad_cols(rwkv_v1[j - 1], LANES).astype(BF16))
                outs.append((LANES, F32))
            res = _rowwise(functools.partial(_rwkv_in_body, vres=vres), "rwkv_in", n, tm_small, [h, xx], [], consts, outs)
            r, kf, v_raw, tw, ya, sgate = res[:6]
            w2c = _block_diag2(rwkv_w2[j, 0], rwkv_w2[j, 1]).astype(BF16)
            a2c = _block_diag2(rwkv_a2[j, 0], rwkv_a2[j, 1]).astype(BF16)
            g2p = _pad_rows(rwkv_g2[j], 2 * LANES).astype(BF16)
            rows = [kf, v_raw, tw, ya, sgate]
            consts = [w2c, a2c, g2p, rwkv_w0[j].reshape(1, 2 * D), rwkv_a0[j].reshape(1, 2 * D), row1(rwkv_k_k[j]), ones4]
            if vres:
                rows += [res[6], v_first]
                consts += [_pad_rows(rwkv_v2[j - 1], LANES).astype(BF16), row1(rwkv_v0[j - 1])]
            lw, ad, kk, v_out, gate = _rowwise(
                functools.partial(_rwkv_feat_body, vres=vres), "rwkv_feat", n, tm_small, rows, [], consts,
                [(2 * D, F32), (2 * D, F32), (D, F32), (D, F32), (D, F32)])
            if j == 0:
                v_first = v_out
            k_a = row1(rwkv_k_a[j])
            y2 = _scan(lw, ad, kf, kk, v_out, r, k_a, lay)
            (xf,) = _rowwise(
                _rwkv_post_body, "rwkv_out", n, tm_small, [y2, r, kf, ad, v_out, gate, xf], [mod_s],
                [k_a, row1(rwkv_r_k[j]), row1(rwkv_ln_w[j]), row1(rwkv_ln_b[j]), ones4, rwkv_wo[j].astype(BF16)],
                [(D, F32)])

        if i % 2 == 0:
            (h,) = _rowwise(functools.partial(_norm_body, si=3), "norm2", n, tm, [xf], [mod_t],
                            [row1(norm2_g[i])], [(D, BF16)])
            xf = _ffn(h, ffn_w1[j].astype(BF16), ffn_w3[j].astype(BF16), ffn_w2[j].astype(BF16),
                      xf, modtab, lay, tm, tf=ffn_w1.shape[2] // 2)
        else:
            h, route = _rowwise(_router_body, "router", n, tm, [xf], [mod_t],
                                [row1(norm2_g[i]), _pad_cols(moe_router[j], LANES)], [(D, F32), (LANES, F32)])
            pos1, pos2, src, tile_expert, n_used = _route_plan(route, n, MOE_TM)
            ys = _gffn(tile_expert, n_used, src, h, moe_w1[j].astype(BF16), moe_w3[j].astype(BF16),
                       moe_w2[j].astype(BF16), MOE_TM, tf=moe_w1.shape[3] // 4)
            xf = _moe_combine(pos1, pos2, ys, route, xf, modtab, lay, tm)
    return xf[:B * T].reshape(B, T, D)
```

```python
import functools

import jax
import jax.numpy as jnp
from jax import lax
from jax.experimental import pallas as pl
from jax.experimental.pallas import tpu as pltpu

F32 = jnp.float32
BF16 = jnp.bfloat16
HIGHEST = lax.Precision.HIGHEST

GRID_W = 64
NORM_EPS = 1e-6
MLA_HEADS = 8
QK_NOPE = 128
QK_ROPE = 64
V_HEAD = 128
QK_HEAD = QK_NOPE + QK_ROPE
Q_LORA = 384
KV_LORA = 256
ROPE_BASE = 10000.0
AXIS_PAIRS = QK_ROPE // 4
SM_SCALE = QK_HEAD ** -0.5
RWKV_HEAD = 64
LN_X_EPS = 64e-5
N_EXPERTS = 8

LANES = 128
QK_PAD = 2 * LANES
V_PAD = 2 * LANES
CHUNK = 64
MOE_TM = 512
IDX_ROW = 1024
ATTN_SUB = 256
DMA_UNROLL = 8
VMEM_LIMIT = 56 * 1024 * 1024

_NT = (((1,), (1,)), ((), ()))
_TN = (((0,), (0,)), ((), ()))


def _dot(a, b):
    return jnp.dot(a, b, preferred_element_type=F32)


def _sigmoid(z):
    return 1.0 / (1.0 + jnp.exp(-z))


def _softplus(z):
    return jnp.maximum(z, 0.0) + jnp.log(1.0 + jnp.exp(-jnp.abs(z)))


def _rms(z, width):
    return lax.rsqrt(jnp.sum(z * z, axis=-1, keepdims=True) * (1.0 / width) + NORM_EPS)


def _norm_mod(x, g, shift, scale):
    return (x * _rms(x, x.shape[-1])) * g * (1.0 + scale) + shift


def _rope(z, cos, sin):
    lane = lax.broadcasted_iota(jnp.int32, z.shape, 1)
    partner = jnp.where((lane & 16) == 0, pltpu.roll(z, LANES - 16, axis=1), pltpu.roll(z, 16, axis=1))
    return z * cos + partner * sin


def _head_sum(z, ones4):
    outs = []
    for q in range(z.shape[1] // 256):
        blk = z[:, 256 * q:256 * (q + 1)]
        hi = blk.astype(BF16)
        lo = (blk - hi.astype(F32)).astype(BF16)
        outs.append(_dot(hi, ones4) + _dot(lo, ones4))
    return jnp.concatenate(outs, axis=1)


class _Layout:
    def __init__(self, B, T, L, D):
        self.B, self.T, self.L, self.D = B, T, L, D
        self.n_lat = B * T
        self.n = B * T + B * L

    def mod_spec(self, tm):
        n_lat_tiles = self.n_lat // tm
        T, B, D = self.T, self.B, self.D
        return pl.BlockSpec((None, 6, D), lambda i: (jnp.where(i < n_lat_tiles, (i * tm) // T, B), 0, 0))

    def rope_spec(self, tm):
        n_lat_tiles = self.n_lat // tm
        tpb = self.T // tm
        return pl.BlockSpec((tm, LANES), lambda i: (jnp.where(i < n_lat_tiles, i % tpb, tpb), 0))


def _const_spec(arr):
    nd = arr.ndim
    return pl.BlockSpec(arr.shape, lambda i: (0,) * nd, pipeline_mode=pl.Buffered(1))


def _rowwise(body, name, n_rows, tm, row_ins, tile_ins, const_ins, outs):
    in_specs = [pl.BlockSpec((tm, a.shape[1]), lambda i: (i, 0)) for a in row_ins]
    in_specs += [spec for (_, spec) in tile_ins]
    in_specs += [_const_spec(a) for a in const_ins]
    out_specs = [pl.BlockSpec((tm, w), lambda i: (i, 0)) for (w, _) in outs]
    out_shape = [jax.ShapeDtypeStruct((n_rows, w), dt) for (w, dt) in outs]
    return pl.pallas_call(
        body, grid=(n_rows // tm,), in_specs=in_specs, out_specs=out_specs, out_shape=out_shape,
        compiler_params=pltpu.CompilerParams(dimension_semantics=("parallel",), vmem_limit_bytes=VMEM_LIMIT),
        name=name,
    )(*row_ins, *[a for (a, _) in tile_ins], *const_ins)


def _ada_body(c_ref, w_ref, b_ref, o_ref):
    c = c_ref[...]
    o_ref[...] = jnp.dot(c * _sigmoid(c), w_ref[...], precision=HIGHEST, preferred_element_type=F32) + b_ref[...]


def _ada_call(cc, ada_w, ada_b):
    depth, D, W = ada_w.shape
    tn = 1024
    return pl.pallas_call(
        _ada_body, grid=(depth, W // tn),
        in_specs=[pl.BlockSpec(cc.shape, lambda l, j: (0, 0)),
                  pl.BlockSpec((None, D, tn), lambda l, j: (l, 0, j)),
                  pl.BlockSpec((None, 1, tn), lambda l, j: (l, 0, j))],
        out_specs=pl.BlockSpec((None, cc.shape[0], tn), lambda l, j: (l, 0, j)),
        out_shape=jax.ShapeDtypeStruct((depth, cc.shape[0], W), F32),
        compiler_params=pltpu.CompilerParams(dimension_semantics=("parallel", "parallel"), vmem_limit_bytes=VMEM_LIMIT),
        name="ada",
    )(cc, ada_w, ada_b.reshape(depth, 1, W))


def _norm_body(x_ref, mod_ref, g_ref, o_ref, *, si):
    h = _norm_mod(x_ref[...], g_ref[...], mod_ref[si:si + 1, :], mod_ref[si + 1:si + 2, :])
    o_ref[...] = h.astype(o_ref.dtype)


def _mla_a_body(x_ref, mod_ref, g_ref, wa_ref, qan_ref, kvn_ref, cq_ref, ckv_ref, kr_ref):
    h = _norm_mod(x_ref[...], g_ref[...], mod_ref[0:1, :], mod_ref[1:2, :]).astype(BF16)
    acc = _dot(h, wa_ref[...])
    qa = acc[:, 0:Q_LORA]
    kv = acc[:, Q_LORA:Q_LORA + KV_LORA]
    cq_ref[...] = (qa * _rms(qa, Q_LORA) * qan_ref[...]).astype(BF16)
    ckv_ref[...] = (kv * _rms(kv, KV_LORA) * kvn_ref[...]).astype(BF16)
    kr_ref[...] = acc[:, Q_LORA + KV_LORA:]


def _mla_q_body(cq_ref, cos_ref, sin_ref, wqb_ref, gq_ref, q_ref):
    acc = _dot(cq_ref[...], wqb_ref[...])
    cos = cos_ref[...]
    sin = sin_ref[...]
    g = gq_ref[...]
    for h in range(MLA_HEADS):
        a = acc[:, QK_PAD * h:QK_PAD * (h + 1)]
        an = a * _rms(a, QK_HEAD) * g
        q_ref[:, QK_PAD * h:QK_PAD * h + LANES] = an[:, :LANES].astype(BF16)
        q_ref[:, QK_PAD * h + LANES:QK_PAD * (h + 1)] = _rope(an[:, LANES:], cos, sin).astype(BF16)


def _mla_kv_body(ckv_ref, kr_ref, cos_ref, sin_ref, wk_ref, wv_ref, gk_ref, k_ref, v_ref):
    ckv = ckv_ref[...]
    kn = _dot(ckv, wk_ref[...])
    vals = _dot(ckv, wv_ref[...]).astype(BF16)
    lane = lax.broadcasted_iota(jnp.int32, (vals.shape[0], LANES), 1)
    ones_col = jnp.where(lane == 0, 1.0, 0.0).astype(BF16)
    for h in range(MLA_HEADS):
        v_ref[:, V_PAD * h:V_PAD * h + V_HEAD] = vals[:, V_HEAD * h:V_HEAD * (h + 1)]
        v_ref[:, V_PAD * h + V_HEAD:V_PAD * (h + 1)] = ones_col
    kr = kr_ref[...]
    ss_rope = jnp.sum(kr * kr, axis=-1, keepdims=True)
    g = gk_ref[...]
    g_nope = g[:, :LANES]
    kr_rot = _rope(kr * g[:, LANES:], cos_ref[...], sin_ref[...])
    for h in range(MLA_HEADS):
        a = kn[:, LANES * h:LANES * (h + 1)]
        inv = lax.rsqrt((jnp.sum(a * a, axis=-1, keepdims=True) + ss_rope) * (1.0 / QK_HEAD) + NORM_EPS)
        k_ref[:, QK_PAD * h:QK_PAD * h + LANES] = (a * inv * g_nope).astype(BF16)
        k_ref[:, QK_PAD * h + LANES:QK_PAD * (h + 1)] = (kr_rot * inv).astype(BF16)


def _attn_body(*refs, nseg, sub):
    q_ref = refs[0]
    o_ref = refs[-1]
    n_sub = q_ref.shape[0] // sub

    def scores(j):
        q = q_ref[sub * j:sub * (j + 1), :]
        return [lax.dot_general(q, refs[1 + 2 * i][...], _NT, preferred_element_type=F32) for i in range(nseg)]

    def finish(j, ss):
        m = functools.reduce(jnp.maximum, [jnp.max(s, axis=-1, keepdims=True) for s in ss])
        acc = functools.reduce(lambda a, b: a + b,
                               [_dot(jnp.exp((ss[i] - m).astype(BF16)), refs[2 + 2 * i][...]) for i in range(nseg)])
        o_ref[sub * j:sub * (j + 1), :] = (acc[:, :V_HEAD] / acc[:, V_HEAD:V_HEAD + 1]).astype(o_ref.dtype)

    pending = scores(0)
    for j in range(n_sub):
        following = scores(j + 1) if j + 1 < n_sub else None
        finish(j, pending)
        pending = following


def _attention(q, k, v, lay, tq):
    B, T, L = lay.B, lay.T, lay.L
    nq = T // tq
    ctx0 = (B * T) // L
    params = pltpu.CompilerParams(dimension_semantics=("parallel", "parallel", "arbitrary"),
                                  vmem_limit_bytes=VMEM_LIMIT)
    o_lat = pl.pallas_call(
        functools.partial(_attn_body, nseg=2, sub=min(ATTN_SUB, tq)), grid=(B, MLA_HEADS, nq),
        in_specs=[pl.BlockSpec((tq, QK_PAD), lambda b, h, i: (b * nq + i, h)),
                  pl.BlockSpec((L, QK_PAD), lambda b, h, i: (ctx0 + b, h)),
                  pl.BlockSpec((L, V_PAD), lambda b, h, i: (ctx0 + b, h)),
                  pl.BlockSpec((T, QK_PAD), lambda b, h, i: (b, h)),
                  pl.BlockSpec((T, V_PAD), lambda b, h, i: (b, h))],
        out_specs=pl.BlockSpec((tq, V_HEAD), lambda b, h, i: (b * nq + i, h)),
        out_shape=jax.ShapeDtypeStruct((B * T, MLA_HEADS * V_HEAD), BF16),
        compiler_params=params, name="attn_latent",
    )(q, k, v, k, v)
    o_ctx = pl.pallas_call(
        functools.partial(_attn_body, nseg=1, sub=min(ATTN_SUB, L)), grid=(B, MLA_HEADS, 1),
        in_specs=[pl.BlockSpec((L, QK_PAD), lambda b, h, i: (ctx0 + b, h)),
                  pl.BlockSpec((L, QK_PAD), lambda b, h, i: (ctx0 + b, h)),
                  pl.BlockSpec((L, V_PAD), lambda b, h, i: (ctx0 + b, h))],
        out_specs=pl.BlockSpec((L, V_HEAD), lambda b, h, i: (b, h)),
        out_shape=jax.ShapeDtypeStruct((B * L, MLA_HEADS * V_HEAD), BF16),
        compiler_params=params, name="attn_ctx",
    )(q, k, v)
    return jnp.concatenate([o_lat, o_ctx], axis=0)


def _proj_resid_body(a_ref, x_ref, mod_ref, w_ref, o_ref):
    o_ref[...] = x_ref[...] + mod_ref[2:3, :] * _dot(a_ref[...], w_ref[...])


def _router_body(x_ref, mod_ref, g_ref, router_ref, h_ref, route_ref):
    hf = _norm_mod(x_ref[...], g_ref[...], mod_ref[3:4, :], mod_ref[4:5, :])
    h_ref[...] = hf
    logits = jnp.dot(hf, router_ref[...], precision=HIGHEST, preferred_element_type=F32)
    lane = lax.broadcasted_iota(jnp.int32, logits.shape, 1)
    lanef = lane.astype(F32)
    neg = jnp.float32(-1e30)
    lg = jnp.where(lane < N_EXPERTS, logits, neg)
    m1 = jnp.max(lg, axis=-1, keepdims=True)
    i1 = jnp.min(jnp.where(lg == m1, lanef, float(LANES)), axis=-1, keepdims=True)
    lg2 = jnp.where(lanef == i1, neg, lg)
    m2 = jnp.max(lg2, axis=-1, keepdims=True)
    i2 = jnp.min(jnp.where(lg2 == m2, lanef, float(LANES)), axis=-1, keepdims=True)
    e2 = jnp.exp(m2 - m1)
    w1 = 1.0 / (1.0 + e2)
    w2 = e2 / (1.0 + e2)
    route_ref[...] = jnp.where(lane == 0, i1, jnp.where(lane == 1, i2, jnp.where(lane == 2, w1, jnp.where(lane == 3, w2, 0.0))))


def _route_plan(route, n, tm_g):
    i1 = route[:, 0].astype(jnp.int32)
    i2 = route[:, 1].astype(jnp.int32)
    experts = jnp.arange(N_EXPERTS, dtype=jnp.int32)
    onehot = ((i1[:, None] == experts) | (i2[:, None] == experts)).astype(jnp.int32)
    rank = jnp.cumsum(onehot, axis=0) - onehot
    tiles_e = (jnp.sum(onehot, axis=0) + tm_g - 1) // tm_g
    tile_end = jnp.cumsum(tiles_e)
    start = (tile_end - tiles_e) * tm_g
    pos1 = start[i1] + jnp.take_along_axis(rank, i1[:, None], axis=1)[:, 0]
    pos2 = start[i2] + jnp.take_along_axis(rank, i2[:, None], axis=1)[:, 0]
    n_tiles = -(-2 * n // tm_g) + N_EXPERTS
    tile_ids = jnp.arange(n_tiles, dtype=jnp.int32)
    tile_expert = jnp.minimum(jnp.sum((tile_ids[:, None] >= tile_end[None, :]).astype(jnp.int32), axis=1), N_EXPERTS - 1)
    tok = jnp.arange(n, dtype=jnp.int32)
    src = jnp.zeros((n_tiles * tm_g,), jnp.int32).at[pos1].set(tok).at[pos2].set(tok)
    src = _pad_cols(src.reshape(n_tiles, tm_g), IDX_ROW)
    return pos1, pos2, src, tile_expert.astype(jnp.int32), tile_end[-1:].astype(jnp.int32)


def _row_copy(src, src_row, dst, dst_row, sem):
    return pltpu.make_async_copy(src.at[pl.ds(src_row, 1), :], dst.at[pl.ds(dst_row, 1), :], sem)


def _gffn_body(te_ref, nu_ref, src_hbm, h_hbm, w1_ref, w3_ref, w2_ref, ys_ref,
               idx, xg, hb_ref, acc_ref, isem, dsem, *, tm):
    del te_ref
    m = pl.program_id(0)
    f = pl.program_id(1)
    last = pl.num_programs(1) - 1
    slot = m & 1
    used = m < nu_ref[0]

    def start_gather(tile, s):
        fetch = pltpu.make_async_copy(src_hbm.at[tile], idx.at[s], isem)
        fetch.start()
        fetch.wait()

        def issue(t, carry):
            _row_copy(h_hbm, idx[s, t], xg.at[s], t, dsem.at[s]).start()
            return carry

        lax.fori_loop(0, tm, issue, 0, unroll=DMA_UNROLL)

    @pl.when((m == 0) & (f == 0))
    def _():
        start_gather(0, 0)

    @pl.when(f == 0)
    def _():
        pltpu.make_async_copy(xg.at[slot], xg.at[slot], dsem.at[slot]).wait()
        hb_ref[...] = xg[slot].astype(BF16)
        acc_ref[...] = jnp.zeros_like(acc_ref)

    @pl.when((f == 0) & (m + 1 < pl.num_programs(0)))
    def _():
        start_gather(m + 1, 1 - slot)

    @pl.when(used)
    def _():
        h = hb_ref[...]
        a1 = _dot(h, w1_ref[...])
        a3 = _dot(h, w3_ref[...])
        acc_ref[...] += _dot((a1 * _sigmoid(a1) * a3).astype(BF16), w2_ref[...])

    @pl.when(f == last)
    def _():
        ys_ref[...] = acc_ref[...]


def _gffn(tile_expert, n_used, src, h, w1, w3, w2, tm_g, tf):
    E, D, F = w1.shape
    n_tiles = src.shape[0]
    nf = F // tf
    fidx = lambda m, f, nu: jnp.where(m < nu[0], f, nf - 1)
    any_spec = pl.BlockSpec(memory_space=pl.ANY)
    grid_spec = pltpu.PrefetchScalarGridSpec(
        num_scalar_prefetch=2, grid=(n_tiles, nf),
        in_specs=[any_spec, any_spec,
                  pl.BlockSpec((None, D, tf), lambda m, f, te, nu: (te[m], 0, fidx(m, f, nu))),
                  pl.BlockSpec((None, D, tf), lambda m, f, te, nu: (te[m], 0, fidx(m, f, nu))),
                  pl.BlockSpec((None, tf, D), lambda m, f, te, nu: (te[m], fidx(m, f, nu), 0))],
        out_specs=pl.BlockSpec((tm_g, D), lambda m, f, te, nu: (m, 0)),
        scratch_shapes=[pltpu.SMEM((2, IDX_ROW), jnp.int32), pltpu.VMEM((2, tm_g, D), F32),
                        pltpu.VMEM((tm_g, D), BF16), pltpu.VMEM((tm_g, D), F32),
                        pltpu.SemaphoreType.DMA, pltpu.SemaphoreType.DMA((2,))])
    return pl.pallas_call(
        functools.partial(_gffn_body, tm=tm_g), grid_spec=grid_spec,
        out_shape=jax.ShapeDtypeStruct((n_tiles * tm_g, D), F32),
        compiler_params=pltpu.CompilerParams(dimension_semantics=("arbitrary", "arbitrary"),
                                             vmem_limit_bytes=VMEM_LIMIT),
        name="moe_gffn",
    )(tile_expert, n_used, src, h, w1, w3, w2)


def _moe_combine_body(pos_hbm, ys_hbm, route_ref, x_ref, mod_ref, o_ref, idx, yg, isem, dsem, *, tm):
    i = pl.program_id(0)
    slot = i & 1

    def start_gather(tile, s):
        fetch = pltpu.make_async_copy(pos_hbm.at[tile], idx.at[s], isem)
        fetch.start()
        fetch.wait()

        def issue(t, carry):
            _row_copy(ys_hbm, idx[s, t], yg.at[s, 0], t, dsem.at[s]).start()
            _row_copy(ys_hbm, idx[s, tm + t], yg.at[s, 1], t, dsem.at[s]).start()
            return carry

        lax.fori_loop(0, tm, issue, 0, unroll=DMA_UNROLL)

    @pl.when(i == 0)
    def _():
        start_gather(0, 0)

    pltpu.make_async_copy(yg.at[slot], yg.at[slot], dsem.at[slot]).wait()

    @pl.when(i + 1 < pl.num_programs(0))
    def _():
        start_gather(i + 1, 1 - slot)

    route = route_ref[...]
    y = route[:, 2:3] * yg[slot, 0] + route[:, 3:4] * yg[slot, 1]
    o_ref[...] = x_ref[...] + mod_ref[5:6, :] * y


def _moe_combine(pos1, pos2, ys, route, xf, modtab, lay, tm):
    n, D = xf.shape
    pos = jnp.concatenate([pos1.reshape(n // tm, tm), pos2.reshape(n // tm, tm)], axis=1)
    any_spec = pl.BlockSpec(memory_space=pl.ANY)
    row = lambda i: (i, 0)
    return pl.pallas_call(
        functools.partial(_moe_combine_body, tm=tm), grid=(n // tm,),
        in_specs=[any_spec, any_spec, pl.BlockSpec((tm, LANES), row), pl.BlockSpec((tm, D), row), lay.mod_spec(tm)],
        out_specs=pl.BlockSpec((tm, D), row),
        out_shape=jax.ShapeDtypeStruct((n, D), F32),
        scratch_shapes=[pltpu.SMEM((2, 2 * tm), jnp.int32), pltpu.VMEM((2, 2, tm, D), F32),
                        pltpu.SemaphoreType.DMA, pltpu.SemaphoreType.DMA((2,))],
        compiler_params=pltpu.CompilerParams(dimension_semantics=("arbitrary",), vmem_limit_bytes=VMEM_LIMIT),
        name="moe_combine",
    )(pos, ys, route, xf, modtab)


def _ffn_body(h_ref, w1_ref, w3_ref, w2_ref, x_ref, mod_ref, o_ref, acc_ref):
    f = pl.program_id(1)

    @pl.when(f == 0)
    def _():
        acc_ref[...] = jnp.zeros_like(acc_ref)

    h = h_ref[...]
    a1 = _dot(h, w1_ref[...])
    a3 = _dot(h, w3_ref[...])
    acc_ref[...] += _dot((a1 * _sigmoid(a1) * a3).astype(BF16), w2_ref[...])

    @pl.when(f == pl.num_programs(1) - 1)
    def _():
        o_ref[...] = x_ref[...] + mod_ref[5:6, :] * acc_ref[...]


def _ffn(h, w1, w3, w2, xf, modtab, lay, tm, tf):
    D, F = w1.shape
    n_lat_tiles = lay.n_lat // tm
    T, B = lay.T, lay.B
    row = lambda m, f: (m, 0)
    return pl.pallas_call(
        _ffn_body, grid=(lay.n // tm, F // tf),
        in_specs=[pl.BlockSpec((tm, D), row),
                  pl.BlockSpec((D, tf), lambda m, f: (0, f)),
                  pl.BlockSpec((D, tf), lambda m, f: (0, f)),
                  pl.BlockSpec((tf, D), lambda m, f: (f, 0)),
                  pl.BlockSpec((tm, D), row),
                  pl.BlockSpec((None, 6, D), lambda m, f: (jnp.where(m < n_lat_tiles, (m * tm) // T, B), 0, 0))],
        out_specs=pl.BlockSpec((tm, D), row),
        out_shape=jax.ShapeDtypeStruct((lay.n, D), F32),
        scratch_shapes=[pltpu.VMEM((tm, D), F32)],
        compiler_params=pltpu.CompilerParams(dimension_semantics=("parallel", "arbitrary"),
                                             vmem_limit_bytes=VMEM_LIMIT),
        name="dense_ffn",
    )(h, w1, w3, w2, xf, modtab)


def _rwkv_in_body(*refs, vres):
    if vres:
        (h_ref, xx_ref, mix_ref, wr_ref, wk_ref, wv_ref, w1_ref, a1_ref, g1_ref, v1_ref,
         r_ref, k_ref, v_ref, tw_ref, ya_ref, sg_ref, yv_ref) = refs
    else:
        (h_ref, xx_ref, mix_ref, wr_ref, wk_ref, wv_ref, w1_ref, a1_ref, g1_ref,
         r_ref, k_ref, v_ref, tw_ref, ya_ref, sg_ref) = refs
    h = h_ref[...]
    xx = xx_ref[...]

    def mixed(j):
        return (h + xx * mix_ref[j:j + 1, :]).astype(BF16)

    r_ref[...] = _dot(mixed(0), wr_ref[...])
    tw_ref[...] = jnp.tanh(_dot(mixed(1), w1_ref[...]))
    k_ref[...] = _dot(mixed(2), wk_ref[...])
    xv = mixed(3)
    v_ref[...] = _dot(xv, wv_ref[...])
    if vres:
        yv_ref[...] = _dot(xv, v1_ref[...])
    ya_ref[...] = _dot(mixed(4), a1_ref[...])
    sg_ref[...] = _sigmoid(_dot(mixed(5), g1_ref[...]))


def _rwkv_feat_body(*refs, vres):
    if vres:
        (k_ref, v_ref, tw_ref, ya_ref, sg_ref, yv_ref, vf_ref, w2_ref, a2_ref, g2_ref, w0_ref, a0_ref, kk_ref_w,
         ones_ref, v2_ref, v0_ref, lw_ref, ad_ref, kk_ref, vo_ref, g_ref) = refs
    else:
        (k_ref, v_ref, tw_ref, ya_ref, sg_ref, w2_ref, a2_ref, g2_ref, w0_ref, a0_ref, kk_ref_w,
         ones_ref, lw_ref, ad_ref, kk_ref, vo_ref, g_ref) = refs
    dec = w0_ref[...] + _dot(tw_ref[...].astype(BF16), w2_ref[...])
    w_log = -_softplus(-dec) - 0.5
    lw_ref[...] = -jnp.exp(w_log)
    ad_ref[...] = _sigmoid(a0_ref[...] + _dot(ya_ref[...].astype(BF16), a2_ref[...]))
    g_ref[...] = _dot(sg_ref[...].astype(BF16), g2_ref[...])
    kkr = k_ref[...] * kk_ref_w[...]
    ss = _head_sum(kkr * kkr, ones_ref[...])
    kk_ref[...] = kkr / jnp.maximum(jnp.sqrt(ss), 1e-12)
    v = v_ref[...]
    if vres:
        v = v + (vf_ref[...] - v) * _sigmoid(v0_ref[...] + _dot(yv_ref[...].astype(BF16), v2_ref[...]))
    vo_ref[...] = v


def _scan_body(lw_ref, ad_ref, kf_ref, kk_ref, v_ref, r_ref, ka_ref, y_ref, s_ref):
    d = pl.program_id(1)
    s = pl.program_id(2)
    C = CHUNK
    D = kf_ref.shape[1]

    @pl.when(s == 0)
    def _():
        s_ref[...] = jnp.zeros_like(s_ref)

    sg = jnp.where(d == 0, 1, -1)
    ri = lax.broadcasted_iota(jnp.int32, (C, C), 0)
    ci = lax.broadcasted_iota(jnp.int32, (C, C), 1)
    dm = (ri - ci) * sg
    xr = ri ^ ci
    eye = (ri == ci).astype(F32)
    r2 = lax.broadcasted_iota(jnp.int32, (C, 2 * C), 0)
    c2 = lax.broadcasted_iota(jnp.int32, (C, 2 * C), 1) & (C - 1)
    dm2 = (r2 - c2) * sg
    strict2 = dm2 > 0
    incl2 = dm2 >= 0

    lw = lw_ref[...]
    ad = ad_ref[...]
    kf = kf_ref[...]
    kk = kk_ref[...]
    r = r_ref[...]
    c = jnp.dot((dm >= 0).astype(F32), lw, precision=HIGHEST, preferred_element_type=F32)
    pc = jnp.where(d == 0, c[C - 1:C, :], c[0:1, :])
    kd = kf * (1.0 + (ad - 1.0) * ka_ref[...])
    b = kk * ad
    einv = jnp.exp(-c)
    a_t = (-kk * jnp.exp(c - lw)).astype(BF16)
    r_t = (r * jnp.exp(c)).astype(BF16)
    b_t = (b * einv).astype(BF16)
    k_t = (kd * einv).astype(BF16)
    epc = jnp.exp(pc - c)
    b_p = (b * epc).astype(BF16)
    k_p = (kd * epc).astype(BF16)
    d_p = jnp.exp(pc)
    vb = v_ref[...].astype(BF16)

    zeros_cc = jnp.zeros((C, C), BF16)

    heads = range(D // C)
    sls = [slice(C * h, C * (h + 1)) for h in heads]
    ar = [jnp.concatenate([a_t[:, sl], r_t[:, sl]], axis=0) for sl in sls]
    bk = [jnp.concatenate([b_t[:, sl], k_t[:, sl]], axis=0) for sl in sls]
    sc = [lax.dot_general(ar[h], bk[h], _NT, preferred_element_type=F32) for h in heads]
    st = [s_ref[h] for h in heads]
    ars = [lax.dot_general(ar[h], st[h].astype(BF16), _NT, preferred_element_type=F32) for h in heads]
    top = [jnp.where(strict2, sc[h][0:C], 0.0) for h in heads]
    bot = [jnp.where(incl2, sc[h][C:], 0.0).astype(BF16) for h in heads]
    lab = [top[h][:, 0:C] for h in heads]
    zv = [jnp.concatenate([zeros_cc, vb[:, sl]], axis=0) for sl in sls]
    x = [ars[h][0:C] + _dot(top[h].astype(BF16), zv[h]) for h in heads]
    t = [eye + jnp.where(xr == 1, lab[h], 0.0) for h in heads]
    for lb in range(1, 6):
        level = (xr >> lb) == 1
        tb = [t[h].astype(BF16) for h in heads]
        ot = [_dot(jnp.where(level, lab[h], 0.0).astype(BF16), tb[h]).astype(BF16) for h in heads]
        t = [t[h] + _dot(tb[h], ot[h]) for h in heads]
    u = [_dot(t[h].astype(BF16), x[h].astype(BF16)) for h in heads]
    uv = [jnp.concatenate([u[h].astype(BF16), vb[:, sls[h]]], axis=0) for h in heads]
    for h in heads:
        y_ref[:, sls[h]] = ars[h][C:] + _dot(bot[h], uv[h])
    for h in heads:
        bkp = jnp.concatenate([b_p[:, sls[h]], k_p[:, sls[h]]], axis=0)
        s_ref[h] = st[h] * d_p[:, sls[h]] + lax.dot_general(uv[h], bkp, _TN, preferred_element_type=F32)


def _scan(lw, ad, kf, kk, v, r, k_a, lay):
    B, T, L, D = lay.B, lay.T, lay.L, lay.D
    nc_ctx = L // CHUNK
    nc_lat = T // CHUNK
    ctx_base = (B * T) // CHUNK

    def chunk_index(b, d, s):
        ctx_i = jnp.where(d == 0, s, nc_ctx - 1 - s)
        lat_i = jnp.where(d == 0, s - nc_ctx, nc_lat - 1 - (s - nc_ctx))
        return jnp.where(s < nc_ctx, ctx_base + b * nc_ctx + ctx_i, b * nc_lat + lat_i)

    per_dir = pl.BlockSpec((CHUNK, D), lambda b, d, s: (chunk_index(b, d, s), d))
    shared = pl.BlockSpec((CHUNK, D), lambda b, d, s: (chunk_index(b, d, s), 0))
    return pl.pallas_call(
        _scan_body, grid=(B, 2, nc_ctx + nc_lat),
        in_specs=[per_dir, per_dir, shared, shared, shared, shared,
                  pl.BlockSpec((1, D), lambda b, d, s: (0, 0))],
        out_specs=per_dir,
        out_shape=jax.ShapeDtypeStruct((lay.n, 2 * D), F32),
        scratch_shapes=[pltpu.VMEM((D // CHUNK, CHUNK, CHUNK), F32)],
        compiler_params=pltpu.CompilerParams(dimension_semantics=("parallel", "parallel", "arbitrary"),
                                             vmem_limit_bytes=VMEM_LIMIT),
        name="wkv_scan",
    )(lw, ad, kf, kk, v, r, k_a)


def _rwkv_post_body(y2_ref, r_ref, kf_ref, ad_ref, v_ref, g_ref, x_ref, mod_ref,
                    ka_ref, rk_ref, lnw_ref, lnb_ref, ones_ref, wo_ref, o_ref):
    D = x_ref.shape[1]
    ones4 = ones_ref[...]
    y2 = y2_ref[...]
    y = y2[:, :D] + y2[:, D:]
    inv_n = 1.0 / RWKV_HEAD
    yc = y - _head_sum(y, ones4) * inv_n
    var = _head_sum(yc * yc, ones4) * inv_n
    yn = yc * lax.rsqrt(var + LN_X_EPS) * lnw_ref[...] + lnb_ref[...]
    kf = kf_ref[...]
    ad = ad_ref[...]
    ka = ka_ref[...]
    k_bonus = 0.5 * (kf * (1.0 + (ad[:, :D] - 1.0) * ka) + kf * (1.0 + (ad[:, D:] - 1.0) * ka))
    bonus = _head_sum(r_ref[...] * k_bonus * rk_ref[...], ones4) * v_ref[...]
    o = ((yn + bonus) * g_ref[...]).astype(BF16)
    o_ref[...] = x_ref[...] + mod_ref[2:3, :] * _dot(o, wo_ref[...])


def _token_shift(h, lay):
    def centred(z):
        zp = jnp.pad(z, ((0, 0), (1, 1), (0, 0)))
        return 0.5 * (zp[:, :-2] + zp[:, 2:]) - z
    B, T, L, D = lay.B, lay.T, lay.L, lay.D
    hl = centred(h[:lay.n_lat].reshape(B, T, D)).reshape(B * T, D)
    hc = centred(h[lay.n_lat:].reshape(B, L, D)).reshape(B * L, D)
    return jnp.concatenate([hl, hc], axis=0)


def _rope_tables(T, tm):
    pos = jnp.arange(T)
    inv_freq = 1.0 / (ROPE_BASE ** (jnp.arange(AXIS_PAIRS, dtype=F32) / AXIS_PAIRS))
    ang_r = (pos // GRID_W).astype(F32)[:, None] * inv_freq[None, :]
    ang_c = (pos % GRID_W).astype(F32)[:, None] * inv_freq[None, :]
    pad1 = jnp.ones((T, LANES - QK_ROPE), F32)
    pad0 = jnp.zeros((T, LANES - QK_ROPE), F32)
    cos = jnp.concatenate([jnp.cos(ang_r), jnp.cos(ang_r), jnp.cos(ang_c), jnp.cos(ang_c), pad1], axis=1)
    sin = jnp.concatenate([-jnp.sin(ang_r), jnp.sin(ang_r), -jnp.sin(ang_c), jnp.sin(ang_c), pad0], axis=1)
    cos = jnp.concatenate([cos, jnp.ones((tm, LANES), F32)], axis=0)
    sin = jnp.concatenate([sin, jnp.zeros((tm, LANES), F32)], axis=0)
    return cos, sin


def _pad_cols(w, width):
    return jnp.pad(w, ((0, 0), (0, width - w.shape[1])))


def _pad_rows(w, height):
    return jnp.pad(w, ((0, height - w.shape[0]), (0, 0)))


def _block_diag2(w0, w1):
    z0 = jnp.zeros_like(w0)
    z1 = jnp.zeros_like(w1)
    return jnp.concatenate([jnp.concatenate([w0, z1], axis=1), jnp.concatenate([z0, w1], axis=1)], axis=0)


def kernel(x, c, ctx, c_ctx, ada_w, ada_b, norm1_g, norm2_g, mla_wqa, mla_qa_norm, mla_wqb, mla_wkva, mla_kva_norm, mla_wkvb, mla_q_norm, mla_k_norm, mla_wo, rwkv_mix, rwkv_wr, rwkv_wk, rwkv_wv, rwkv_wo, rwkv_w0, rwkv_w1, rwkv_w2, rwkv_a0, rwkv_a1, rwkv_a2, rwkv_g1, rwkv_g2, rwkv_k_k, rwkv_k_a, rwkv_r_k, rwkv_ln_w, rwkv_ln_b, rwkv_v0, rwkv_v1, rwkv_v2, ffn_w1, ffn_w3, ffn_w2, moe_router, moe_w1, moe_w3, moe_w2):
    B, T, D = x.shape
    L = ctx.shape[1]
    depth = ada_w.shape[0]
    lay = _Layout(B, T, L, D)
    n = lay.n
    tm = min(512, T)
    tm_small = min(256, T)
    assert T % tm == 0 and (B * L) % tm == 0 and T % GRID_W == 0 and L % CHUNK == 0 and B + 1 <= 16

    xf = jnp.concatenate([x.reshape(B * T, D), ctx.reshape(B * L, D)], axis=0)
    cc = jnp.concatenate([c, c_ctx[None, :], jnp.zeros((16 - B - 1, D), F32)], axis=0)
    modall = _ada_call(cc, ada_w, ada_b)[:, :B + 1].reshape(depth, B + 1, 6, D)
    cos_t, sin_t = _rope_tables(T, tm)
    ones4 = jnp.kron(jnp.eye(4, dtype=F32), jnp.ones((RWKV_HEAD, RWKV_HEAD), F32)).astype(BF16)
    row1 = lambda z: z.reshape(1, -1)

    v_first = None
    for i in range(depth):
        j = i // 2
        modtab = modall[i]
        mod_t = (modtab, lay.mod_spec(tm))
        mod_s = (modtab, lay.mod_spec(tm_small))
        if i % 2 == 0:
            wa = jnp.concatenate([mla_wqa[j], _pad_cols(mla_wkva[j], KV_LORA + LANES)], axis=1).astype(BF16)
            wqb = mla_wqb[j].reshape(Q_LORA, MLA_HEADS, QK_HEAD)
            wqb = jnp.pad(wqb, ((0, 0), (0, 0), (0, QK_PAD - QK_HEAD))).reshape(Q_LORA, MLA_HEADS * QK_PAD).astype(BF16)
            wkvb = mla_wkvb[j].reshape(KV_LORA, MLA_HEADS, QK_NOPE + V_HEAD)
            wk = wkvb[:, :, :QK_NOPE].reshape(KV_LORA, MLA_HEADS * QK_NOPE).astype(BF16)
            wv = wkvb[:, :, QK_NOPE:].reshape(KV_LORA, MLA_HEADS * V_HEAD).astype(BF16)
            gq = _pad_cols(row1(mla_q_norm[j]) * SM_SCALE, QK_PAD)
            gk = _pad_cols(row1(mla_k_norm[j]), QK_PAD)
            cq, ckv, kr = _rowwise(
                _mla_a_body, "mla_a", n, tm, [xf], [mod_t],
                [row1(norm1_g[i]), wa, row1(mla_qa_norm[j]), row1(mla_kva_norm[j])],
                [(Q_LORA, BF16), (KV_LORA, BF16), (LANES, F32)])
            rope_in = [(cos_t, lay.rope_spec(tm)), (sin_t, lay.rope_spec(tm))]
            (q,) = _rowwise(_mla_q_body, "mla_q", n, tm, [cq], rope_in, [wqb, gq], [(MLA_HEADS * QK_PAD, BF16)])
            k, v = _rowwise(_mla_kv_body, "mla_kv", n, tm, [ckv, kr], rope_in, [wk, wv, gk],
                            [(MLA_HEADS * QK_PAD, BF16), (MLA_HEADS * V_PAD, BF16)])
            o = _attention(q, k, v, lay, tq=min(1024, T))
            (xf,) = _rowwise(_proj_resid_body, "mla_out", n, tm, [o, xf], [mod_t], [mla_wo[j].astype(BF16)], [(D, F32)])
        else:
            vres = j > 0
            (h,) = _rowwise(functools.partial(_norm_body, si=0), "norm1", n, tm, [xf], [mod_t],
                            [row1(norm1_g[i])], [(D, F32)])
            xx = _token_shift(h, lay)
            mix = _pad_rows(rwkv_mix[j], 8)
            w1c = jnp.concatenate([rwkv_w1[j, 0], rwkv_w1[j, 1]], axis=1).astype(BF16)
            a1c = jnp.concatenate([rwkv_a1[j, 0], rwkv_a1[j, 1]], axis=1).astype(BF16)
            g1p = _pad_cols(rwkv_g1[j], 2 * LANES).astype(BF16)
            consts = [mix, rwkv_wr[j].astype(BF16), rwkv_wk[j].astype(BF16), rwkv_wv[j].astype(BF16), w1c, a1c, g1p]
            outs = [(D, F32), (D, F32), (D, F32), (LANES, F32), (LANES, F32), (2 * LANES, F32)]
            if vres:
                consts.append(_pad_cols(rwkv_v1[j - 1], LANES).astype(BF16))
                outs.append((LANES, F32))
            res = _rowwise(functools.partial(_rwkv_in_body, vres=vres), "rwkv_in", n, tm_small, [h, xx], [], consts, outs)
            r, kf, v_raw, tw, ya, sgate = res[:6]
            w2c = _block_diag2(rwkv_w2[j, 0], rwkv_w2[j, 1]).astype(BF16)
            a2c = _block_diag2(rwkv_a2[j, 0], rwkv_a2[j, 1]).astype(BF16)
            g2p = _pad_rows(rwkv_g2[j], 2 * LANES).astype(BF16)
            rows = [kf, v_raw, tw, ya, sgate]
            consts = [w2c, a2c, g2p, rwkv_w0[j].reshape(1, 2 * D), rwkv_a0[j].reshape(1, 2 * D), row1(rwkv_k_k[j]), ones4]
            if vres:
                rows += [res[6], v_first]
                consts += [_pad_rows(rwkv_v2[j - 1], LANES).astype(BF16), row1(rwkv_v0[j - 1])]
            lw, ad, kk, v_out, gate = _rowwise(
                functools.partial(_rwkv_feat_body, vres=vres), "rwkv_feat", n, tm_small, rows, [], consts,
                [(2 * D, F32), (2 * D, F32), (D, F32), (D, F32), (D, F32)])
            if j == 0:
                v_first = v_out
            k_a = row1(rwkv_k_a[j])
            y2 = _scan(lw, ad, kf, kk, v_out, r, k_a, lay)
            (xf,) = _rowwise(
                _rwkv_post_body, "rwkv_out", n, tm_small, [y2, r, kf, ad, v_out, gate, xf], [mod_s],
                [k_a, row1(rwkv_r_k[j]), row1(rwkv_ln_w[j]), row1(rwkv_ln_b[j]), ones4, rwkv_wo[j].astype(BF16)],
                [(D, F32)])

        if i % 2 == 0:
            (h,) = _rowwise(functools.partial(_norm_body, si=3), "norm2", n, tm, [xf], [mod_t],
                            [row1(norm2_g[i])], [(D, BF16)])
            xf = _ffn(h, ffn_w1[j].astype(BF16), ffn_w3[j].astype(BF16), ffn_w2[j].astype(BF16),
                      xf, modtab, lay, tm, tf=ffn_w1.shape[2] // 2)
        else:
            h, route = _rowwise(_router_body, "router", n, tm, [xf], [mod_t],
                                [row1(norm2_g[i]), _pad_cols(moe_router[j], LANES)], [(D, F32), (LANES, F32)])
            pos1, pos2, src, tile_expert, n_used = _route_plan(route, n, MOE_TM)
            ys = _gffn(tile_expert, n_used, src, h, moe_w1[j].astype(BF16), moe_w3[j].astype(BF16),
                       moe_w2[j].astype(BF16), MOE_TM, tf=moe_w1.shape[3] // 4)
            xf = _moe_combine(pos1, pos2, ys, route, xf, modtab, lay, tm)
    return xf[:B * T].reshape(B, T, D)
```

```python
import functools

import jax
import jax.numpy as jnp
from jax import lax
from jax.experimental import pallas as pl
from jax.experimental.pallas import tpu as pltpu

F32 = jnp.float32
BF16 = jnp.bfloat16
HIGHEST = lax.Precision.HIGHEST

GRID_W = 64
NORM_EPS = 1e-6
MLA_HEADS = 8
QK_NOPE = 128
QK_ROPE = 64
V_HEAD = 128
QK_HEAD = QK_NOPE + QK_ROPE
Q_LORA = 384
KV_LORA = 256
ROPE_BASE = 10000.0
AXIS_PAIRS = QK_ROPE // 4
SM_SCALE = QK_HEAD ** -0.5
RWKV_HEAD = 64
LN_X_EPS = 64e-5
N_EXPERTS = 8

LANES = 128
QK_PAD = 2 * LANES
V_PAD = 2 * LANES
CHUNK = 64
MOE_TM = 512
IDX_ROW = 1024
ATTN_SUB = 256
DMA_UNROLL = 8
HALO = 8
VMEM_LIMIT = 56 * 1024 * 1024

_NT = (((1,), (1,)), ((), ()))
_TN = (((0,), (0,)), ((), ()))


def _dot(a, b):
    return jnp.dot(a, b, preferred_element_type=F32)


def _sigmoid(z):
    return 1.0 / (1.0 + jnp.exp(-z))


def _softplus(z):
    return jnp.maximum(z, 0.0) + jnp.log(1.0 + jnp.exp(-jnp.abs(z)))


def _rms(z, width):
    return lax.rsqrt(jnp.sum(z * z, axis=-1, keepdims=True) * (1.0 / width) + NORM_EPS)


def _norm_mod(x, g, shift, scale):
    return (x * _rms(x, x.shape[-1])) * g * (1.0 + scale) + shift


def _rope(z, cos, sin):
    lane = lax.broadcasted_iota(jnp.int32, z.shape, 1)
    partner = jnp.where((lane & 16) == 0, pltpu.roll(z, LANES - 16, axis=1), pltpu.roll(z, 16, axis=1))
    return z * cos + partner * sin


def _head_sum(z, ones4):
    outs = []
    for q in range(z.shape[1] // 256):
        blk = z[:, 256 * q:256 * (q + 1)]
        hi = blk.astype(BF16)
        lo = (blk - hi.astype(F32)).astype(BF16)
        outs.append(_dot(hi, ones4) + _dot(lo, ones4))
    return jnp.concatenate(outs, axis=1)


class _Layout:
    def __init__(self, B, T, L, D):
        self.B, self.T, self.L, self.D = B, T, L, D
        self.n_lat = B * T
        self.n = B * T + B * L

    def mod_spec(self, tm):
        n_lat_tiles = self.n_lat // tm
        T, B, D = self.T, self.B, self.D
        return pl.BlockSpec((None, 6, D), lambda i: (jnp.where(i < n_lat_tiles, (i * tm) // T, B), 0, 0))

    def rope_spec(self, tm):
        n_lat_tiles = self.n_lat // tm
        tpb = self.T // tm
        return pl.BlockSpec((tm, LANES), lambda i: (jnp.where(i < n_lat_tiles, i % tpb, tpb), 0))


def _const_spec(arr):
    nd = arr.ndim
    return pl.BlockSpec(arr.shape, lambda i: (0,) * nd, pipeline_mode=pl.Buffered(1))


def _rowwise(body, name, n_rows, tm, row_ins, tile_ins, const_ins, outs):
    in_specs = [pl.BlockSpec((tm, a.shape[1]), lambda i: (i, 0)) for a in row_ins]
    in_specs += [spec for (_, spec) in tile_ins]
    in_specs += [_const_spec(a) for a in const_ins]
    out_specs = [pl.BlockSpec((tm, w), lambda i: (i, 0)) for (w, _) in outs]
    out_shape = [jax.ShapeDtypeStruct((n_rows, w), dt) for (w, dt) in outs]
    return pl.pallas_call(
        body, grid=(n_rows // tm,), in_specs=in_specs, out_specs=out_specs, out_shape=out_shape,
        compiler_params=pltpu.CompilerParams(dimension_semantics=("parallel",), vmem_limit_bytes=VMEM_LIMIT),
        name=name,
    )(*row_ins, *[a for (a, _) in tile_ins], *const_ins)


def _ada_body(c_ref, w_ref, b_ref, o_ref):
    c = c_ref[...]
    o_ref[...] = jnp.dot(c * _sigmoid(c), w_ref[...], precision=HIGHEST, preferred_element_type=F32) + b_ref[...]


def _ada_call(cc, ada_w, ada_b):
    depth, D, W = ada_w.shape
    tn = 1024
    return pl.pallas_call(
        _ada_body, grid=(depth, W // tn),
        in_specs=[pl.BlockSpec(cc.shape, lambda l, j: (0, 0)),
                  pl.BlockSpec((None, D, tn), lambda l, j: (l, 0, j)),
                  pl.BlockSpec((None, 1, tn), lambda l, j: (l, 0, j))],
        out_specs=pl.BlockSpec((None, cc.shape[0], tn), lambda l, j: (l, 0, j)),
        out_shape=jax.ShapeDtypeStruct((depth, cc.shape[0], W), F32),
        compiler_params=pltpu.CompilerParams(dimension_semantics=("parallel", "parallel"), vmem_limit_bytes=VMEM_LIMIT),
        name="ada",
    )(cc, ada_w, ada_b.reshape(depth, 1, W))


def _norm_body(x_ref, mod_ref, g_ref, o_ref, *, si):
    h = _norm_mod(x_ref[...], g_ref[...], mod_ref[si:si + 1, :], mod_ref[si + 1:si + 2, :])
    o_ref[...] = h.astype(o_ref.dtype)


def _mla_a_body(x_ref, mod_ref, g_ref, wa_ref, qan_ref, kvn_ref, cq_ref, ckv_ref, kr_ref):
    h = _norm_mod(x_ref[...], g_ref[...], mod_ref[0:1, :], mod_ref[1:2, :]).astype(BF16)
    acc = _dot(h, wa_ref[...])
    qa = acc[:, 0:Q_LORA]
    kv = acc[:, Q_LORA:Q_LORA + KV_LORA]
    cq_ref[...] = (qa * _rms(qa, Q_LORA) * qan_ref[...]).astype(BF16)
    ckv_ref[...] = (kv * _rms(kv, KV_LORA) * kvn_ref[...]).astype(BF16)
    kr_ref[...] = acc[:, Q_LORA + KV_LORA:]


def _mla_q_body(cq_ref, cos_ref, sin_ref, wqb_ref, gq_ref, q_ref):
    acc = _dot(cq_ref[...], wqb_ref[...])
    cos = cos_ref[...]
    sin = sin_ref[...]
    g = gq_ref[...]
    for h in range(MLA_HEADS):
        a = acc[:, QK_PAD * h:QK_PAD * (h + 1)]
        an = a * _rms(a, QK_HEAD) * g
        q_ref[:, QK_PAD * h:QK_PAD * h + LANES] = an[:, :LANES].astype(BF16)
        q_ref[:, QK_PAD * h + LANES:QK_PAD * (h + 1)] = _rope(an[:, LANES:], cos, sin).astype(BF16)


def _mla_kv_body(ckv_ref, kr_ref, cos_ref, sin_ref, wk_ref, wv_ref, gk_ref, k_ref, v_ref):
    ckv = ckv_ref[...]
    kn = _dot(ckv, wk_ref[...])
    vals = _dot(ckv, wv_ref[...]).astype(BF16)
    lane = lax.broadcasted_iota(jnp.int32, (vals.shape[0], LANES), 1)
    ones_col = jnp.where(lane == 0, 1.0, 0.0).astype(BF16)
    for h in range(MLA_HEADS):
        v_ref[:, V_PAD * h:V_PAD * h + V_HEAD] = vals[:, V_HEAD * h:V_HEAD * (h + 1)]
        v_ref[:, V_PAD * h + V_HEAD:V_PAD * (h + 1)] = ones_col
    kr = kr_ref[...]
    ss_rope = jnp.sum(kr * kr, axis=-1, keepdims=True)
    g = gk_ref[...]
    g_nope = g[:, :LANES]
    kr_rot = _rope(kr * g[:, LANES:], cos_ref[...], sin_ref[...])
    for h in range(MLA_HEADS):
        a = kn[:, LANES * h:LANES * (h + 1)]
        inv = lax.rsqrt((jnp.sum(a * a, axis=-1, keepdims=True) + ss_rope) * (1.0 / QK_HEAD) + NORM_EPS)
        k_ref[:, QK_PAD * h:QK_PAD * h + LANES] = (a * inv * g_nope).astype(BF16)
        k_ref[:, QK_PAD * h + LANES:QK_PAD * (h + 1)] = (kr_rot * inv).astype(BF16)


def _attn_body(*refs, nseg, sub):
    q_ref = refs[0]
    o_ref = refs[-1]
    n_sub = q_ref.shape[0] // sub

    def scores(j):
        q = q_ref[sub * j:sub * (j + 1), :]
        return [lax.dot_general(q, refs[1 + 2 * i][...], _NT, preferred_element_type=F32) for i in range(nseg)]

    def finish(j, ss):
        m = functools.reduce(jnp.maximum, [jnp.max(s, axis=-1, keepdims=True) for s in ss])
        acc = functools.reduce(lambda a, b: a + b,
                               [_dot(jnp.exp((ss[i] - m).astype(BF16)), refs[2 + 2 * i][...]) for i in range(nseg)])
        o_ref[sub * j:sub * (j + 1), :] = (acc[:, :V_HEAD] / acc[:, V_HEAD:V_HEAD + 1]).astype(o_ref.dtype)

    pending = scores(0)
    for j in range(n_sub):
        following = scores(j + 1) if j + 1 < n_sub else None
        finish(j, pending)
        pending = following


def _attention(q, k, v, lay, tq):
    B, T, L = lay.B, lay.T, lay.L
    nq = T // tq
    ctx0 = (B * T) // L
    params = pltpu.CompilerParams(dimension_semantics=("parallel", "parallel", "arbitrary"),
                                  vmem_limit_bytes=VMEM_LIMIT)
    o_lat = pl.pallas_call(
        functools.partial(_attn_body, nseg=2, sub=min(ATTN_SUB, tq)), grid=(B, MLA_HEADS, nq),
        in_specs=[pl.BlockSpec((tq, QK_PAD), lambda b, h, i: (b * nq + i, h)),
                  pl.BlockSpec((L, QK_PAD), lambda b, h, i: (ctx0 + b, h)),
                  pl.BlockSpec((L, V_PAD), lambda b, h, i: (ctx0 + b, h)),
                  pl.BlockSpec((T, QK_PAD), lambda b, h, i: (b, h)),
                  pl.BlockSpec((T, V_PAD), lambda b, h, i: (b, h))],
        out_specs=pl.BlockSpec((tq, V_HEAD), lambda b, h, i: (b * nq + i, h)),
        out_shape=jax.ShapeDtypeStruct((B * T, MLA_HEADS * V_HEAD), BF16),
        compiler_params=params, name="attn_latent",
    )(q, k, v, k, v)
    o_ctx = pl.pallas_call(
        functools.partial(_attn_body, nseg=1, sub=min(ATTN_SUB, L)), grid=(B, MLA_HEADS, 1),
        in_specs=[pl.BlockSpec((L, QK_PAD), lambda b, h, i: (ctx0 + b, h)),
                  pl.BlockSpec((L, QK_PAD), lambda b, h, i: (ctx0 + b, h)),
                  pl.BlockSpec((L, V_PAD), lambda b, h, i: (ctx0 + b, h))],
        out_specs=pl.BlockSpec((L, V_HEAD), lambda b, h, i: (b, h)),
        out_shape=jax.ShapeDtypeStruct((B * L, MLA_HEADS * V_HEAD), BF16),
        compiler_params=params, name="attn_ctx",
    )(q, k, v)
    return jnp.concatenate([o_lat, o_ctx], axis=0)


def _proj_resid_body(a_ref, x_ref, mod_ref, w_ref, o_ref):
    o_ref[...] = x_ref[...] + mod_ref[2:3, :] * _dot(a_ref[...], w_ref[...])


def _router_body(x_ref, mod_ref, g_ref, router_ref, h_ref, route_ref):
    hf = _norm_mod(x_ref[...], g_ref[...], mod_ref[3:4, :], mod_ref[4:5, :])
    h_ref[...] = hf
    logits = jnp.dot(hf, router_ref[...], precision=HIGHEST, preferred_element_type=F32)
    lane = lax.broadcasted_iota(jnp.int32, logits.shape, 1)
    lanef = lane.astype(F32)
    neg = jnp.float32(-1e30)
    lg = jnp.where(lane < N_EXPERTS, logits, neg)
    m1 = jnp.max(lg, axis=-1, keepdims=True)
    i1 = jnp.min(jnp.where(lg == m1, lanef, float(LANES)), axis=-1, keepdims=True)
    lg2 = jnp.where(lanef == i1, neg, lg)
    m2 = jnp.max(lg2, axis=-1, keepdims=True)
    i2 = jnp.min(jnp.where(lg2 == m2, lanef, float(LANES)), axis=-1, keepdims=True)
    e2 = jnp.exp(m2 - m1)
    w1 = 1.0 / (1.0 + e2)
    w2 = e2 / (1.0 + e2)
    route_ref[...] = jnp.where(lane == 0, i1, jnp.where(lane == 1, i2, jnp.where(lane == 2, w1, jnp.where(lane == 3, w2, 0.0))))


def _route_plan(route, n, tm_g):
    i1 = route[:, 0].astype(jnp.int32)
    i2 = route[:, 1].astype(jnp.int32)
    experts = jnp.arange(N_EXPERTS, dtype=jnp.int32)
    onehot = ((i1[:, None] == experts) | (i2[:, None] == experts)).astype(jnp.int32)
    rank = jnp.cumsum(onehot, axis=0) - onehot
    tiles_e = (jnp.sum(onehot, axis=0) + tm_g - 1) // tm_g
    tile_end = jnp.cumsum(tiles_e)
    start = (tile_end - tiles_e) * tm_g
    pos1 = start[i1] + jnp.take_along_axis(rank, i1[:, None], axis=1)[:, 0]
    pos2 = start[i2] + jnp.take_along_axis(rank, i2[:, None], axis=1)[:, 0]
    n_tiles = -(-2 * n // tm_g) + N_EXPERTS
    tile_ids = jnp.arange(n_tiles, dtype=jnp.int32)
    tile_expert = jnp.minimum(jnp.sum((tile_ids[:, None] >= tile_end[None, :]).astype(jnp.int32), axis=1), N_EXPERTS - 1)
    tok = jnp.arange(n, dtype=jnp.int32)
    src = jnp.zeros((n_tiles * tm_g,), jnp.int32).at[jnp.concatenate([pos1, pos2])].set(
        jnp.concatenate([tok, tok]), unique_indices=True, mode="promise_in_bounds")
    src = _pad_cols(src.reshape(n_tiles, tm_g), IDX_ROW)
    return pos1, pos2, src, tile_expert.astype(jnp.int32), tile_end[-1:].astype(jnp.int32)


def _row_copy(src, src_row, dst, dst_row, sem):
    return pltpu.make_async_copy(src.at[pl.ds(src_row, 1), :], dst.at[pl.ds(dst_row, 1), :], sem)


def _gffn_body(te_ref, nu_ref, src_hbm, h_hbm, w1_ref, w3_ref, w2_ref, ys_ref,
               idx, xg, hb_ref, acc_ref, isem, dsem, *, tm, nf):
    del te_ref
    m = pl.program_id(0)
    f = pl.program_id(1)
    n_m = pl.num_programs(0)
    slot = m & 1
    other = 1 - slot
    n_used = nu_ref[0]
    used = m < n_used
    per_f = tm // nf

    def fetch_idx(tile, s):
        fetch = pltpu.make_async_copy(src_hbm.at[tile], idx.at[s], isem)
        fetch.start()
        fetch.wait()

    def wait_slot(s):
        pltpu.make_async_copy(xg.at[s], xg.at[s], dsem.at[s]).wait()

    @pl.when((m == 0) & (f == 0))
    def _():
        fetch_idx(0, 0)

        def issue(t, carry):
            _row_copy(h_hbm, idx[0, t], xg.at[0], t, dsem.at[0]).start()
            return carry

        lax.fori_loop(0, tm, issue, 0, unroll=DMA_UNROLL)

    @pl.when(f == 0)
    def _():
        acc_ref[...] = jnp.zeros_like(acc_ref)

    @pl.when((f == 0) & (m <= n_used))
    def _():
        wait_slot(slot)
        hb_ref[...] = xg[slot].astype(BF16)

    @pl.when((f == 0) & used)
    def _():
        fetch_idx(jnp.minimum(m + 1, n_m - 1), other)

    @pl.when(used)
    def _():
        for t in range(per_f):
            row = f * per_f + t
            _row_copy(h_hbm, idx[other, row], xg.at[other], row, dsem.at[other]).start()
        h = hb_ref[...]
        a1 = _dot(h, w1_ref[...])
        a3 = _dot(h, w3_ref[...])
        acc_ref[...] += _dot((a1 * _sigmoid(a1) * a3).astype(BF16), w2_ref[...])

    @pl.when(f == nf - 1)
    def _():
        ys_ref[...] = acc_ref[...]

    @pl.when((f == nf - 1) & used & (m == n_m - 1))
    def _():
        wait_slot(other)


def _gffn(tile_expert, n_used, src, h, w1, w3, w2, layer, tm_g, tf):
    _, E, D, F = w1.shape
    n_tiles = src.shape[0]
    nf = F // tf
    fidx = lambda m, f, nu: jnp.where(m < nu[0], f, nf - 1)
    any_spec = pl.BlockSpec(memory_space=pl.ANY)
    grid_spec = pltpu.PrefetchScalarGridSpec(
        num_scalar_prefetch=2, grid=(n_tiles, nf),
        in_specs=[any_spec, any_spec,
                  pl.BlockSpec((None, None, D, tf), lambda m, f, te, nu: (layer, te[m], 0, fidx(m, f, nu))),
                  pl.BlockSpec((None, None, D, tf), lambda m, f, te, nu: (layer, te[m], 0, fidx(m, f, nu))),
                  pl.BlockSpec((None, None, tf, D), lambda m, f, te, nu: (layer, te[m], fidx(m, f, nu), 0))],
        out_specs=pl.BlockSpec((tm_g, D), lambda m, f, te, nu: (m, 0)),
        scratch_shapes=[pltpu.SMEM((2, IDX_ROW), jnp.int32), pltpu.VMEM((2, tm_g, D), F32),
                        pltpu.VMEM((tm_g, D), BF16), pltpu.VMEM((tm_g, D), F32),
                        pltpu.SemaphoreType.DMA, pltpu.SemaphoreType.DMA((2,))])
    return pl.pallas_call(
        functools.partial(_gffn_body, tm=tm_g, nf=nf), grid_spec=grid_spec,
        out_shape=jax.ShapeDtypeStruct((n_tiles * tm_g, D), F32),
        compiler_params=pltpu.CompilerParams(dimension_semantics=("arbitrary", "arbitrary"),
                                             vmem_limit_bytes=VMEM_LIMIT),
        name="moe_gffn",
    )(tile_expert, n_used, src, h, w1, w3, w2)


def _moe_combine_body(pos_hbm, ys_hbm, route_ref, x_ref, mod_ref, o_ref, idx, yg, isem, dsem, *, tm):
    i = pl.program_id(0)
    slot = i & 1

    def start_gather(tile, s):
        fetch = pltpu.make_async_copy(pos_hbm.at[tile], idx.at[s], isem)
        fetch.start()
        fetch.wait()

        def issue(t, carry):
            _row_copy(ys_hbm, idx[s, t], yg.at[s, 0], t, dsem.at[s]).start()
            _row_copy(ys_hbm, idx[s, tm + t], yg.at[s, 1], t, dsem.at[s]).start()
            return carry

        lax.fori_loop(0, tm, issue, 0, unroll=DMA_UNROLL)

    @pl.when(i == 0)
    def _():
        start_gather(0, 0)

    pltpu.make_async_copy(yg.at[slot], yg.at[slot], dsem.at[slot]).wait()

    @pl.when(i + 1 < pl.num_programs(0))
    def _():
        start_gather(i + 1, 1 - slot)

    route = route_ref[...]
    y = route[:, 2:3] * yg[slot, 0] + route[:, 3:4] * yg[slot, 1]
    o_ref[...] = x_ref[...] + mod_ref[5:6, :] * y


def _moe_combine(pos1, pos2, ys, route, xf, modtab, lay, tm, n_out):
    n, D = xf.shape
    pos = jnp.concatenate([pos1.reshape(n // tm, tm), pos2.reshape(n // tm, tm)], axis=1)
    any_spec = pl.BlockSpec(memory_space=pl.ANY)
    row = lambda i: (i, 0)
    return pl.pallas_call(
        functools.partial(_moe_combine_body, tm=tm), grid=(n_out // tm,),
        in_specs=[any_spec, any_spec, pl.BlockSpec((tm, LANES), row), pl.BlockSpec((tm, D), row), lay.mod_spec(tm)],
        out_specs=pl.BlockSpec((tm, D), row),
        out_shape=jax.ShapeDtypeStruct((n_out, D), F32),
        scratch_shapes=[pltpu.SMEM((2, 2 * tm), jnp.int32), pltpu.VMEM((2, 2, tm, D), F32),
                        pltpu.SemaphoreType.DMA, pltpu.SemaphoreType.DMA((2,))],
        compiler_params=pltpu.CompilerParams(dimension_semantics=("arbitrary",), vmem_limit_bytes=VMEM_LIMIT),
        name="moe_combine",
    )(pos, ys, route, xf, modtab)


def _ffn_body(h_ref, w1_ref, w3_ref, w2_ref, x_ref, mod_ref, o_ref, acc_ref):
    f = pl.program_id(1)

    @pl.when(f == 0)
    def _():
        acc_ref[...] = jnp.zeros_like(acc_ref)

    h = h_ref[...]
    a1 = _dot(h, w1_ref[...])
    a3 = _dot(h, w3_ref[...])
    acc_ref[...] += _dot((a1 * _sigmoid(a1) * a3).astype(BF16), w2_ref[...])

    @pl.when(f == pl.num_programs(1) - 1)
    def _():
        o_ref[...] = x_ref[...] + mod_ref[5:6, :] * acc_ref[...]


def _ffn(h, w1, w3, w2, layer, xf, modtab, lay, tm, tf):
    _, D, F = w1.shape
    n_lat_tiles = lay.n_lat // tm
    T, B = lay.T, lay.B
    row = lambda m, f: (m, 0)
    return pl.pallas_call(
        _ffn_body, grid=(lay.n // tm, F // tf),
        in_specs=[pl.BlockSpec((tm, D), row),
                  pl.BlockSpec((None, D, tf), lambda m, f: (layer, 0, f)),
                  pl.BlockSpec((None, D, tf), lambda m, f: (layer, 0, f)),
                  pl.BlockSpec((None, tf, D), lambda m, f: (layer, f, 0)),
                  pl.BlockSpec((tm, D), row),
                  pl.BlockSpec((None, 6, D), lambda m, f: (jnp.where(m < n_lat_tiles, (m * tm) // T, B), 0, 0))],
        out_specs=pl.BlockSpec((tm, D), row),
        out_shape=jax.ShapeDtypeStruct((lay.n, D), F32),
        scratch_shapes=[pltpu.VMEM((tm, D), F32)],
        compiler_params=pltpu.CompilerParams(dimension_semantics=("parallel", "arbitrary"),
                                             vmem_limit_bytes=VMEM_LIMIT),
        name="dense_ffn",
    )(h, w1, w3, w2, xf, modtab)


def _rwkv_in_body(*refs, vres, lay):
    if vres:
        (x_ref, xp_ref, xn_ref, mod_ref, g_ref, mix_ref, wr_ref, wk_ref, wv_ref, w1_ref, a1_ref, g1_ref, v1_ref,
         r_ref, k_ref, v_ref, tw_ref, ya_ref, sg_ref, yv_ref) = refs
    else:
        (x_ref, xp_ref, xn_ref, mod_ref, g_ref, mix_ref, wr_ref, wk_ref, wv_ref, w1_ref, a1_ref, g1_ref,
         r_ref, k_ref, v_ref, tw_ref, ya_ref, sg_ref) = refs
    tm = x_ref.shape[0]
    g = g_ref[...]
    shift = mod_ref[0:1, :]
    scale = mod_ref[1:2, :]
    h = _norm_mod(x_ref[...], g, shift, scale)
    h_before = _norm_mod(xp_ref[...], g, shift, scale)[HALO - 1:HALO, :]
    h_after = _norm_mod(xn_ref[...], g, shift, scale)[0:1, :]
    row = lax.broadcasted_iota(jnp.int32, (tm, 1), 0)
    gidx = pl.program_id(0) * tm + row
    in_lat = gidx < lay.n_lat
    pos = jnp.where(in_lat, gidx & (lay.T - 1), (gidx - lay.n_lat) & (lay.L - 1))
    last_pos = jnp.where(in_lat, lay.T - 1, lay.L - 1)
    prev = jnp.where(row == 0, h_before, pltpu.roll(h, 1, axis=0))
    prev = jnp.where(pos == 0, 0.0, prev)
    nxt = jnp.where(row == tm - 1, h_after, pltpu.roll(h, tm - 1, axis=0))
    nxt = jnp.where(pos == last_pos, 0.0, nxt)
    xx = 0.5 * (prev + nxt) - h

    def mixed(j):
        return (h + xx * mix_ref[j:j + 1, :]).astype(BF16)

    r_ref[...] = _dot(mixed(0), wr_ref[...])
    tw_ref[...] = jnp.tanh(_dot(mixed(1), w1_ref[...]))
    k_ref[...] = _dot(mixed(2), wk_ref[...])
    xv = mixed(3)
    v_ref[...] = _dot(xv, wv_ref[...])
    if vres:
        yv_ref[...] = _dot(xv, v1_ref[...])
    ya_ref[...] = _dot(mixed(4), a1_ref[...])
    sg_ref[...] = _sigmoid(_dot(mixed(5), g1_ref[...]))


def _rwkv_feat_body(*refs, vres):
    if vres:
        (k_ref, v_ref, tw_ref, ya_ref, sg_ref, yv_ref, vf_ref, w2_ref, a2_ref, g2_ref, w0_ref, a0_ref, kk_ref_w,
         ones_ref, v2_ref, v0_ref, lw_ref, ad_ref, kk_ref, vo_ref, g_ref) = refs
    else:
        (k_ref, v_ref, tw_ref, ya_ref, sg_ref, w2_ref, a2_ref, g2_ref, w0_ref, a0_ref, kk_ref_w,
         ones_ref, lw_ref, ad_ref, kk_ref, vo_ref, g_ref) = refs
    dec = w0_ref[...] + _dot(tw_ref[...].astype(BF16), w2_ref[...])
    w_log = -_softplus(-dec) - 0.5
    lw_ref[...] = -jnp.exp(w_log)
    ad_ref[...] = _sigmoid(a0_ref[...] + _dot(ya_ref[...].astype(BF16), a2_ref[...]))
    g_ref[...] = _dot(sg_ref[...].astype(BF16), g2_ref[...])
    kkr = k_ref[...] * kk_ref_w[...]
    ss = _head_sum(kkr * kkr, ones_ref[...])
    kk_ref[...] = kkr / jnp.maximum(jnp.sqrt(ss), 1e-12)
    v = v_ref[...]
    if vres:
        v = v + (vf_ref[...] - v) * _sigmoid(v0_ref[...] + _dot(yv_ref[...].astype(BF16), v2_ref[...]))
    vo_ref[...] = v


def _scan_body(lw_ref, ad_ref, kf_ref, kk_ref, v_ref, r_ref, ka_ref, y_ref, s_ref):
    d = pl.program_id(1)
    s = pl.program_id(2)
    C = CHUNK
    D = kf_ref.shape[1]

    @pl.when(s == 0)
    def _():
        s_ref[...] = jnp.zeros_like(s_ref)

    sg = jnp.where(d == 0, 1, -1)
    ri = lax.broadcasted_iota(jnp.int32, (C, C), 0)
    ci = lax.broadcasted_iota(jnp.int32, (C, C), 1)
    dm = (ri - ci) * sg
    xr = ri ^ ci
    eye = (ri == ci).astype(F32)
    r2 = lax.broadcasted_iota(jnp.int32, (C, 2 * C), 0)
    c2 = lax.broadcasted_iota(jnp.int32, (C, 2 * C), 1) & (C - 1)
    dm2 = (r2 - c2) * sg
    strict2 = dm2 > 0
    incl2 = dm2 >= 0

    lw = lw_ref[...]
    ad = ad_ref[...]
    kf = kf_ref[...]
    kk = kk_ref[...]
    r = r_ref[...]
    c = jnp.dot((dm >= 0).astype(F32), lw, precision=HIGHEST, preferred_element_type=F32)
    pc = jnp.where(d == 0, c[C - 1:C, :], c[0:1, :])
    kd = kf * (1.0 + (ad - 1.0) * ka_ref[...])
    b = kk * ad
    einv = jnp.exp(-c)
    a_t = (-kk * jnp.exp(c - lw)).astype(BF16)
    r_t = (r * jnp.exp(c)).astype(BF16)
    b_t = (b * einv).astype(BF16)
    k_t = (kd * einv).astype(BF16)
    epc = jnp.exp(pc - c)
    b_p = (b * epc).astype(BF16)
    k_p = (kd * epc).astype(BF16)
    d_p = jnp.exp(pc)
    vb = v_ref[...].astype(BF16)

    zeros_cc = jnp.zeros((C, C), BF16)

    heads = range(D // C)
    sls = [slice(C * h, C * (h + 1)) for h in heads]
    ar = [jnp.concatenate([a_t[:, sl], r_t[:, sl]], axis=0) for sl in sls]
    bk = [jnp.concatenate([b_t[:, sl], k_t[:, sl]], axis=0) for sl in sls]
    sc = [lax.dot_general(ar[h], bk[h], _NT, preferred_element_type=F32) for h in heads]
    st = [s_ref[h] for h in heads]
    ars = [lax.dot_general(ar[h], st[h].astype(BF16), _NT, preferred_element_type=F32) for h in heads]
    top = [jnp.where(strict2, sc[h][0:C], 0.0) for h in heads]
    bot = [jnp.where(incl2, sc[h][C:], 0.0).astype(BF16) for h in heads]
    lab = [top[h][:, 0:C] for h in heads]
    zv = [jnp.concatenate([zeros_cc, vb[:, sl]], axis=0) for sl in sls]
    x = [ars[h][0:C] + _dot(top[h].astype(BF16), zv[h]) for h in heads]
    t = [eye + jnp.where(xr == 1, lab[h], 0.0) for h in heads]
    for lb in range(1, 6):
        level = (xr >> lb) == 1
        tb = [t[h].astype(BF16) for h in heads]
        ot = [_dot(jnp.where(level, lab[h], 0.0).astype(BF16), tb[h]).astype(BF16) for h in heads]
        t = [t[h] + _dot(tb[h], ot[h]) for h in heads]
    u = [_dot(t[h].astype(BF16), x[h].astype(BF16)) for h in heads]
    uv = [jnp.concatenate([u[h].astype(BF16), vb[:, sls[h]]], axis=0) for h in heads]
    for h in heads:
        y_ref[:, sls[h]] = ars[h][C:] + _dot(bot[h], uv[h])
    for h in heads:
        bkp = jnp.concatenate([b_p[:, sls[h]], k_p[:, sls[h]]], axis=0)
        s_ref[h] = st[h] * d_p[:, sls[h]] + lax.dot_general(uv[h], bkp, _TN, preferred_element_type=F32)


def _scan(lw, ad, kf, kk, v, r, k_a, lay):
    B, T, L, D = lay.B, lay.T, lay.L, lay.D
    nc_ctx = L // CHUNK
    nc_lat = T // CHUNK
    ctx_base = (B * T) // CHUNK

    def chunk_index(b, d, s):
        ctx_i = jnp.where(d == 0, s, nc_ctx - 1 - s)
        lat_i = jnp.where(d == 0, s - nc_ctx, nc_lat - 1 - (s - nc_ctx))
        return jnp.where(s < nc_ctx, ctx_base + b * nc_ctx + ctx_i, b * nc_lat + lat_i)

    per_dir = pl.BlockSpec((CHUNK, D), lambda b, d, s: (chunk_index(b, d, s), d))
    shared = pl.BlockSpec((CHUNK, D), lambda b, d, s: (chunk_index(b, d, s), 0))
    return pl.pallas_call(
        _scan_body, grid=(B, 2, nc_ctx + nc_lat),
        in_specs=[per_dir, per_dir, shared, shared, shared, shared,
                  pl.BlockSpec((1, D), lambda b, d, s: (0, 0))],
        out_specs=per_dir,
        out_shape=jax.ShapeDtypeStruct((lay.n, 2 * D), F32),
        scratch_shapes=[pltpu.VMEM((D // CHUNK, CHUNK, CHUNK), F32)],
        compiler_params=pltpu.CompilerParams(dimension_semantics=("parallel", "parallel", "arbitrary"),
                                             vmem_limit_bytes=VMEM_LIMIT),
        name="wkv_scan",
    )(lw, ad, kf, kk, v, r, k_a)


def _rwkv_post_body(y2_ref, r_ref, kf_ref, ad_ref, v_ref, g_ref, x_ref, mod_ref,
                    ka_ref, rk_ref, lnw_ref, lnb_ref, ones_ref, wo_ref, o_ref):
    D = x_ref.shape[1]
    ones4 = ones_ref[...]
    y2 = y2_ref[...]
    y = y2[:, :D] + y2[:, D:]
    inv_n = 1.0 / RWKV_HEAD
    yc = y - _head_sum(y, ones4) * inv_n
    var = _head_sum(yc * yc, ones4) * inv_n
    yn = yc * lax.rsqrt(var + LN_X_EPS) * lnw_ref[...] + lnb_ref[...]
    kf = kf_ref[...]
    ad = ad_ref[...]
    ka = ka_ref[...]
    k_bonus = 0.5 * (kf * (1.0 + (ad[:, :D] - 1.0) * ka) + kf * (1.0 + (ad[:, D:] - 1.0) * ka))
    bonus = _head_sum(r_ref[...] * k_bonus * rk_ref[...], ones4) * v_ref[...]
    o = ((yn + bonus) * g_ref[...]).astype(BF16)
    o_ref[...] = x_ref[...] + mod_ref[2:3, :] * _dot(o, wo_ref[...])


def _rope_tables(T, tm):
    pos = jnp.arange(T)
    inv_freq = 1.0 / (ROPE_BASE ** (jnp.arange(AXIS_PAIRS, dtype=F32) / AXIS_PAIRS))
    ang_r = (pos // GRID_W).astype(F32)[:, None] * inv_freq[None, :]
    ang_c = (pos % GRID_W).astype(F32)[:, None] * inv_freq[None, :]
    pad1 = jnp.ones((T, LANES - QK_ROPE), F32)
    pad0 = jnp.zeros((T, LANES - QK_ROPE), F32)
    cos = jnp.concatenate([jnp.cos(ang_r), jnp.cos(ang_r), jnp.cos(ang_c), jnp.cos(ang_c), pad1], axis=1)
    sin = jnp.concatenate([-jnp.sin(ang_r), jnp.sin(ang_r), -jnp.sin(ang_c), jnp.sin(ang_c), pad0], axis=1)
    cos = jnp.concatenate([cos, jnp.ones((tm, LANES), F32)], axis=0)
    sin = jnp.concatenate([sin, jnp.zeros((tm, LANES), F32)], axis=0)
    return cos, sin


def _pad_cols(w, width):
    return jnp.pad(w, ((0, 0), (0, width - w.shape[1])))


def _pad_rows(w, height):
    return jnp.pad(w, ((0, height - w.shape[0]), (0, 0)))


def _block_diag2(w0, w1):
    z0 = jnp.zeros_like(w0)
    z1 = jnp.zeros_like(w1)
    return jnp.concatenate([jnp.concatenate([w0, z1], axis=1), jnp.concatenate([z0, w1], axis=1)], axis=0)


def kernel(x, c, ctx, c_ctx, ada_w, ada_b, norm1_g, norm2_g, mla_wqa, mla_qa_norm, mla_wqb, mla_wkva, mla_kva_norm, mla_wkvb, mla_q_norm, mla_k_norm, mla_wo, rwkv_mix, rwkv_wr, rwkv_wk, rwkv_wv, rwkv_wo, rwkv_w0, rwkv_w1, rwkv_w2, rwkv_a0, rwkv_a1, rwkv_a2, rwkv_g1, rwkv_g2, rwkv_k_k, rwkv_k_a, rwkv_r_k, rwkv_ln_w, rwkv_ln_b, rwkv_v0, rwkv_v1, rwkv_v2, ffn_w1, ffn_w3, ffn_w2, moe_router, moe_w1, moe_w3, moe_w2):
    B, T, D = x.shape
    L = ctx.shape[1]
    depth = ada_w.shape[0]
    lay = _Layout(B, T, L, D)
    n = lay.n
    tm = min(512, T)
    tm_small = min(256, T)
    assert T % tm == 0 and (B * L) % tm == 0 and T % GRID_W == 0 and L % CHUNK == 0 and B + 1 <= 16
    assert T & (T - 1) == 0 and L & (L - 1) == 0 and L % tm_small == 0

    ffn_b = [w.astype(BF16) for w in (ffn_w1, ffn_w3, ffn_w2)]
    moe_b = [w.astype(BF16) for w in (moe_w1, moe_w3, moe_w2)]
    xf = jnp.concatenate([x.reshape(B * T, D), ctx.reshape(B * L, D)], axis=0)
    cc = jnp.concatenate([c, c_ctx[None, :], jnp.zeros((16 - B - 1, D), F32)], axis=0)
    modall = _ada_call(cc, ada_w, ada_b)[:, :B + 1].reshape(depth, B + 1, 6, D)
    cos_t, sin_t = _rope_tables(T, tm)
    ones4 = jnp.kron(jnp.eye(4, dtype=F32), jnp.ones((RWKV_HEAD, RWKV_HEAD), F32)).astype(BF16)
    row1 = lambda z: z.reshape(1, -1)

    v_first = None
    for i in range(depth):
        j = i // 2
        modtab = modall[i]
        mod_t = (modtab, lay.mod_spec(tm))
        mod_s = (modtab, lay.mod_spec(tm_small))
        if i % 2 == 0:
            wa = jnp.concatenate([mla_wqa[j], _pad_cols(mla_wkva[j], KV_LORA + LANES)], axis=1).astype(BF16)
            wqb = mla_wqb[j].reshape(Q_LORA, MLA_HEADS, QK_HEAD)
            wqb = jnp.pad(wqb, ((0, 0), (0, 0), (0, QK_PAD - QK_HEAD))).reshape(Q_LORA, MLA_HEADS * QK_PAD).astype(BF16)
            wkvb = mla_wkvb[j].reshape(KV_LORA, MLA_HEADS, QK_NOPE + V_HEAD)
            wk = wkvb[:, :, :QK_NOPE].reshape(KV_LORA, MLA_HEADS * QK_NOPE).astype(BF16)
            wv = wkvb[:, :, QK_NOPE:].reshape(KV_LORA, MLA_HEADS * V_HEAD).astype(BF16)
            gq = _pad_cols(row1(mla_q_norm[j]) * SM_SCALE, QK_PAD)
            gk = _pad_cols(row1(mla_k_norm[j]), QK_PAD)
            cq, ckv, kr = _rowwise(
                _mla_a_body, "mla_a", n, tm, [xf], [mod_t],
                [row1(norm1_g[i]), wa, row1(mla_qa_norm[j]), row1(mla_kva_norm[j])],
                [(Q_LORA, BF16), (KV_LORA, BF16), (LANES, F32)])
            rope_in = [(cos_t, lay.rope_spec(tm)), (sin_t, lay.rope_spec(tm))]
            (q,) = _rowwise(_mla_q_body, "mla_q", n, tm, [cq], rope_in, [wqb, gq], [(MLA_HEADS * QK_PAD, BF16)])
            k, v = _rowwise(_mla_kv_body, "mla_kv", n, tm, [ckv, kr], rope_in, [wk, wv, gk],
                            [(MLA_HEADS * QK_PAD, BF16), (MLA_HEADS * V_PAD, BF16)])
            o = _attention(q, k, v, lay, tq=min(1024, T))
            (xf,) = _rowwise(_proj_resid_body, "mla_out", n, tm, [o, xf], [mod_t], [mla_wo[j].astype(BF16)], [(D, F32)])
        else:
            vres = j > 0
            mix = _pad_rows(rwkv_mix[j], 8)
            w1c = jnp.concatenate([rwkv_w1[j, 0], rwkv_w1[j, 1]], axis=1).astype(BF16)
            a1c = jnp.concatenate([rwkv_a1[j, 0], rwkv_a1[j, 1]], axis=1).astype(BF16)
            g1p = _pad_cols(rwkv_g1[j], 2 * LANES).astype(BF16)
            consts = [row1(norm1_g[i]), mix, rwkv_wr[j].astype(BF16), rwkv_wk[j].astype(BF16),
                      rwkv_wv[j].astype(BF16), w1c, a1c, g1p]
            outs = [(D, F32), (D, F32), (D, F32), (LANES, F32), (LANES, F32), (2 * LANES, F32)]
            if vres:
                consts.append(_pad_cols(rwkv_v1[j - 1], LANES).astype(BF16))
                outs.append((LANES, F32))
            halo_per_tile = tm_small // HALO
            before = pl.BlockSpec((HALO, D), lambda t: (jnp.maximum(t * halo_per_tile - 1, 0), 0))
            after = pl.BlockSpec((HALO, D), lambda t: (jnp.minimum((t + 1) * halo_per_tile, n // HALO - 1), 0))
            res = _rowwise(functools.partial(_rwkv_in_body, vres=vres, lay=lay), "rwkv_in", n, tm_small,
                           [xf], [(xf, before), (xf, after), mod_s], consts, outs)
            r, kf, v_raw, tw, ya, sgate = res[:6]
            w2c = _block_diag2(rwkv_w2[j, 0], rwkv_w2[j, 1]).astype(BF16)
            a2c = _block_diag2(rwkv_a2[j, 0], rwkv_a2[j, 1]).astype(BF16)
            g2p = _pad_rows(rwkv_g2[j], 2 * LANES).astype(BF16)
            rows = [kf, v_raw, tw, ya, sgate]
            consts = [w2c, a2c, g2p, rwkv_w0[j].reshape(1, 2 * D), rwkv_a0[j].reshape(1, 2 * D), row1(rwkv_k_k[j]), ones4]
            if vres:
                rows += [res[6], v_first]
                consts += [_pad_rows(rwkv_v2[j - 1], LANES).astype(BF16), row1(rwkv_v0[j - 1])]
            lw, ad, kk, v_out, gate = _rowwise(
                functools.partial(_rwkv_feat_body, vres=vres), "rwkv_feat", n, tm_small, rows, [], consts,
                [(2 * D, F32), (2 * D, F32), (D, F32), (D, F32), (D, F32)])
            if j == 0:
                v_first = v_out
            k_a = row1(rwkv_k_a[j])
            y2 = _scan(lw, ad, kf, kk, v_out, r, k_a, lay)
            (xf,) = _rowwise(
                _rwkv_post_body, "rwkv_out", n, tm_small, [y2, r, kf, ad, v_out, gate, xf], [mod_s],
                [k_a, row1(rwkv_r_k[j]), row1(rwkv_ln_w[j]), row1(rwkv_ln_b[j]), ones4, rwkv_wo[j].astype(BF16)],
                [(D, F32)])

        if i % 2 == 0:
            (h,) = _rowwise(functools.partial(_norm_body, si=3), "norm2", n, tm, [xf], [mod_t],
                            [row1(norm2_g[i])], [(D, BF16)])
            xf = _ffn(h, ffn_b[0], ffn_b[1], ffn_b[2], j, xf, modtab, lay, tm, tf=ffn_w1.shape[2] // 2)
        else:
            h, route = _rowwise(_router_body, "router", n, tm, [xf], [mod_t],
                                [row1(norm2_g[i]), _pad_cols(moe_router[j], LANES)], [(D, F32), (LANES, F32)])
            pos1, pos2, src, tile_expert, n_used = _route_plan(route, n, MOE_TM)
            ys = _gffn(tile_expert, n_used, src, h, moe_b[0], moe_b[1], moe_b[2], j, MOE_TM, tf=moe_w1.shape[3] // 4)
            n_out = lay.n_lat if i == depth - 1 else n
            xf = _moe_combine(pos1, pos2, ys, route, xf, modtab, lay, tm, n_out)
    return xf[:B * T].reshape(B, T, D)
```

```python
import functools

import jax
import jax.numpy as jnp
from jax import lax
from jax.experimental import pallas as pl
from jax.experimental.pallas import tpu as pltpu

F32 = jnp.float32
BF16 = jnp.bfloat16
HIGHEST = lax.Precision.HIGHEST

GRID_W = 64
NORM_EPS = 1e-6
MLA_HEADS = 8
QK_NOPE = 128
QK_ROPE = 64
V_HEAD = 128
QK_HEAD = QK_NOPE + QK_ROPE
Q_LORA = 384
KV_LORA = 256
ROPE_BASE = 10000.0
AXIS_PAIRS = QK_ROPE // 4
SM_SCALE = QK_HEAD ** -0.5
RWKV_HEAD = 64
LN_X_EPS = 64e-5
N_EXPERTS = 8

LANES = 128
QK_PAD = 2 * LANES
V_PAD = 2 * LANES
CHUNK = 64
MOE_TM = 512
IDX_ROW = 1024
ATTN_SUB = 512
ATTN_KV = 8192
DMA_UNROLL = 8
HALO = 8
VMEM_LIMIT = 56 * 1024 * 1024

_NT = (((1,), (1,)), ((), ()))
_TN = (((0,), (0,)), ((), ()))


def _dot(a, b):
    return jnp.dot(a, b, preferred_element_type=F32)


def _sigmoid(z):
    return 1.0 / (1.0 + jnp.exp(-z))


def _softplus(z):
    return jnp.maximum(z, 0.0) + jnp.log(1.0 + jnp.exp(-jnp.abs(z)))


def _rms(z, width):
    return lax.rsqrt(jnp.sum(z * z, axis=-1, keepdims=True) * (1.0 / width) + NORM_EPS)


def _norm_mod(x, g, shift, scale):
    return (x * _rms(x, x.shape[-1])) * g * (1.0 + scale) + shift


def _rope(z, cos, sin):
    lane = lax.broadcasted_iota(jnp.int32, z.shape, 1)
    partner = jnp.where((lane & 16) == 0, pltpu.roll(z, LANES - 16, axis=1), pltpu.roll(z, 16, axis=1))
    return z * cos + partner * sin


def _head_sum(z, ones4):
    outs = []
    for q in range(z.shape[1] // 256):
        blk = z[:, 256 * q:256 * (q + 1)]
        hi = blk.astype(BF16)
        lo = (blk - hi.astype(F32)).astype(BF16)
        outs.append(_dot(hi, ones4) + _dot(lo, ones4))
    return jnp.concatenate(outs, axis=1)


class _Layout:
    def __init__(self, B, T, L, D):
        self.B, self.T, self.L, self.D = B, T, L, D
        self.n_lat = B * T
        self.n = B * T + B * L

    def mod_spec(self, tm):
        n_lat_tiles = self.n_lat // tm
        T, B, D = self.T, self.B, self.D
        return pl.BlockSpec((None, 6, D), lambda i: (jnp.where(i < n_lat_tiles, (i * tm) // T, B), 0, 0))

    def rope_spec(self, tm):
        n_lat_tiles = self.n_lat // tm
        tpb = self.T // tm
        return pl.BlockSpec((tm, LANES), lambda i: (jnp.where(i < n_lat_tiles, i % tpb, tpb), 0))


def _const_spec(arr):
    nd = arr.ndim
    return pl.BlockSpec(arr.shape, lambda i: (0,) * nd, pipeline_mode=pl.Buffered(1))


def _rowwise(body, name, n_rows, tm, row_ins, tile_ins, const_ins, outs):
    in_specs = [pl.BlockSpec((tm, a.shape[1]), lambda i: (i, 0)) for a in row_ins]
    in_specs += [spec for (_, spec) in tile_ins]
    in_specs += [_const_spec(a) for a in const_ins]
    out_specs = [pl.BlockSpec((tm, w), lambda i: (i, 0)) for (w, _) in outs]
    out_shape = [jax.ShapeDtypeStruct((n_rows, w), dt) for (w, dt) in outs]
    return pl.pallas_call(
        body, grid=(n_rows // tm,), in_specs=in_specs, out_specs=out_specs, out_shape=out_shape,
        compiler_params=pltpu.CompilerParams(dimension_semantics=("parallel",), vmem_limit_bytes=VMEM_LIMIT),
        name=name,
    )(*row_ins, *[a for (a, _) in tile_ins], *const_ins)


def _ada_body(c_ref, w_ref, b_ref, o_ref):
    c = c_ref[...]
    o_ref[...] = jnp.dot(c * _sigmoid(c), w_ref[...], precision=HIGHEST, preferred_element_type=F32) + b_ref[...]


def _ada_call(cc, ada_w, ada_b):
    depth, D, W = ada_w.shape
    tn = 1024
    return pl.pallas_call(
        _ada_body, grid=(depth, W // tn),
        in_specs=[pl.BlockSpec(cc.shape, lambda l, j: (0, 0)),
                  pl.BlockSpec((None, D, tn), lambda l, j: (l, 0, j)),
                  pl.BlockSpec((None, 1, tn), lambda l, j: (l, 0, j))],
        out_specs=pl.BlockSpec((None, cc.shape[0], tn), lambda l, j: (l, 0, j)),
        out_shape=jax.ShapeDtypeStruct((depth, cc.shape[0], W), F32),
        compiler_params=pltpu.CompilerParams(dimension_semantics=("parallel", "parallel"), vmem_limit_bytes=VMEM_LIMIT),
        name="ada",
    )(cc, ada_w, ada_b.reshape(depth, 1, W))


def _mla_a_body(x_ref, mod_ref, g_ref, wa_ref, qan_ref, kvn_ref, cq_ref, ckv_ref, kr_ref):
    h = _norm_mod(x_ref[...], g_ref[...], mod_ref[0:1, :], mod_ref[1:2, :]).astype(BF16)
    acc = _dot(h, wa_ref[...])
    qa = acc[:, 0:Q_LORA]
    kv = acc[:, Q_LORA:Q_LORA + KV_LORA]
    cq_ref[...] = (qa * _rms(qa, Q_LORA) * qan_ref[...]).astype(BF16)
    ckv_ref[...] = (kv * _rms(kv, KV_LORA) * kvn_ref[...]).astype(BF16)
    kr_ref[...] = acc[:, Q_LORA + KV_LORA:]


def _mla_q_body(cq_ref, cos_ref, sin_ref, wqb_ref, gq_ref, q_ref):
    acc = _dot(cq_ref[...], wqb_ref[...])
    cos = cos_ref[...]
    sin = sin_ref[...]
    g = gq_ref[...]
    for h in range(MLA_HEADS):
        a = acc[:, QK_PAD * h:QK_PAD * (h + 1)]
        an = a * _rms(a, QK_HEAD) * g
        q_ref[:, QK_PAD * h:QK_PAD * h + LANES] = an[:, :LANES].astype(BF16)
        q_ref[:, QK_PAD * h + LANES:QK_PAD * (h + 1)] = _rope(an[:, LANES:], cos, sin).astype(BF16)


def _mla_kv_body(ckv_ref, kr_ref, cos_ref, sin_ref, wk_ref, wv_ref, gk_ref, k_ref, v_ref):
    ckv = ckv_ref[...]
    kn = _dot(ckv, wk_ref[...])
    vals = _dot(ckv, wv_ref[...]).astype(BF16)
    lane = lax.broadcasted_iota(jnp.int32, (vals.shape[0], LANES), 1)
    ones_col = jnp.where(lane == 0, 1.0, 0.0).astype(BF16)
    for h in range(MLA_HEADS):
        v_ref[:, V_PAD * h:V_PAD * h + V_HEAD] = vals[:, V_HEAD * h:V_HEAD * (h + 1)]
        v_ref[:, V_PAD * h + V_HEAD:V_PAD * (h + 1)] = ones_col
    kr = kr_ref[...]
    ss_rope = jnp.sum(kr * kr, axis=-1, keepdims=True)
    g = gk_ref[...]
    g_nope = g[:, :LANES]
    kr_rot = _rope(kr * g[:, LANES:], cos_ref[...], sin_ref[...])
    for h in range(MLA_HEADS):
        a = kn[:, LANES * h:LANES * (h + 1)]
        inv = lax.rsqrt((jnp.sum(a * a, axis=-1, keepdims=True) + ss_rope) * (1.0 / QK_HEAD) + NORM_EPS)
        k_ref[:, QK_PAD * h:QK_PAD * h + LANES] = (a * inv * g_nope).astype(BF16)
        k_ref[:, QK_PAD * h + LANES:QK_PAD * (h + 1)] = (kr_rot * inv).astype(BF16)


def _attn_body(*refs, nseg, sub):
    q_ref = refs[0]
    o_ref = refs[-1]
    n_sub = q_ref.shape[0] // sub

    chunks = []
    for i in range(nseg):
        rows = refs[1 + 2 * i].shape[0]
        step = min(ATTN_KV, rows)
        chunks += [(i, c0, step) for c0 in range(0, rows, step)]

    def scores(j):
        q = q_ref[sub * j:sub * (j + 1), :]
        ss = [lax.dot_general(q, refs[1 + 2 * i][c0:c0 + w, :], _NT, preferred_element_type=F32) for (i, c0, w) in chunks]
        m = functools.reduce(jnp.maximum, [jnp.max(s, axis=-1, keepdims=True) for s in ss])
        return ss, m

    def finish(j, scored):
        ss, m = scored
        acc = functools.reduce(lambda a, b: a + b,
                               [_dot(jnp.exp((s - m).astype(BF16)), refs[2 + 2 * i][c0:c0 + w, :])
                                for s, (i, c0, w) in zip(ss, chunks)])
        o_ref[sub * j:sub * (j + 1), :] = (acc[:, :V_HEAD] / acc[:, V_HEAD:V_HEAD + 1]).astype(o_ref.dtype)

    pending = scores(0)
    for j in range(n_sub):
        following = scores(j + 1) if j + 1 < n_sub else None
        finish(j, pending)
        pending = following


def _attention(q, k, v, lay, tq):
    B, T, L = lay.B, lay.T, lay.L
    nq = T // tq
    ctx0 = (B * T) // L
    params = pltpu.CompilerParams(dimension_semantics=("parallel", "parallel", "arbitrary"),
                                  vmem_limit_bytes=VMEM_LIMIT)
    o_lat = pl.pallas_call(
        functools.partial(_attn_body, nseg=2, sub=min(ATTN_SUB, tq)), grid=(B, MLA_HEADS, nq),
        in_specs=[pl.BlockSpec((tq, QK_PAD), lambda b, h, i: (b * nq + i, h)),
                  pl.BlockSpec((L, QK_PAD), lambda b, h, i: (ctx0 + b, h)),
                  pl.BlockSpec((L, V_PAD), lambda b, h, i: (ctx0 + b, h)),
                  pl.BlockSpec((T, QK_PAD), lambda b, h, i: (b, h)),
                  pl.BlockSpec((T, V_PAD), lambda b, h, i: (b, h))],
        out_specs=pl.BlockSpec((tq, V_HEAD), lambda b, h, i: (b * nq + i, h)),
        out_shape=jax.ShapeDtypeStruct((B * T, MLA_HEADS * V_HEAD), BF16),
        compiler_params=params, name="attn_latent",
    )(q, k, v, k, v)
    o_ctx = pl.pallas_call(
        functools.partial(_attn_body, nseg=1, sub=min(ATTN_SUB, L)), grid=(B, MLA_HEADS, 1),
        in_specs=[pl.BlockSpec((L, QK_PAD), lambda b, h, i: (ctx0 + b, h)),
                  pl.BlockSpec((L, QK_PAD), lambda b, h, i: (ctx0 + b, h)),
                  pl.BlockSpec((L, V_PAD), lambda b, h, i: (ctx0 + b, h))],
        out_specs=pl.BlockSpec((L, V_HEAD), lambda b, h, i: (b, h)),
        out_shape=jax.ShapeDtypeStruct((B * L, MLA_HEADS * V_HEAD), BF16),
        compiler_params=params, name="attn_ctx",
    )(q, k, v)
    return jnp.concatenate([o_lat, o_ctx], axis=0)


def _proj_resid_body(a_ref, x_ref, mod_ref, w_ref, o_ref):
    o_ref[...] = x_ref[...] + mod_ref[2:3, :] * _dot(a_ref[...], w_ref[...])


def _router_body(x_ref, mod_ref, g_ref, router_ref, h_ref, route_ref):
    hf = _norm_mod(x_ref[...], g_ref[...], mod_ref[3:4, :], mod_ref[4:5, :])
    h_ref[...] = hf
    logits = jnp.dot(hf, router_ref[...], precision=HIGHEST, preferred_element_type=F32)
    lane = lax.broadcasted_iota(jnp.int32, logits.shape, 1)
    lanef = lane.astype(F32)
    neg = jnp.float32(-1e30)
    lg = jnp.where(lane < N_EXPERTS, logits, neg)
    m1 = jnp.max(lg, axis=-1, keepdims=True)
    i1 = jnp.min(jnp.where(lg == m1, lanef, float(LANES)), axis=-1, keepdims=True)
    lg2 = jnp.where(lanef == i1, neg, lg)
    m2 = jnp.max(lg2, axis=-1, keepdims=True)
    i2 = jnp.min(jnp.where(lg2 == m2, lanef, float(LANES)), axis=-1, keepdims=True)
    e2 = jnp.exp(m2 - m1)
    w1 = 1.0 / (1.0 + e2)
    w2 = e2 / (1.0 + e2)
    route_ref[...] = jnp.where(lane == 0, i1, jnp.where(lane == 1, i2, jnp.where(lane == 2, w1, jnp.where(lane == 3, w2, 0.0))))


def _route_plan(route, n, tm_g):
    i1 = route[:, 0].astype(jnp.int32)
    i2 = route[:, 1].astype(jnp.int32)
    experts = jnp.arange(N_EXPERTS, dtype=jnp.int32)
    onehot = ((i1[:, None] == experts) | (i2[:, None] == experts)).astype(jnp.int32)
    rank = jnp.cumsum(onehot, axis=0) - onehot
    tiles_e = (jnp.sum(onehot, axis=0) + tm_g - 1) // tm_g
    tile_end = jnp.cumsum(tiles_e)
    start = (tile_end - tiles_e) * tm_g
    pos1 = start[i1] + jnp.take_along_axis(rank, i1[:, None], axis=1)[:, 0]
    pos2 = start[i2] + jnp.take_along_axis(rank, i2[:, None], axis=1)[:, 0]
    n_tiles = -(-2 * n // tm_g) + N_EXPERTS
    tile_ids = jnp.arange(n_tiles, dtype=jnp.int32)
    tile_expert = jnp.minimum(jnp.sum((tile_ids[:, None] >= tile_end[None, :]).astype(jnp.int32), axis=1), N_EXPERTS - 1)
    tok = jnp.arange(n, dtype=jnp.int32)
    src = jnp.zeros((n_tiles * tm_g,), jnp.int32).at[jnp.concatenate([pos1, pos2])].set(
        jnp.concatenate([tok, tok]), unique_indices=True, mode="promise_in_bounds")
    src = _pad_cols(src.reshape(n_tiles, tm_g), IDX_ROW)
    return pos1, pos2, src, tile_expert.astype(jnp.int32), tile_end[-1:].astype(jnp.int32)


def _row_copy(src, src_row, dst, dst_row, sem):
    return pltpu.make_async_copy(src.at[pl.ds(src_row, 1), :], dst.at[pl.ds(dst_row, 1), :], sem)


def _gffn_body(te_ref, nu_ref, src_hbm, h_hbm, w1_ref, w3_ref, w2_ref, ys_ref,
               idx, xg, hb_ref, acc_ref, isem, dsem, *, tm, nf):
    del te_ref
    m = pl.program_id(0)
    f = pl.program_id(1)
    n_m = pl.num_programs(0)
    slot = m & 1
    other = 1 - slot
    n_used = nu_ref[0]
    used = m < n_used
    per_f = tm // nf

    def fetch_idx(tile, s):
        fetch = pltpu.make_async_copy(src_hbm.at[tile], idx.at[s], isem)
        fetch.start()
        fetch.wait()

    def wait_slot(s):
        pltpu.make_async_copy(xg.at[s], xg.at[s], dsem.at[s]).wait()

    @pl.when((m == 0) & (f == 0))
    def _():
        fetch_idx(0, 0)

        def issue(t, carry):
            _row_copy(h_hbm, idx[0, t], xg.at[0], t, dsem.at[0]).start()
            return carry

        lax.fori_loop(0, tm, issue, 0, unroll=DMA_UNROLL)

    @pl.when(f == 0)
    def _():
        acc_ref[...] = jnp.zeros_like(acc_ref)

    @pl.when((f == 0) & (m <= n_used))
    def _():
        wait_slot(slot)
        hb_ref[...] = xg[slot].astype(BF16)

    @pl.when((f == 0) & used)
    def _():
        fetch_idx(jnp.minimum(m + 1, n_m - 1), other)

    @pl.when(used)
    def _():
        for t in range(per_f):
            row = f * per_f + t
            _row_copy(h_hbm, idx[other, row], xg.at[other], row, dsem.at[other]).start()
        h = hb_ref[...]
        a1 = _dot(h, w1_ref[...])
        a3 = _dot(h, w3_ref[...])
        acc_ref[...] += _dot((a1 * _sigmoid(a1) * a3).astype(BF16), w2_ref[...])

    @pl.when(f == nf - 1)
    def _():
        ys_ref[...] = acc_ref[...]

    @pl.when((f == nf - 1) & used & (m == n_m - 1))
    def _():
        wait_slot(other)


def _gffn(tile_expert, n_used, src, h, w1, w3, w2, layer, tm_g, tf):
    _, E, D, F = w1.shape
    n_tiles = src.shape[0]
    nf = F // tf
    fidx = lambda m, f, nu: jnp.where(m < nu[0], f, nf - 1)
    any_spec = pl.BlockSpec(memory_space=pl.ANY)
    grid_spec = pltpu.PrefetchScalarGridSpec(
        num_scalar_prefetch=2, grid=(n_tiles, nf),
        in_specs=[any_spec, any_spec,
                  pl.BlockSpec((None, None, D, tf), lambda m, f, te, nu: (layer, te[m], 0, fidx(m, f, nu))),
                  pl.BlockSpec((None, None, D, tf), lambda m, f, te, nu: (layer, te[m], 0, fidx(m, f, nu))),
                  pl.BlockSpec((None, None, tf, D), lambda m, f, te, nu: (layer, te[m], fidx(m, f, nu), 0))],
        out_specs=pl.BlockSpec((tm_g, D), lambda m, f, te, nu: (m, 0)),
        scratch_shapes=[pltpu.SMEM((2, IDX_ROW), jnp.int32), pltpu.VMEM((2, tm_g, D), F32),
                        pltpu.VMEM((tm_g, D), BF16), pltpu.VMEM((tm_g, D), F32),
                        pltpu.SemaphoreType.DMA, pltpu.SemaphoreType.DMA((2,))])
    return pl.pallas_call(
        functools.partial(_gffn_body, tm=tm_g, nf=nf), grid_spec=grid_spec,
        out_shape=jax.ShapeDtypeStruct((n_tiles * tm_g, D), F32),
        compiler_params=pltpu.CompilerParams(dimension_semantics=("arbitrary", "arbitrary"),
                                             vmem_limit_bytes=VMEM_LIMIT),
        name="moe_gffn",
    )(tile_expert, n_used, src, h, w1, w3, w2)


def _moe_combine_body(pos_hbm, ys_hbm, route_ref, x_ref, mod_ref, o_ref, idx, yg, isem, dsem, *, tm):
    i = pl.program_id(0)
    slot = i & 1

    def start_gather(tile, s):
        fetch = pltpu.make_async_copy(pos_hbm.at[tile], idx.at[s], isem)
        fetch.start()
        fetch.wait()

        def issue(t, carry):
            _row_copy(ys_hbm, idx[s, t], yg.at[s, 0], t, dsem.at[s]).start()
            _row_copy(ys_hbm, idx[s, tm + t], yg.at[s, 1], t, dsem.at[s]).start()
            return carry

        lax.fori_loop(0, tm, issue, 0, unroll=DMA_UNROLL)

    @pl.when(i == 0)
    def _():
        start_gather(0, 0)

    pltpu.make_async_copy(yg.at[slot], yg.at[slot], dsem.at[slot]).wait()

    @pl.when(i + 1 < pl.num_programs(0))
    def _():
        start_gather(i + 1, 1 - slot)

    route = route_ref[...]
    y = route[:, 2:3] * yg[slot, 0] + route[:, 3:4] * yg[slot, 1]
    o_ref[...] = x_ref[...] + mod_ref[5:6, :] * y


def _moe_combine(pos1, pos2, ys, route, xf, modtab, lay, tm, n_out):
    n, D = xf.shape
    pos = jnp.concatenate([pos1.reshape(n // tm, tm), pos2.reshape(n // tm, tm)], axis=1)
    any_spec = pl.BlockSpec(memory_space=pl.ANY)
    row = lambda i: (i, 0)
    return pl.pallas_call(
        functools.partial(_moe_combine_body, tm=tm), grid=(n_out // tm,),
        in_specs=[any_spec, any_spec, pl.BlockSpec((tm, LANES), row), pl.BlockSpec((tm, D), row), lay.mod_spec(tm)],
        out_specs=pl.BlockSpec((tm, D), row),
        out_shape=jax.ShapeDtypeStruct((n_out, D), F32),
        scratch_shapes=[pltpu.SMEM((2, 2 * tm), jnp.int32), pltpu.VMEM((2, 2, tm, D), F32),
                        pltpu.SemaphoreType.DMA, pltpu.SemaphoreType.DMA((2,))],
        compiler_params=pltpu.CompilerParams(dimension_semantics=("arbitrary",), vmem_limit_bytes=VMEM_LIMIT),
        name="moe_combine",
    )(pos, ys, route, xf, modtab)


def _ffn_body(g_ref, w1_ref, w3_ref, w2_ref, x_ref, mod_ref, o_ref, hb_ref, acc_ref):
    f = pl.program_id(1)

    @pl.when(f == 0)
    def _():
        hb_ref[...] = _norm_mod(x_ref[...], g_ref[...], mod_ref[3:4, :], mod_ref[4:5, :]).astype(BF16)
        acc_ref[...] = jnp.zeros_like(acc_ref)

    h = hb_ref[...]
    a1 = _dot(h, w1_ref[...])
    a3 = _dot(h, w3_ref[...])
    acc_ref[...] += _dot((a1 * _sigmoid(a1) * a3).astype(BF16), w2_ref[...])

    @pl.when(f == pl.num_programs(1) - 1)
    def _():
        o_ref[...] = x_ref[...] + mod_ref[5:6, :] * acc_ref[...]


def _ffn(g, w1, w3, w2, layer, xf, modtab, lay, tm, tf):
    _, D, F = w1.shape
    n_lat_tiles = lay.n_lat // tm
    T, B = lay.T, lay.B
    row = lambda m, f: (m, 0)
    return pl.pallas_call(
        _ffn_body, grid=(lay.n // tm, F // tf),
        in_specs=[pl.BlockSpec((1, D), lambda m, f: (0, 0)),
                  pl.BlockSpec((None, D, tf), lambda m, f: (layer, 0, f)),
                  pl.BlockSpec((None, D, tf), lambda m, f: (layer, 0, f)),
                  pl.BlockSpec((None, tf, D), lambda m, f: (layer, f, 0)),
                  pl.BlockSpec((tm, D), row),
                  pl.BlockSpec((None, 6, D), lambda m, f: (jnp.where(m < n_lat_tiles, (m * tm) // T, B), 0, 0))],
        out_specs=pl.BlockSpec((tm, D), row),
        out_shape=jax.ShapeDtypeStruct((lay.n, D), F32),
        scratch_shapes=[pltpu.VMEM((tm, D), BF16), pltpu.VMEM((tm, D), F32)],
        compiler_params=pltpu.CompilerParams(dimension_semantics=("parallel", "arbitrary"),
                                             vmem_limit_bytes=VMEM_LIMIT),
        name="dense_ffn",
    )(g, w1, w3, w2, xf, modtab)


def _rwkv_in_body(*refs, vres, lay):
    if vres:
        (x_ref, xp_ref, xn_ref, mod_ref, g_ref, mix_ref, wr_ref, wk_ref, wv_ref, w1_ref, a1_ref, g1_ref, v1_ref,
         r_ref, k_ref, v_ref, tw_ref, ya_ref, sg_ref, yv_ref) = refs
    else:
        (x_ref, xp_ref, xn_ref, mod_ref, g_ref, mix_ref, wr_ref, wk_ref, wv_ref, w1_ref, a1_ref, g1_ref,
         r_ref, k_ref, v_ref, tw_ref, ya_ref, sg_ref) = refs
    tm = x_ref.shape[0]
    g = g_ref[...]
    shift = mod_ref[0:1, :]
    scale = mod_ref[1:2, :]
    h = _norm_mod(x_ref[...], g, shift, scale)
    h_before = _norm_mod(xp_ref[...], g, shift, scale)[HALO - 1:HALO, :]
    h_after = _norm_mod(xn_ref[...], g, shift, scale)[0:1, :]
    row = lax.broadcasted_iota(jnp.int32, (tm, 1), 0)
    gidx = pl.program_id(0) * tm + row
    in_lat = gidx < lay.n_lat
    pos = jnp.where(in_lat, gidx & (lay.T - 1), (gidx - lay.n_lat) & (lay.L - 1))
    last_pos = jnp.where(in_lat, lay.T - 1, lay.L - 1)
    prev = jnp.where(row == 0, h_before, pltpu.roll(h, 1, axis=0))
    prev = jnp.where(pos == 0, 0.0, prev)
    nxt = jnp.where(row == tm - 1, h_after, pltpu.roll(h, tm - 1, axis=0))
    nxt = jnp.where(pos == last_pos, 0.0, nxt)
    xx = 0.5 * (prev + nxt) - h

    def mixed(j):
        return (h + xx * mix_ref[j:j + 1, :]).astype(BF16)

    r_ref[...] = _dot(mixed(0), wr_ref[...])
    tw_ref[...] = jnp.tanh(_dot(mixed(1), w1_ref[...]))
    k_ref[...] = _dot(mixed(2), wk_ref[...])
    xv = mixed(3)
    v_ref[...] = _dot(xv, wv_ref[...])
    if vres:
        yv_ref[...] = _dot(xv, v1_ref[...])
    ya_ref[...] = _dot(mixed(4), a1_ref[...])
    sg_ref[...] = _sigmoid(_dot(mixed(5), g1_ref[...]))


def _rwkv_feat_body(*refs, vres):
    if vres:
        (k_ref, v_ref, tw_ref, ya_ref, sg_ref, yv_ref, vf_ref, w2_ref, a2_ref, g2_ref, w0_ref, a0_ref, kk_ref_w,
         ones_ref, v2_ref, v0_ref, lw_ref, ad_ref, kk_ref, vo_ref, g_ref) = refs
    else:
        (k_ref, v_ref, tw_ref, ya_ref, sg_ref, w2_ref, a2_ref, g2_ref, w0_ref, a0_ref, kk_ref_w,
         ones_ref, lw_ref, ad_ref, kk_ref, vo_ref, g_ref) = refs
    dec = w0_ref[...] + _dot(tw_ref[...].astype(BF16), w2_ref[...])
    w_log = -_softplus(-dec) - 0.5
    lw_ref[...] = -jnp.exp(w_log)
    ad_ref[...] = _sigmoid(a0_ref[...] + _dot(ya_ref[...].astype(BF16), a2_ref[...]))
    g_ref[...] = _dot(sg_ref[...].astype(BF16), g2_ref[...])
    kkr = k_ref[...] * kk_ref_w[...]
    ss = _head_sum(kkr * kkr, ones_ref[...])
    kk_ref[...] = kkr / jnp.maximum(jnp.sqrt(ss), 1e-12)
    v = v_ref[...]
    if vres:
        v = v + (vf_ref[...] - v) * _sigmoid(v0_ref[...] + _dot(yv_ref[...].astype(BF16), v2_ref[...]))
    vo_ref[...] = v


def _scan_body(*refs):
    dir_refs = (refs[0:6], refs[6:12])
    ka_ref = refs[12]
    y_refs = refs[13:15]
    s_ref = refs[15]
    C = CHUNK
    D = ka_ref.shape[1]
    n_heads = D // C

    @pl.when(pl.program_id(1) == 0)
    def _():
        s_ref[...] = jnp.zeros_like(s_ref)

    ri = lax.broadcasted_iota(jnp.int32, (C, C), 0)
    ci = lax.broadcasted_iota(jnp.int32, (C, C), 1)
    xr = ri ^ ci
    eye = (ri == ci).astype(F32)
    r2 = lax.broadcasted_iota(jnp.int32, (C, 2 * C), 0)
    c2 = lax.broadcasted_iota(jnp.int32, (C, 2 * C), 1) & (C - 1)
    zeros_cc = jnp.zeros((C, C), BF16)
    ka = ka_ref[...]

    ar, bk, vh, bkp, dph, strict, incl, where = [], [], [], [], [], [], [], []
    for d, (lw_ref, ad_ref, kf_ref, kk_ref, v_ref, r_ref) in enumerate(dir_refs):
        sg = 1 if d == 0 else -1
        strict_d = (r2 - c2) * sg > 0
        incl_d = (r2 - c2) * sg >= 0
        lw = lw_ref[...]
        ad = ad_ref[...]
        kk = kk_ref[...]
        c = jnp.dot(((ri - ci) * sg >= 0).astype(F32), lw, precision=HIGHEST, preferred_element_type=F32)
        pc = c[C - 1:C, :] if d == 0 else c[0:1, :]
        kd = kf_ref[...] * (1.0 + (ad - 1.0) * ka)
        b = kk * ad
        einv = jnp.exp(-c)
        a_t = (-kk * jnp.exp(c - lw)).astype(BF16)
        r_t = (r_ref[...] * jnp.exp(c)).astype(BF16)
        b_t = (b * einv).astype(BF16)
        k_t = (kd * einv).astype(BF16)
        epc = jnp.exp(pc - c)
        b_p = (b * epc).astype(BF16)
        k_p = (kd * epc).astype(BF16)
        d_p = jnp.exp(pc)
        vb = v_ref[...].astype(BF16)
        for h in range(n_heads):
            sl = slice(C * h, C * (h + 1))
            ar.append(jnp.concatenate([a_t[:, sl], r_t[:, sl]], axis=0))
            bk.append(jnp.concatenate([b_t[:, sl], k_t[:, sl]], axis=0))
            vh.append(vb[:, sl])
            bkp.append(jnp.concatenate([b_p[:, sl], k_p[:, sl]], axis=0))
            dph.append(d_p[:, sl])
            strict.append(strict_d)
            incl.append(incl_d)
            where.append((d, h, sl))

    items = range(len(where))
    sc = [lax.dot_general(ar[i], bk[i], _NT, preferred_element_type=F32) for i in items]
    st = [s_ref[d, h] for (d, h, _) in where]
    ars = [lax.dot_general(ar[i], st[i].astype(BF16), _NT, preferred_element_type=F32) for i in items]
    top = [jnp.where(strict[i], sc[i][0:C], 0.0) for i in items]
    bot = [jnp.where(incl[i], sc[i][C:], 0.0).astype(BF16) for i in items]
    lab = [top[i][:, 0:C] for i in items]
    x = [ars[i][0:C] + _dot(top[i].astype(BF16), jnp.concatenate([zeros_cc, vh[i]], axis=0)) for i in items]
    t = [eye + jnp.where(xr == 1, lab[i], 0.0) for i in items]
    for lb in range(1, 6):
        level = (xr >> lb) == 1
        tb = [t[i].astype(BF16) for i in items]
        ot = [_dot(jnp.where(level, lab[i], 0.0).astype(BF16), tb[i]).astype(BF16) for i in items]
        t = [t[i] + _dot(tb[i], ot[i]) for i in items]
    u = [_dot(t[i].astype(BF16), x[i].astype(BF16)) for i in items]
    uv = [jnp.concatenate([u[i].astype(BF16), vh[i]], axis=0) for i in items]
    for i, (d, h, sl) in enumerate(where):
        y_refs[d][:, sl] = ars[i][C:] + _dot(bot[i], uv[i])
    for i, (d, h, sl) in enumerate(where):
        s_ref[d, h] = st[i] * dph[i] + lax.dot_general(uv[i], bkp[i], _TN, preferred_element_type=F32)


def _scan(lw, ad, kf, kk, v, r, k_a, lay):
    B, T, L, D = lay.B, lay.T, lay.L, lay.D
    nc_ctx = L // CHUNK
    nc_lat = T // CHUNK
    ctx_base = (B * T) // CHUNK

    def chunk_index(b, d, s):
        ctx_i = s if d == 0 else nc_ctx - 1 - s
        lat_i = s - nc_ctx if d == 0 else nc_lat - 1 - (s - nc_ctx)
        return jnp.where(s < nc_ctx, ctx_base + b * nc_ctx + ctx_i, b * nc_lat + lat_i)

    def spec(d, col):
        return pl.BlockSpec((CHUNK, D), lambda b, s: (chunk_index(b, d, s), col))

    in_specs, args = [], []
    for d in range(2):
        in_specs += [spec(d, d), spec(d, d), spec(d, 0), spec(d, 0), spec(d, 0), spec(d, 0)]
        args += [lw, ad, kf, kk, v, r]
    return pl.pallas_call(
        _scan_body, grid=(B, nc_ctx + nc_lat),
        in_specs=in_specs + [pl.BlockSpec((1, D), lambda b, s: (0, 0))],
        out_specs=[spec(0, 0), spec(1, 0)],
        out_shape=[jax.ShapeDtypeStruct((lay.n, D), F32), jax.ShapeDtypeStruct((lay.n, D), F32)],
        scratch_shapes=[pltpu.VMEM((2, D // CHUNK, CHUNK, CHUNK), F32)],
        compiler_params=pltpu.CompilerParams(dimension_semantics=("parallel", "arbitrary"),
                                             vmem_limit_bytes=VMEM_LIMIT),
        name="wkv_scan",
    )(*args, k_a)


def _rwkv_post_body(yf_ref, yr_ref, r_ref, kf_ref, ad_ref, v_ref, g_ref, x_ref, mod_ref,
                    ka_ref, rk_ref, lnw_ref, lnb_ref, ones_ref, wo_ref, o_ref):
    D = x_ref.shape[1]
    ones4 = ones_ref[...]
    y = yf_ref[...] + yr_ref[...]
    inv_n = 1.0 / RWKV_HEAD
    yc = y - _head_sum(y, ones4) * inv_n
    var = _head_sum(yc * yc, ones4) * inv_n
    yn = yc * lax.rsqrt(var + LN_X_EPS) * lnw_ref[...] + lnb_ref[...]
    kf = kf_ref[...]
    ad = ad_ref[...]
    ka = ka_ref[...]
    k_bonus = 0.5 * (kf * (1.0 + (ad[:, :D] - 1.0) * ka) + kf * (1.0 + (ad[:, D:] - 1.0) * ka))
    bonus = _head_sum(r_ref[...] * k_bonus * rk_ref[...], ones4) * v_ref[...]
    o = ((yn + bonus) * g_ref[...]).astype(BF16)
    o_ref[...] = x_ref[...] + mod_ref[2:3, :] * _dot(o, wo_ref[...])


def _rope_tables(T, tm):
    pos = jnp.arange(T)
    inv_freq = 1.0 / (ROPE_BASE ** (jnp.arange(AXIS_PAIRS, dtype=F32) / AXIS_PAIRS))
    ang_r = (pos // GRID_W).astype(F32)[:, None] * inv_freq[None, :]
    ang_c = (pos % GRID_W).astype(F32)[:, None] * inv_freq[None, :]
    pad1 = jnp.ones((T, LANES - QK_ROPE), F32)
    pad0 = jnp.zeros((T, LANES - QK_ROPE), F32)
    cos = jnp.concatenate([jnp.cos(ang_r), jnp.cos(ang_r), jnp.cos(ang_c), jnp.cos(ang_c), pad1], axis=1)
    sin = jnp.concatenate([-jnp.sin(ang_r), jnp.sin(ang_r), -jnp.sin(ang_c), jnp.sin(ang_c), pad0], axis=1)
    cos = jnp.concatenate([cos, jnp.ones((tm, LANES), F32)], axis=0)
    sin = jnp.concatenate([sin, jnp.zeros((tm, LANES), F32)], axis=0)
    return cos, sin


def _pad_cols(w, width):
    return jnp.pad(w, ((0, 0), (0, width - w.shape[1])))


def _pad_rows(w, height):
    return jnp.pad(w, ((0, height - w.shape[0]), (0, 0)))


def _block_diag2(w0, w1):
    z0 = jnp.zeros_like(w0)
    z1 = jnp.zeros_like(w1)
    return jnp.concatenate([jnp.concatenate([w0, z1], axis=1), jnp.concatenate([z0, w1], axis=1)], axis=0)


def kernel(x, c, ctx, c_ctx, ada_w, ada_b, norm1_g, norm2_g, mla_wqa, mla_qa_norm, mla_wqb, mla_wkva, mla_kva_norm, mla_wkvb, mla_q_norm, mla_k_norm, mla_wo, rwkv_mix, rwkv_wr, rwkv_wk, rwkv_wv, rwkv_wo, rwkv_w0, rwkv_w1, rwkv_w2, rwkv_a0, rwkv_a1, rwkv_a2, rwkv_g1, rwkv_g2, rwkv_k_k, rwkv_k_a, rwkv_r_k, rwkv_ln_w, rwkv_ln_b, rwkv_v0, rwkv_v1, rwkv_v2, ffn_w1, ffn_w3, ffn_w2, moe_router, moe_w1, moe_w3, moe_w2):
    B, T, D = x.shape
    L = ctx.shape[1]
    depth = ada_w.shape[0]
    lay = _Layout(B, T, L, D)
    n = lay.n
    tm = min(512, T)
    tm_small = min(256, T)
    assert T % tm == 0 and (B * L) % tm == 0 and T % GRID_W == 0 and L % CHUNK == 0 and B + 1 <= 16
    assert T & (T - 1) == 0 and L & (L - 1) == 0 and L % tm_small == 0

    ffn_b = [w.astype(BF16) for w in (ffn_w1, ffn_w3, ffn_w2)]
    moe_b = [w.astype(BF16) for w in (moe_w1, moe_w3, moe_w2)]
    xf = jnp.concatenate([x.reshape(B * T, D), ctx.reshape(B * L, D)], axis=0)
    cc = jnp.concatenate([c, c_ctx[None, :], jnp.zeros((16 - B - 1, D), F32)], axis=0)
    modall = _ada_call(cc, ada_w, ada_b)[:, :B + 1].reshape(depth, B + 1, 6, D)
    cos_t, sin_t = _rope_tables(T, tm)
    ones4 = jnp.kron(jnp.eye(4, dtype=F32), jnp.ones((RWKV_HEAD, RWKV_HEAD), F32)).astype(BF16)
    row1 = lambda z: z.reshape(1, -1)

    v_first = None
    for i in range(depth):
        j = i // 2
        modtab = modall[i]
        mod_t = (modtab, lay.mod_spec(tm))
        mod_s = (modtab, lay.mod_spec(tm_small))
        if i % 2 == 0:
            wa = jnp.concatenate([mla_wqa[j], _pad_cols(mla_wkva[j], KV_LORA + LANES)], axis=1).astype(BF16)
            wqb = mla_wqb[j].reshape(Q_LORA, MLA_HEADS, QK_HEAD)
            wqb = jnp.pad(wqb, ((0, 0), (0, 0), (0, QK_PAD - QK_HEAD))).reshape(Q_LORA, MLA_HEADS * QK_PAD).astype(BF16)
            wkvb = mla_wkvb[j].reshape(KV_LORA, MLA_HEADS, QK_NOPE + V_HEAD)
            wk = wkvb[:, :, :QK_NOPE].reshape(KV_LORA, MLA_HEADS * QK_NOPE).astype(BF16)
            wv = wkvb[:, :, QK_NOPE:].reshape(KV_LORA, MLA_HEADS * V_HEAD).astype(BF16)
            gq = _pad_cols(row1(mla_q_norm[j]) * SM_SCALE, QK_PAD)
            gk = _pad_cols(row1(mla_k_norm[j]), QK_PAD)
            cq, ckv, kr = _rowwise(
                _mla_a_body, "mla_a", n, tm, [xf], [mod_t],
                [row1(norm1_g[i]), wa, row1(mla_qa_norm[j]), row1(mla_kva_norm[j])],
                [(Q_LORA, BF16), (KV_LORA, BF16), (LANES, F32)])
            rope_in = [(cos_t, lay.rope_spec(tm)), (sin_t, lay.rope_spec(tm))]
            (q,) = _rowwise(_mla_q_body, "mla_q", n, tm, [cq], rope_in, [wqb, gq], [(MLA_HEADS * QK_PAD, BF16)])
            k, v = _rowwise(_mla_kv_body, "mla_kv", n, tm, [ckv, kr], rope_in, [wk, wv, gk],
                            [(MLA_HEADS * QK_PAD, BF16), (MLA_HEADS * V_PAD, BF16)])
            o = _attention(q, k, v, lay, tq=min(2048, T))
            (xf,) = _rowwise(_proj_resid_body, "mla_out", n, tm, [o, xf], [mod_t], [mla_wo[j].astype(BF16)], [(D, F32)])
        else:
            vres = j > 0
            mix = _pad_rows(rwkv_mix[j], 8)
            w1c = jnp.concatenate([rwkv_w1[j, 0], rwkv_w1[j, 1]], axis=1).astype(BF16)
            a1c = jnp.concatenate([rwkv_a1[j, 0], rwkv_a1[j, 1]], axis=1).astype(BF16)
            g1p = _pad_cols(rwkv_g1[j], 2 * LANES).astype(BF16)
            consts = [row1(norm1_g[i]), mix, rwkv_wr[j].astype(BF16), rwkv_wk[j].astype(BF16),
                      rwkv_wv[j].astype(BF16), w1c, a1c, g1p]
            outs = [(D, F32), (D, F32), (D, F32), (LANES, F32), (LANES, F32), (2 * LANES, F32)]
            if vres:
                consts.append(_pad_cols(rwkv_v1[j - 1], LANES).astype(BF16))
                outs.append((LANES, F32))
            halo_per_tile = tm_small // HALO
            before = pl.BlockSpec((HALO, D), lambda t: (jnp.maximum(t * halo_per_tile - 1, 0), 0))
            after = pl.BlockSpec((HALO, D), lambda t: (jnp.minimum((t + 1) * halo_per_tile, n // HALO - 1), 0))
            res = _rowwise(functools.partial(_rwkv_in_body, vres=vres, lay=lay), "rwkv_in", n, tm_small,
                           [xf], [(xf, before), (xf, after), mod_s], consts, outs)
            r, kf, v_raw, tw, ya, sgate = res[:6]
            w2c = _block_diag2(rwkv_w2[j, 0], rwkv_w2[j, 1]).astype(BF16)
            a2c = _block_diag2(rwkv_a2[j, 0], rwkv_a2[j, 1]).astype(BF16)
            g2p = _pad_rows(rwkv_g2[j], 2 * LANES).astype(BF16)
            rows = [kf, v_raw, tw, ya, sgate]
            consts = [w2c, a2c, g2p, rwkv_w0[j].reshape(1, 2 * D), rwkv_a0[j].reshape(1, 2 * D), row1(rwkv_k_k[j]), ones4]
            if vres:
                rows += [res[6], v_first]
                consts += [_pad_rows(rwkv_v2[j - 1], LANES).astype(BF16), row1(rwkv_v0[j - 1])]
            lw, ad, kk, v_out, gate = _rowwise(
                functools.partial(_rwkv_feat_body, vres=vres), "rwkv_feat", n, tm_small, rows, [], consts,
                [(2 * D, F32), (2 * D, F32), (D, F32), (D, F32), (D, F32)])
            if j == 0:
                v_first = v_out
            k_a = row1(rwkv_k_a[j])
            y_fwd, y_rev = _scan(lw, ad, kf, kk, v_out, r, k_a, lay)
            (xf,) = _rowwise(
                _rwkv_post_body, "rwkv_out", n, tm_small, [y_fwd, y_rev, r, kf, ad, v_out, gate, xf], [mod_s],
                [k_a, row1(rwkv_r_k[j]), row1(rwkv_ln_w[j]), row1(rwkv_ln_b[j]), ones4, rwkv_wo[j].astype(BF16)],
                [(D, F32)])

        if i % 2 == 0:
            xf = _ffn(row1(norm2_g[i]), ffn_b[0], ffn_b[1], ffn_b[2], j, xf, modtab, lay, tm, tf=ffn_w1.shape[2] // 2)
        else:
            h, route = _rowwise(_router_body, "router", n, tm, [xf], [mod_t],
                                [row1(norm2_g[i]), _pad_cols(moe_router[j], LANES)], [(D, F32), (LANES, F32)])
            pos1, pos2, src, tile_expert, n_used = _route_plan(route, n, MOE_TM)
            ys = _gffn(tile_expert, n_used, src, h, moe_b[0], moe_b[1], moe_b[2], j, MOE_TM, tf=moe_w1.shape[3] // 4)
            n_out = lay.n_lat if i == depth - 1 else n
            xf = _moe_combine(pos1, pos2, ys, route, xf, modtab, lay, tm, n_out)
    return xf[:B * T].reshape(B, T, D)
```

```python
import functools

import jax
import jax.numpy as jnp
from jax import lax
from jax.experimental import pallas as pl
from jax.experimental.pallas import tpu as pltpu

F32 = jnp.float32
BF16 = jnp.bfloat16
HIGHEST = lax.Precision.HIGHEST

GRID_W = 64
NORM_EPS = 1e-6
MLA_HEADS = 8
QK_NOPE = 128
QK_ROPE = 64
V_HEAD = 128
QK_HEAD = QK_NOPE + QK_ROPE
Q_LORA = 384
KV_LORA = 256
ROPE_BASE = 10000.0
AXIS_PAIRS = QK_ROPE // 4
SM_SCALE = QK_HEAD ** -0.5
RWKV_HEAD = 64
LN_X_EPS = 64e-5
N_EXPERTS = 8

LANES = 128
QK_PAD = 2 * LANES
V_PAD = 2 * LANES
CHUNK = 64
MOE_TM = 512
IDX_ROW = 1024
ATTN_SUB = 512
ATTN_KV = 8192
DMA_UNROLL = 8
HALO = 8
VMEM_LIMIT = 56 * 1024 * 1024

_NT = (((1,), (1,)), ((), ()))
_TN = (((0,), (0,)), ((), ()))


def _dot(a, b):
    return jnp.dot(a, b, preferred_element_type=F32)


def _sigmoid(z):
    return 1.0 / (1.0 + jnp.exp(-z))


def _softplus(z):
    return jnp.maximum(z, 0.0) + jnp.log(1.0 + jnp.exp(-jnp.abs(z)))


def _rms(z, width):
    return lax.rsqrt(jnp.sum(z * z, axis=-1, keepdims=True) * (1.0 / width) + NORM_EPS)


def _norm_mod(x, g, shift, scale):
    return (x * _rms(x, x.shape[-1])) * g * (1.0 + scale) + shift


def _rope(z, cos, sin):
    lane = lax.broadcasted_iota(jnp.int32, z.shape, 1)
    partner = jnp.where((lane & 16) == 0, pltpu.roll(z, LANES - 16, axis=1), pltpu.roll(z, 16, axis=1))
    return z * cos + partner * sin


def _head_sum(z, ones4):
    outs = []
    for q in range(z.shape[1] // 256):
        blk = z[:, 256 * q:256 * (q + 1)]
        hi = blk.astype(BF16)
        lo = (blk - hi.astype(F32)).astype(BF16)
        outs.append(_dot(hi, ones4) + _dot(lo, ones4))
    return jnp.concatenate(outs, axis=1)


class _Layout:
    def __init__(self, B, T, L, D):
        self.B, self.T, self.L, self.D = B, T, L, D
        self.n_lat = B * T
        self.n = B * T + B * L

    def mod_spec(self, tm):
        n_lat_tiles = self.n_lat // tm
        T, B, D = self.T, self.B, self.D
        return pl.BlockSpec((None, 6, D), lambda i: (jnp.where(i < n_lat_tiles, (i * tm) // T, B), 0, 0))

    def rope_spec(self, tm):
        n_lat_tiles = self.n_lat // tm
        tpb = self.T // tm
        return pl.BlockSpec((tm, LANES), lambda i: (jnp.where(i < n_lat_tiles, i % tpb, tpb), 0))


def _const_spec(arr):
    nd = arr.ndim
    return pl.BlockSpec(arr.shape, lambda i: (0,) * nd, pipeline_mode=pl.Buffered(1))


def _rowwise(body, name, n_rows, tm, row_ins, tile_ins, const_ins, outs):
    in_specs = [pl.BlockSpec((tm, a.shape[1]), lambda i: (i, 0)) for a in row_ins]
    in_specs += [spec for (_, spec) in tile_ins]
    in_specs += [_const_spec(a) for a in const_ins]
    out_specs = [pl.BlockSpec((tm, w), lambda i: (i, 0)) for (w, _) in outs]
    out_shape = [jax.ShapeDtypeStruct((n_rows, w), dt) for (w, dt) in outs]
    return pl.pallas_call(
        body, grid=(n_rows // tm,), in_specs=in_specs, out_specs=out_specs, out_shape=out_shape,
        compiler_params=pltpu.CompilerParams(dimension_semantics=("parallel",), vmem_limit_bytes=VMEM_LIMIT),
        name=name,
    )(*row_ins, *[a for (a, _) in tile_ins], *const_ins)


def _ada_body(c_ref, w_ref, b_ref, o_ref):
    c = c_ref[...]
    o_ref[...] = jnp.dot(c * _sigmoid(c), w_ref[...], precision=HIGHEST, preferred_element_type=F32) + b_ref[...]


def _ada_call(cc, ada_w, ada_b):
    depth, D, W = ada_w.shape
    tn = 1024
    return pl.pallas_call(
        _ada_body, grid=(depth, W // tn),
        in_specs=[pl.BlockSpec(cc.shape, lambda l, j: (0, 0)),
                  pl.BlockSpec((None, D, tn), lambda l, j: (l, 0, j)),
                  pl.BlockSpec((None, 1, tn), lambda l, j: (l, 0, j))],
        out_specs=pl.BlockSpec((None, cc.shape[0], tn), lambda l, j: (l, 0, j)),
        out_shape=jax.ShapeDtypeStruct((depth, cc.shape[0], W), F32),
        compiler_params=pltpu.CompilerParams(dimension_semantics=("parallel", "parallel"), vmem_limit_bytes=VMEM_LIMIT),
        name="ada",
    )(cc, ada_w, ada_b.reshape(depth, 1, W))


def _mla_a_body(x_ref, mod_ref, g_ref, wa_ref, qan_ref, kvn_ref, cq_ref, ckv_ref, kr_ref):
    h = _norm_mod(x_ref[...], g_ref[...], mod_ref[0:1, :], mod_ref[1:2, :]).astype(BF16)
    acc = _dot(h, wa_ref[...])
    qa = acc[:, 0:Q_LORA]
    kv = acc[:, Q_LORA:Q_LORA + KV_LORA]
    cq_ref[...] = (qa * _rms(qa, Q_LORA) * qan_ref[...]).astype(BF16)
    ckv_ref[...] = (kv * _rms(kv, KV_LORA) * kvn_ref[...]).astype(BF16)
    kr_ref[...] = acc[:, Q_LORA + KV_LORA:]


def _mla_q_body(cq_ref, cos_ref, sin_ref, wqb_ref, gq_ref, q_ref):
    acc = _dot(cq_ref[...], wqb_ref[...])
    cos = cos_ref[...]
    sin = sin_ref[...]
    g = gq_ref[...]
    for h in range(MLA_HEADS):
        a = acc[:, QK_PAD * h:QK_PAD * (h + 1)]
        an = a * _rms(a, QK_HEAD) * g
        q_ref[:, QK_PAD * h:QK_PAD * h + LANES] = an[:, :LANES].astype(BF16)
        q_ref[:, QK_PAD * h + LANES:QK_PAD * (h + 1)] = _rope(an[:, LANES:], cos, sin).astype(BF16)


def _mla_kv_body(ckv_ref, kr_ref, cos_ref, sin_ref, wk_ref, wv_ref, gk_ref, k_ref, v_ref):
    ckv = ckv_ref[...]
    kn = _dot(ckv, wk_ref[...])
    vals = _dot(ckv, wv_ref[...]).astype(BF16)
    lane = lax.broadcasted_iota(jnp.int32, (vals.shape[0], LANES), 1)
    ones_col = jnp.where(lane == 0, 1.0, 0.0).astype(BF16)
    for h in range(MLA_HEADS):
        v_ref[:, V_PAD * h:V_PAD * h + V_HEAD] = vals[:, V_HEAD * h:V_HEAD * (h + 1)]
        v_ref[:, V_PAD * h + V_HEAD:V_PAD * (h + 1)] = ones_col
    kr = kr_ref[...]
    ss_rope = jnp.sum(kr * kr, axis=-1, keepdims=True)
    g = gk_ref[...]
    g_nope = g[:, :LANES]
    kr_rot = _rope(kr * g[:, LANES:], cos_ref[...], sin_ref[...])
    for h in range(MLA_HEADS):
        a = kn[:, LANES * h:LANES * (h + 1)]
        inv = lax.rsqrt((jnp.sum(a * a, axis=-1, keepdims=True) + ss_rope) * (1.0 / QK_HEAD) + NORM_EPS)
        k_ref[:, QK_PAD * h:QK_PAD * h + LANES] = (a * inv * g_nope).astype(BF16)
        k_ref[:, QK_PAD * h + LANES:QK_PAD * (h + 1)] = (kr_rot * inv).astype(BF16)


def _attn_body(*refs, nseg, sub):
    q_ref = refs[0]
    o_ref = refs[-1]
    n_sub = q_ref.shape[0] // sub

    chunks = []
    for i in range(nseg):
        rows = refs[1 + 2 * i].shape[0]
        step = min(ATTN_KV, rows)
        chunks += [(i, c0, step) for c0 in range(0, rows, step)]

    def scores(j):
        q = q_ref[sub * j:sub * (j + 1), :]
        ss = [lax.dot_general(q, refs[1 + 2 * i][c0:c0 + w, :], _NT, preferred_element_type=F32) for (i, c0, w) in chunks]
        m = functools.reduce(jnp.maximum, [jnp.max(s, axis=-1, keepdims=True) for s in ss])
        return ss, m

    def finish(j, scored):
        ss, m = scored
        acc = functools.reduce(lambda a, b: a + b,
                               [_dot(jnp.exp((s - m).astype(BF16)), refs[2 + 2 * i][c0:c0 + w, :])
                                for s, (i, c0, w) in zip(ss, chunks)])
        o_ref[sub * j:sub * (j + 1), :] = (acc[:, :V_HEAD] / acc[:, V_HEAD:V_HEAD + 1]).astype(o_ref.dtype)

    pending = scores(0)
    for j in range(n_sub):
        following = scores(j + 1) if j + 1 < n_sub else None
        finish(j, pending)
        pending = following


def _attention(q, k, v, lay, tq):
    B, T, L = lay.B, lay.T, lay.L
    nq = T // tq
    ctx0 = (B * T) // L
    params = pltpu.CompilerParams(dimension_semantics=("parallel", "parallel", "arbitrary"),
                                  vmem_limit_bytes=VMEM_LIMIT)
    o_lat = pl.pallas_call(
        functools.partial(_attn_body, nseg=2, sub=min(ATTN_SUB, tq)), grid=(B, MLA_HEADS, nq),
        in_specs=[pl.BlockSpec((tq, QK_PAD), lambda b, h, i: (b * nq + i, h)),
                  pl.BlockSpec((L, QK_PAD), lambda b, h, i: (ctx0 + b, h)),
                  pl.BlockSpec((L, V_PAD), lambda b, h, i: (ctx0 + b, h)),
                  pl.BlockSpec((T, QK_PAD), lambda b, h, i: (b, h)),
                  pl.BlockSpec((T, V_PAD), lambda b, h, i: (b, h))],
        out_specs=pl.BlockSpec((tq, V_HEAD), lambda b, h, i: (b * nq + i, h)),
        out_shape=jax.ShapeDtypeStruct((B * T, MLA_HEADS * V_HEAD), BF16),
        compiler_params=params, name="attn_latent",
    )(q, k, v, k, v)
    o_ctx = pl.pallas_call(
        functools.partial(_attn_body, nseg=1, sub=min(ATTN_SUB, L)), grid=(B, MLA_HEADS, 1),
        in_specs=[pl.BlockSpec((L, QK_PAD), lambda b, h, i: (ctx0 + b, h)),
                  pl.BlockSpec((L, QK_PAD), lambda b, h, i: (ctx0 + b, h)),
                  pl.BlockSpec((L, V_PAD), lambda b, h, i: (ctx0 + b, h))],
        out_specs=pl.BlockSpec((L, V_HEAD), lambda b, h, i: (b, h)),
        out_shape=jax.ShapeDtypeStruct((B * L, MLA_HEADS * V_HEAD), BF16),
        compiler_params=params, name="attn_ctx",
    )(q, k, v)
    return jnp.concatenate([o_lat, o_ctx], axis=0)


def _proj_resid_body(a_ref, x_ref, mod_ref, w_ref, o_ref):
    o_ref[...] = x_ref[...] + mod_ref[2:3, :] * _dot(a_ref[...], w_ref[...])


def _router_body(x_ref, mod_ref, g_ref, router_ref, h_ref, route_ref):
    hf = _norm_mod(x_ref[...], g_ref[...], mod_ref[3:4, :], mod_ref[4:5, :])
    h_ref[...] = hf
    logits = jnp.dot(hf, router_ref[...], precision=HIGHEST, preferred_element_type=F32)
    lane = lax.broadcasted_iota(jnp.int32, logits.shape, 1)
    lanef = lane.astype(F32)
    neg = jnp.float32(-1e30)
    lg = jnp.where(lane < N_EXPERTS, logits, neg)
    m1 = jnp.max(lg, axis=-1, keepdims=True)
    i1 = jnp.min(jnp.where(lg == m1, lanef, float(LANES)), axis=-1, keepdims=True)
    lg2 = jnp.where(lanef == i1, neg, lg)
    m2 = jnp.max(lg2, axis=-1, keepdims=True)
    i2 = jnp.min(jnp.where(lg2 == m2, lanef, float(LANES)), axis=-1, keepdims=True)
    e2 = jnp.exp(m2 - m1)
    w1 = 1.0 / (1.0 + e2)
    w2 = e2 / (1.0 + e2)
    route_ref[...] = jnp.where(lane == 0, i1, jnp.where(lane == 1, i2, jnp.where(lane == 2, w1, jnp.where(lane == 3, w2, 0.0))))


def _route_plan(route, n, tm_g):
    i1 = route[:, 0].astype(jnp.int32)
    i2 = route[:, 1].astype(jnp.int32)
    experts = jnp.arange(N_EXPERTS, dtype=jnp.int32)
    onehot = ((i1[:, None] == experts) | (i2[:, None] == experts)).astype(jnp.int32)
    rank = jnp.cumsum(onehot, axis=0) - onehot
    tiles_e = (jnp.sum(onehot, axis=0) + tm_g - 1) // tm_g
    tile_end = jnp.cumsum(tiles_e)
    start = (tile_end - tiles_e) * tm_g
    pos1 = start[i1] + jnp.take_along_axis(rank, i1[:, None], axis=1)[:, 0]
    pos2 = start[i2] + jnp.take_along_axis(rank, i2[:, None], axis=1)[:, 0]
    n_tiles = -(-2 * n // tm_g) + N_EXPERTS
    tile_ids = jnp.arange(n_tiles, dtype=jnp.int32)
    tile_expert = jnp.minimum(jnp.sum((tile_ids[:, None] >= tile_end[None, :]).astype(jnp.int32), axis=1), N_EXPERTS - 1)
    tok = jnp.arange(n, dtype=jnp.int32)
    src = jnp.zeros((n_tiles * tm_g,), jnp.int32).at[jnp.concatenate([pos1, pos2])].set(
        jnp.concatenate([tok, tok]), unique_indices=True, mode="promise_in_bounds")
    src = _pad_cols(src.reshape(n_tiles, tm_g), IDX_ROW)
    return pos1, pos2, src, tile_expert.astype(jnp.int32), tile_end[-1:].astype(jnp.int32)


def _row_copy(src, src_row, dst, dst_row, sem):
    return pltpu.make_async_copy(src.at[pl.ds(src_row, 1), :], dst.at[pl.ds(dst_row, 1), :], sem)


def _gffn_body(te_ref, nu_ref, src_hbm, h_hbm, w1_ref, w3_ref, w2_ref, ys_ref,
               idx, xg, acc_ref, isem, dsem, *, tm, tf):
    del te_ref
    m = pl.program_id(0)
    n_m = pl.num_programs(0)
    slot = m & 1
    other = 1 - slot
    n_used = nu_ref[0]
    used = m < n_used
    nf = w1_ref.shape[1] // tf
    per_f = tm // nf

    def fetch_idx(tile, s):
        fetch = pltpu.make_async_copy(src_hbm.at[tile], idx.at[s], isem)
        fetch.start()
        fetch.wait()

    def wait_slot(s):
        pltpu.make_async_copy(xg.at[s], xg.at[s], dsem.at[s]).wait()

    @pl.when(m == 0)
    def _():
        fetch_idx(0, 0)

        def issue(t, carry):
            _row_copy(h_hbm, idx[0, t], xg.at[0], t, dsem.at[0]).start()
            return carry

        lax.fori_loop(0, tm, issue, 0, unroll=DMA_UNROLL)

    @pl.when(m <= n_used)
    def _():
        wait_slot(slot)

    @pl.when(used)
    def _():
        fetch_idx(jnp.minimum(m + 1, n_m - 1), other)
        h = xg[slot].astype(BF16)
        for c in range(nf):
            for t in range(per_f * c, per_f * (c + 1)):
                _row_copy(h_hbm, idx[other, t], xg.at[other], t, dsem.at[other]).start()
            a1 = _dot(h, w1_ref[:, tf * c:tf * (c + 1)])
            a3 = _dot(h, w3_ref[:, tf * c:tf * (c + 1)])
            part = _dot((a1 * _sigmoid(a1) * a3).astype(BF16), w2_ref[tf * c:tf * (c + 1), :])
            if c == 0:
                acc_ref[...] = part
            else:
                acc_ref[...] += part
        ys_ref[...] = acc_ref[...]

    @pl.when(jnp.logical_not(used))
    def _():
        ys_ref[...] = jnp.zeros_like(ys_ref)

    @pl.when(used & (m == n_m - 1))
    def _():
        wait_slot(other)


def _gffn(tile_expert, n_used, src, h, w1, w3, w2, layer, tm_g, tf):
    _, E, D, F = w1.shape
    n_tiles = src.shape[0]
    any_spec = pl.BlockSpec(memory_space=pl.ANY)
    resident = pl.Buffered(1)
    grid_spec = pltpu.PrefetchScalarGridSpec(
        num_scalar_prefetch=2, grid=(n_tiles,),
        in_specs=[any_spec, any_spec,
                  pl.BlockSpec((None, None, D, F), lambda m, te, nu: (layer, te[m], 0, 0), pipeline_mode=resident),
                  pl.BlockSpec((None, None, D, F), lambda m, te, nu: (layer, te[m], 0, 0), pipeline_mode=resident),
                  pl.BlockSpec((None, None, F, D), lambda m, te, nu: (layer, te[m], 0, 0), pipeline_mode=resident)],
        out_specs=pl.BlockSpec((tm_g, D), lambda m, te, nu: (m, 0)),
        scratch_shapes=[pltpu.SMEM((2, IDX_ROW), jnp.int32), pltpu.VMEM((2, tm_g, D), F32),
                        pltpu.VMEM((tm_g, D), F32),
                        pltpu.SemaphoreType.DMA, pltpu.SemaphoreType.DMA((2,))])
    return pl.pallas_call(
        functools.partial(_gffn_body, tm=tm_g, tf=tf), grid_spec=grid_spec,
        out_shape=jax.ShapeDtypeStruct((n_tiles * tm_g, D), F32),
        compiler_params=pltpu.CompilerParams(dimension_semantics=("arbitrary",),
                                             vmem_limit_bytes=VMEM_LIMIT),
        name="moe_gffn",
    )(tile_expert, n_used, src, h, w1, w3, w2)


def _moe_combine_body(pos_hbm, ys_hbm, route_ref, x_ref, mod_ref, o_ref, idx, yg, isem, dsem, *, tm):
    i = pl.program_id(0)
    slot = i & 1

    def start_gather(tile, s):
        fetch = pltpu.make_async_copy(pos_hbm.at[tile], idx.at[s], isem)
        fetch.start()
        fetch.wait()

        def issue(t, carry):
            _row_copy(ys_hbm, idx[s, t], yg.at[s, 0], t, dsem.at[s]).start()
            _row_copy(ys_hbm, idx[s, tm + t], yg.at[s, 1], t, dsem.at[s]).start()
            return carry

        lax.fori_loop(0, tm, issue, 0, unroll=DMA_UNROLL)

    @pl.when(i == 0)
    def _():
        start_gather(0, 0)

    pltpu.make_async_copy(yg.at[slot], yg.at[slot], dsem.at[slot]).wait()

    @pl.when(i + 1 < pl.num_programs(0))
    def _():
        start_gather(i + 1, 1 - slot)

    route = route_ref[...]
    y = route[:, 2:3] * yg[slot, 0] + route[:, 3:4] * yg[slot, 1]
    o_ref[...] = x_ref[...] + mod_ref[5:6, :] * y


def _moe_combine(pos1, pos2, ys, route, xf, modtab, lay, tm, n_out):
    n, D = xf.shape
    pos = jnp.concatenate([pos1.reshape(n // tm, tm), pos2.reshape(n // tm, tm)], axis=1)
    any_spec = pl.BlockSpec(memory_space=pl.ANY)
    row = lambda i: (i, 0)
    return pl.pallas_call(
        functools.partial(_moe_combine_body, tm=tm), grid=(n_out // tm,),
        in_specs=[any_spec, any_spec, pl.BlockSpec((tm, LANES), row), pl.BlockSpec((tm, D), row), lay.mod_spec(tm)],
        out_specs=pl.BlockSpec((tm, D), row),
        out_shape=jax.ShapeDtypeStruct((n_out, D), F32),
        scratch_shapes=[pltpu.SMEM((2, 2 * tm), jnp.int32), pltpu.VMEM((2, 2, tm, D), F32),
                        pltpu.SemaphoreType.DMA, pltpu.SemaphoreType.DMA((2,))],
        compiler_params=pltpu.CompilerParams(dimension_semantics=("arbitrary",), vmem_limit_bytes=VMEM_LIMIT),
        name="moe_combine",
    )(pos, ys, route, xf, modtab)


def _ffn_body(x_ref, mod_ref, g_ref, w1_ref, w3_ref, w2_ref, o_ref, *, tf):
    x = x_ref[...]
    h = _norm_mod(x, g_ref[...], mod_ref[3:4, :], mod_ref[4:5, :]).astype(BF16)
    acc = None
    for c in range(w1_ref.shape[1] // tf):
        a1 = _dot(h, w1_ref[:, tf * c:tf * (c + 1)])
        a3 = _dot(h, w3_ref[:, tf * c:tf * (c + 1)])
        part = _dot((a1 * _sigmoid(a1) * a3).astype(BF16), w2_ref[tf * c:tf * (c + 1), :])
        acc = part if acc is None else acc + part
    o_ref[...] = x + mod_ref[5:6, :] * acc


def _rwkv_feat_body(*refs, vres, lay):
    if vres:
        (x_ref, vf_ref, xp_ref, xn_ref, mod_ref, g_ref, mix_ref, wr_ref, wk_ref, wv_ref, w1_ref, a1_ref, g1_ref,
         w2_ref, a2_ref, g2_ref, w0_ref, a0_ref, kkw_ref, ones_ref, v1_ref, v2_ref, v0_ref,
         r_ref, k_ref, lw_ref, ad_ref, kk_ref, vo_ref, gate_ref) = refs
    else:
        (x_ref, xp_ref, xn_ref, mod_ref, g_ref, mix_ref, wr_ref, wk_ref, wv_ref, w1_ref, a1_ref, g1_ref,
         w2_ref, a2_ref, g2_ref, w0_ref, a0_ref, kkw_ref, ones_ref,
         r_ref, k_ref, lw_ref, ad_ref, kk_ref, vo_ref, gate_ref) = refs
    tm = x_ref.shape[0]
    g = g_ref[...]
    shift = mod_ref[0:1, :]
    scale = mod_ref[1:2, :]
    h = _norm_mod(x_ref[...], g, shift, scale)
    h_before = _norm_mod(xp_ref[...], g, shift, scale)[HALO - 1:HALO, :]
    h_after = _norm_mod(xn_ref[...], g, shift, scale)[0:1, :]
    row = lax.broadcasted_iota(jnp.int32, (tm, 1), 0)
    gidx = pl.program_id(0) * tm + row
    in_lat = gidx < lay.n_lat
    pos = jnp.where(in_lat, gidx & (lay.T - 1), (gidx - lay.n_lat) & (lay.L - 1))
    last_pos = jnp.where(in_lat, lay.T - 1, lay.L - 1)
    prev = jnp.where(row == 0, h_before, pltpu.roll(h, 1, axis=0))
    prev = jnp.where(pos == 0, 0.0, prev)
    nxt = jnp.where(row == tm - 1, h_after, pltpu.roll(h, tm - 1, axis=0))
    nxt = jnp.where(pos == last_pos, 0.0, nxt)
    xx = 0.5 * (prev + nxt) - h

    def mixed(j):
        return (h + xx * mix_ref[j:j + 1, :]).astype(BF16)

    r_ref[...] = _dot(mixed(0), wr_ref[...])
    tw = jnp.tanh(_dot(mixed(1), w1_ref[...]))
    dec = w0_ref[...] + _dot(tw.astype(BF16), w2_ref[...])
    w_log = -_softplus(-dec) - 0.5
    lw_ref[...] = -jnp.exp(w_log)
    k = _dot(mixed(2), wk_ref[...])
    k_ref[...] = k
    kkr = k * kkw_ref[...]
    ss = _head_sum(kkr * kkr, ones_ref[...])
    kk_ref[...] = kkr / jnp.maximum(jnp.sqrt(ss), 1e-12)
    xv = mixed(3)
    v = _dot(xv, wv_ref[...])
    if vres:
        yv = _dot(xv, v1_ref[...])
        v = v + (vf_ref[...] - v) * _sigmoid(v0_ref[...] + _dot(yv.astype(BF16), v2_ref[...]))
    vo_ref[...] = v
    ya = _dot(mixed(4), a1_ref[...])
    ad_ref[...] = _sigmoid(a0_ref[...] + _dot(ya.astype(BF16), a2_ref[...]))
    sgate = _sigmoid(_dot(mixed(5), g1_ref[...]))
    gate_ref[...] = _dot(sgate.astype(BF16), g2_ref[...])


def _scan_body(*refs):
    dir_refs = (refs[0:6], refs[6:12])
    ka_ref = refs[12]
    y_refs = refs[13:15]
    s_ref = refs[15]
    C = CHUNK
    D = ka_ref.shape[1]
    n_heads = D // C

    @pl.when(pl.program_id(1) == 0)
    def _():
        s_ref[...] = jnp.zeros_like(s_ref)

    ri = lax.broadcasted_iota(jnp.int32, (C, C), 0)
    ci = lax.broadcasted_iota(jnp.int32, (C, C), 1)
    xr = ri ^ ci
    eye = (ri == ci).astype(F32)
    r2 = lax.broadcasted_iota(jnp.int32, (C, 2 * C), 0)
    c2 = lax.broadcasted_iota(jnp.int32, (C, 2 * C), 1) & (C - 1)
    zeros_cc = jnp.zeros((C, C), BF16)
    ka = ka_ref[...]

    ar, bk, vh, bkp, dph, strict, incl, where = [], [], [], [], [], [], [], []
    for d, (lw_ref, ad_ref, kf_ref, kk_ref, v_ref, r_ref) in enumerate(dir_refs):
        sg = 1 if d == 0 else -1
        strict_d = (r2 - c2) * sg > 0
        incl_d = (r2 - c2) * sg >= 0
        lw = lw_ref[...]
        ad = ad_ref[...]
        kk = kk_ref[...]
        c = jnp.dot(((ri - ci) * sg >= 0).astype(F32), lw, precision=HIGHEST, preferred_element_type=F32)
        pc = c[C - 1:C, :] if d == 0 else c[0:1, :]
        kd = kf_ref[...] * (1.0 + (ad - 1.0) * ka)
        b = kk * ad
        einv = jnp.exp(-c)
        a_t = (-kk * jnp.exp(c - lw)).astype(BF16)
        r_t = (r_ref[...] * jnp.exp(c)).astype(BF16)
        b_t = (b * einv).astype(BF16)
        k_t = (kd * einv).astype(BF16)
        epc = jnp.exp(pc - c)
        b_p = (b * epc).astype(BF16)
        k_p = (kd * epc).astype(BF16)
        d_p = jnp.exp(pc)
        vb = v_ref[...].astype(BF16)
        for h in range(n_heads):
            sl = slice(C * h, C * (h + 1))
            ar.append(jnp.concatenate([a_t[:, sl], r_t[:, sl]], axis=0))
            bk.append(jnp.concatenate([b_t[:, sl], k_t[:, sl]], axis=0))
            vh.append(vb[:, sl])
            bkp.append(jnp.concatenate([b_p[:, sl], k_p[:, sl]], axis=0))
            dph.append(d_p[:, sl])
            strict.append(strict_d)
            incl.append(incl_d)
            where.append((d, h, sl))

    items = range(len(where))
    sc = [lax.dot_general(ar[i], bk[i], _NT, preferred_element_type=F32) for i in items]
    st = [s_ref[d, h] for (d, h, _) in where]
    ars = [lax.dot_general(ar[i], st[i].astype(BF16), _NT, preferred_element_type=F32) for i in items]
    top = [jnp.where(strict[i], sc[i][0:C], 0.0) for i in items]
    bot = [jnp.where(incl[i], sc[i][C:], 0.0).astype(BF16) for i in items]
    lab = [top[i][:, 0:C] for i in items]
    x = [ars[i][0:C] + _dot(top[i].astype(BF16), jnp.concatenate([zeros_cc, vh[i]], axis=0)) for i in items]
    t = [eye + jnp.where(xr == 1, lab[i], 0.0) for i in items]
    for lb in range(1, 6):
        level = (xr >> lb) == 1
        tb = [t[i].astype(BF16) for i in items]
        ot = [_dot(jnp.where(level, lab[i], 0.0).astype(BF16), tb[i]).astype(BF16) for i in items]
        t = [t[i] + _dot(tb[i], ot[i]) for i in items]
    u = [_dot(t[i].astype(BF16), x[i].astype(BF16)) for i in items]
    uv = [jnp.concatenate([u[i].astype(BF16), vh[i]], axis=0) for i in items]
    for i, (d, h, sl) in enumerate(where):
        y_refs[d][:, sl] = ars[i][C:] + _dot(bot[i], uv[i])
    for i, (d, h, sl) in enumerate(where):
        s_ref[d, h] = st[i] * dph[i] + lax.dot_general(uv[i], bkp[i], _TN, preferred_element_type=F32)


def _scan(lw, ad, kf, kk, v, r, k_a, lay):
    B, T, L, D = lay.B, lay.T, lay.L, lay.D
    nc_ctx = L // CHUNK
    nc_lat = T // CHUNK
    ctx_base = (B * T) // CHUNK

    def chunk_index(b, d, s):
        ctx_i = s if d == 0 else nc_ctx - 1 - s
        lat_i = s - nc_ctx if d == 0 else nc_lat - 1 - (s - nc_ctx)
        return jnp.where(s < nc_ctx, ctx_base + b * nc_ctx + ctx_i, b * nc_lat + lat_i)

    def spec(d, col):
        return pl.BlockSpec((CHUNK, D), lambda b, s: (chunk_index(b, d, s), col))

    in_specs, args = [], []
    for d in range(2):
        in_specs += [spec(d, d), spec(d, d), spec(d, 0), spec(d, 0), spec(d, 0), spec(d, 0)]
        args += [lw, ad, kf, kk, v, r]
    return pl.pallas_call(
        _scan_body, grid=(B, nc_ctx + nc_lat),
        in_specs=in_specs + [pl.BlockSpec((1, D), lambda b, s: (0, 0))],
        out_specs=[spec(0, 0), spec(1, 0)],
        out_shape=[jax.ShapeDtypeStruct((lay.n, D), F32), jax.ShapeDtypeStruct((lay.n, D), F32)],
        scratch_shapes=[pltpu.VMEM((2, D // CHUNK, CHUNK, CHUNK), F32)],
        compiler_params=pltpu.CompilerParams(dimension_semantics=("parallel", "arbitrary"),
                                             vmem_limit_bytes=VMEM_LIMIT),
        name="wkv_scan",
    )(*args, k_a)


def _rwkv_post_body(yf_ref, yr_ref, r_ref, kf_ref, ad_ref, v_ref, g_ref, x_ref, mod_ref,
                    ka_ref, rk_ref, lnw_ref, lnb_ref, ones_ref, wo_ref, o_ref):
    D = x_ref.shape[1]
    ones4 = ones_ref[...]
    y = yf_ref[...] + yr_ref[...]
    inv_n = 1.0 / RWKV_HEAD
    yc = y - _head_sum(y, ones4) * inv_n
    var = _head_sum(yc * yc, ones4) * inv_n
    yn = yc * lax.rsqrt(var + LN_X_EPS) * lnw_ref[...] + lnb_ref[...]
    kf = kf_ref[...]
    ad = ad_ref[...]
    ka = ka_ref[...]
    k_bonus = 0.5 * (kf * (1.0 + (ad[:, :D] - 1.0) * ka) + kf * (1.0 + (ad[:, D:] - 1.0) * ka))
    bonus = _head_sum(r_ref[...] * k_bonus * rk_ref[...], ones4) * v_ref[...]
    o = ((yn + bonus) * g_ref[...]).astype(BF16)
    o_ref[...] = x_ref[...] + mod_ref[2:3, :] * _dot(o, wo_ref[...])


def _rope_tables(T, tm):
    pos = jnp.arange(T)
    inv_freq = 1.0 / (ROPE_BASE ** (jnp.arange(AXIS_PAIRS, dtype=F32) / AXIS_PAIRS))
    ang_r = (pos // GRID_W).astype(F32)[:, None] * inv_freq[None, :]
    ang_c = (pos % GRID_W).astype(F32)[:, None] * inv_freq[None, :]
    pad1 = jnp.ones((T, LANES - QK_ROPE), F32)
    pad0 = jnp.zeros((T, LANES - QK_ROPE), F32)
    cos = jnp.concatenate([jnp.cos(ang_r), jnp.cos(ang_r), jnp.cos(ang_c), jnp.cos(ang_c), pad1], axis=1)
    sin = jnp.concatenate([-jnp.sin(ang_r), jnp.sin(ang_r), -jnp.sin(ang_c), jnp.sin(ang_c), pad0], axis=1)
    cos = jnp.concatenate([cos, jnp.ones((tm, LANES), F32)], axis=0)
    sin = jnp.concatenate([sin, jnp.zeros((tm, LANES), F32)], axis=0)
    return cos, sin


def _pad_cols(w, width):
    return jnp.pad(w, ((0, 0), (0, width - w.shape[1])))


def _pad_rows(w, height):
    return jnp.pad(w, ((0, height - w.shape[0]), (0, 0)))


def _block_diag2(w0, w1):
    z0 = jnp.zeros_like(w0)
    z1 = jnp.zeros_like(w1)
    return jnp.concatenate([jnp.concatenate([w0, z1], axis=1), jnp.concatenate([z0, w1], axis=1)], axis=0)


def kernel(x, c, ctx, c_ctx, ada_w, ada_b, norm1_g, norm2_g, mla_wqa, mla_qa_norm, mla_wqb, mla_wkva, mla_kva_norm, mla_wkvb, mla_q_norm, mla_k_norm, mla_wo, rwkv_mix, rwkv_wr, rwkv_wk, rwkv_wv, rwkv_wo, rwkv_w0, rwkv_w1, rwkv_w2, rwkv_a0, rwkv_a1, rwkv_a2, rwkv_g1, rwkv_g2, rwkv_k_k, rwkv_k_a, rwkv_r_k, rwkv_ln_w, rwkv_ln_b, rwkv_v0, rwkv_v1, rwkv_v2, ffn_w1, ffn_w3, ffn_w2, moe_router, moe_w1, moe_w3, moe_w2):
    B, T, D = x.shape
    L = ctx.shape[1]
    depth = ada_w.shape[0]
    lay = _Layout(B, T, L, D)
    n = lay.n
    tm = min(512, T)
    tm_small = min(256, T)
    assert T % tm == 0 and (B * L) % tm == 0 and T % GRID_W == 0 and L % CHUNK == 0 and B + 1 <= 16
    assert T & (T - 1) == 0 and L & (L - 1) == 0 and L % tm_small == 0

    ffn_b = [w.astype(BF16) for w in (ffn_w1, ffn_w3, ffn_w2)]
    moe_b = [w.astype(BF16) for w in (moe_w1, moe_w3, moe_w2)]
    xf = jnp.concatenate([x.reshape(B * T, D), ctx.reshape(B * L, D)], axis=0)
    cc = jnp.concatenate([c, c_ctx[None, :], jnp.zeros((16 - B - 1, D), F32)], axis=0)
    modall = _ada_call(cc, ada_w, ada_b)[:, :B + 1].reshape(depth, B + 1, 6, D)
    cos_t, sin_t = _rope_tables(T, tm)
    ones4 = jnp.kron(jnp.eye(4, dtype=F32), jnp.ones((RWKV_HEAD, RWKV_HEAD), F32)).astype(BF16)
    row1 = lambda z: z.reshape(1, -1)

    v_first = None
    for i in range(depth):
        j = i // 2
        modtab = modall[i]
        mod_t = (modtab, lay.mod_spec(tm))
        mod_s = (modtab, lay.mod_spec(tm_small))
        if i % 2 == 0:
            wa = jnp.concatenate([mla_wqa[j], _pad_cols(mla_wkva[j], KV_LORA + LANES)], axis=1).astype(BF16)
            wqb = mla_wqb[j].reshape(Q_LORA, MLA_HEADS, QK_HEAD)
            wqb = jnp.pad(wqb, ((0, 0), (0, 0), (0, QK_PAD - QK_HEAD))).reshape(Q_LORA, MLA_HEADS * QK_PAD).astype(BF16)
            wkvb = mla_wkvb[j].reshape(KV_LORA, MLA_HEADS, QK_NOPE + V_HEAD)
            wk = wkvb[:, :, :QK_NOPE].reshape(KV_LORA, MLA_HEADS * QK_NOPE).astype(BF16)
            wv = wkvb[:, :, QK_NOPE:].reshape(KV_LORA, MLA_HEADS * V_HEAD).astype(BF16)
            gq = _pad_cols(row1(mla_q_norm[j]) * SM_SCALE, QK_PAD)
            gk = _pad_cols(row1(mla_k_norm[j]), QK_PAD)
            cq, ckv, kr = _rowwise(
                _mla_a_body, "mla_a", n, tm, [xf], [mod_t],
                [row1(norm1_g[i]), wa, row1(mla_qa_norm[j]), row1(mla_kva_norm[j])],
                [(Q_LORA, BF16), (KV_LORA, BF16), (LANES, F32)])
            rope_in = [(cos_t, lay.rope_spec(tm)), (sin_t, lay.rope_spec(tm))]
            (q,) = _rowwise(_mla_q_body, "mla_q", n, tm, [cq], rope_in, [wqb, gq], [(MLA_HEADS * QK_PAD, BF16)])
            k, v = _rowwise(_mla_kv_body, "mla_kv", n, tm, [ckv, kr], rope_in, [wk, wv, gk],
                            [(MLA_HEADS * QK_PAD, BF16), (MLA_HEADS * V_PAD, BF16)])
            o = _attention(q, k, v, lay, tq=min(2048, T))
            (xf,) = _rowwise(_proj_resid_body, "mla_out", n, tm, [o, xf], [mod_t], [mla_wo[j].astype(BF16)], [(D, F32)])
        else:
            vres = j > 0
            mix = _pad_rows(rwkv_mix[j], 8)
            w1c = jnp.concatenate([rwkv_w1[j, 0], rwkv_w1[j, 1]], axis=1).astype(BF16)
            a1c = jnp.concatenate([rwkv_a1[j, 0], rwkv_a1[j, 1]], axis=1).astype(BF16)
            g1p = _pad_cols(rwkv_g1[j], 2 * LANES).astype(BF16)
            w2c = _block_diag2(rwkv_w2[j, 0], rwkv_w2[j, 1]).astype(BF16)
            a2c = _block_diag2(rwkv_a2[j, 0], rwkv_a2[j, 1]).astype(BF16)
            g2p = _pad_rows(rwkv_g2[j], 2 * LANES).astype(BF16)
            consts = [row1(norm1_g[i]), mix, rwkv_wr[j].astype(BF16), rwkv_wk[j].astype(BF16),
                      rwkv_wv[j].astype(BF16), w1c, a1c, g1p, w2c, a2c, g2p,
                      rwkv_w0[j].reshape(1, 2 * D), rwkv_a0[j].reshape(1, 2 * D), row1(rwkv_k_k[j]), ones4]
            rows = [xf]
            if vres:
                rows.append(v_first)
                consts += [_pad_cols(rwkv_v1[j - 1], LANES).astype(BF16), _pad_rows(rwkv_v2[j - 1], LANES).astype(BF16),
                           row1(rwkv_v0[j - 1])]
            halo_per_tile = tm_small // HALO
            before = pl.BlockSpec((HALO, D), lambda t: (jnp.maximum(t * halo_per_tile - 1, 0), 0))
            after = pl.BlockSpec((HALO, D), lambda t: (jnp.minimum((t + 1) * halo_per_tile, n // HALO - 1), 0))
            r, kf, lw, ad, kk, v_out, gate = _rowwise(
                functools.partial(_rwkv_feat_body, vres=vres, lay=lay), "rwkv_feat", n, tm_small,
                rows, [(xf, before), (xf, after), mod_s], consts,
                [(D, F32), (D, F32), (2 * D, F32), (2 * D, F32), (D, F32), (D, F32), (D, F32)])
            if j == 0:
                v_first = v_out
            k_a = row1(rwkv_k_a[j])
            y_fwd, y_rev = _scan(lw, ad, kf, kk, v_out, r, k_a, lay)
            (xf,) = _rowwise(
                _rwkv_post_body, "rwkv_out", n, tm_small, [y_fwd, y_rev, r, kf, ad, v_out, gate, xf], [mod_s],
                [k_a, row1(rwkv_r_k[j]), row1(rwkv_ln_w[j]), row1(rwkv_ln_b[j]), ones4, rwkv_wo[j].astype(BF16)],
                [(D, F32)])

        if i % 2 == 0:
            (xf,) = _rowwise(functools.partial(_ffn_body, tf=ffn_w1.shape[2] // 2), "dense_ffn", n, tm, [xf], [mod_t],
                             [row1(norm2_g[i]), ffn_b[0][j], ffn_b[1][j], ffn_b[2][j]], [(D, F32)])
        else:
            h, route = _rowwise(_router_body, "router", n, tm, [xf], [mod_t],
                                [row1(norm2_g[i]), _pad_cols(moe_router[j], LANES)], [(D, F32), (LANES, F32)])
            pos1, pos2, src, tile_expert, n_used = _route_plan(route, n, MOE_TM)
            ys = _gffn(tile_expert, n_used, src, h, moe_b[0], moe_b[1], moe_b[2], j, MOE_TM, tf=moe_w1.shape[3] // 4)
            n_out = lay.n_lat if i == depth - 1 else n
            xf = _moe_combine(pos1, pos2, ys, route, xf, modtab, lay, tm, n_out)
    return xf[:B * T].reshape(B, T, D)
```

```python
import functools

import jax
import jax.numpy as jnp
from jax import lax
from jax.experimental import pallas as pl
from jax.experimental.pallas import tpu as pltpu

F32 = jnp.float32
BF16 = jnp.bfloat16
HIGHEST = lax.Precision.HIGHEST

GRID_W = 64
NORM_EPS = 1e-6
MLA_HEADS = 8
QK_NOPE = 128
QK_ROPE = 64
V_HEAD = 128
QK_HEAD = QK_NOPE + QK_ROPE
Q_LORA = 384
KV_LORA = 256
ROPE_BASE = 10000.0
AXIS_PAIRS = QK_ROPE // 4
SM_SCALE = QK_HEAD ** -0.5
RWKV_HEAD = 64
LN_X_EPS = 64e-5
DECAY_SCALE = 0.6065306597126334
N_EXPERTS = 8

LANES = 128
QK_PAD = 2 * LANES
V_PAD = 2 * LANES
CHUNK = 64
MOE_TM = 512
IDX_ROW = 1024
ATTN_SUB = 512
DMA_UNROLL = 8
HALO = 8
VMEM_LIMIT = 56 * 1024 * 1024

_NT = (((1,), (1,)), ((), ()))
_TN = (((0,), (0,)), ((), ()))


def _dot(a, b):
    return jnp.dot(a, b, preferred_element_type=F32)


def _sigmoid(z):
    return 1.0 / (1.0 + jnp.exp(-z))


def _rms(z, width):
    return lax.rsqrt(jnp.sum(z * z, axis=-1, keepdims=True) * (1.0 / width) + NORM_EPS)


def _norm_mod(x, g, shift, scale):
    return (x * _rms(x, x.shape[-1])) * g * (1.0 + scale) + shift


def _rope(z, cos, sin):
    lane = lax.broadcasted_iota(jnp.int32, z.shape, 1)
    partner = jnp.where((lane & 16) == 0, pltpu.roll(z, LANES - 16, axis=1), pltpu.roll(z, 16, axis=1))
    return z * cos + partner * sin


def _head_sum(z, ones4):
    outs = []
    for q in range(z.shape[1] // 256):
        blk = z[:, 256 * q:256 * (q + 1)]
        hi = blk.astype(BF16)
        lo = (blk - hi.astype(F32)).astype(BF16)
        outs.append(_dot(hi, ones4) + _dot(lo, ones4))
    return jnp.concatenate(outs, axis=1)


class _Layout:
    def __init__(self, B, T, L, D):
        self.B, self.T, self.L, self.D = B, T, L, D
        self.n_lat = B * T
        self.n = B * T + B * L

    def mod_spec(self, tm):
        n_lat_tiles = self.n_lat // tm
        T, B, D = self.T, self.B, self.D
        return pl.BlockSpec((None, 6, D), lambda i: (jnp.where(i < n_lat_tiles, (i * tm) // T, B), 0, 0))

    def rope_spec(self, tm):
        n_lat_tiles = self.n_lat // tm
        tpb = self.T // tm
        return pl.BlockSpec((tm, LANES), lambda i: (jnp.where(i < n_lat_tiles, i % tpb, tpb), 0))


def _const_spec(arr):
    nd = arr.ndim
    return pl.BlockSpec(arr.shape, lambda i: (0,) * nd, pipeline_mode=pl.Buffered(1))


def _rowwise(body, name, n_rows, tm, row_ins, tile_ins, const_ins, outs):
    in_specs = [pl.BlockSpec((tm, a.shape[1]), lambda i: (i, 0)) for a in row_ins]
    in_specs += [spec for (_, spec) in tile_ins]
    in_specs += [_const_spec(a) for a in const_ins]
    out_specs = [pl.BlockSpec((tm, w), lambda i: (i, 0)) for (w, _) in outs]
    out_shape = [jax.ShapeDtypeStruct((n_rows, w), dt) for (w, dt) in outs]
    return pl.pallas_call(
        body, grid=(n_rows // tm,), in_specs=in_specs, out_specs=out_specs, out_shape=out_shape,
        compiler_params=pltpu.CompilerParams(dimension_semantics=("parallel",), vmem_limit_bytes=VMEM_LIMIT),
        name=name,
    )(*row_ins, *[a for (a, _) in tile_ins], *const_ins)


def _ada_body(c_ref, w_ref, b_ref, o_ref):
    c = c_ref[...]
    o_ref[...] = jnp.dot(c * _sigmoid(c), w_ref[...], precision=HIGHEST, preferred_element_type=F32) + b_ref[...]


def _ada_call(cc, ada_w, ada_b):
    depth, D, W = ada_w.shape
    tn = 1024
    return pl.pallas_call(
        _ada_body, grid=(depth, W // tn),
        in_specs=[pl.BlockSpec(cc.shape, lambda l, j: (0, 0)),
                  pl.BlockSpec((None, D, tn), lambda l, j: (l, 0, j)),
                  pl.BlockSpec((None, 1, tn), lambda l, j: (l, 0, j))],
        out_specs=pl.BlockSpec((None, cc.shape[0], tn), lambda l, j: (l, 0, j)),
        out_shape=jax.ShapeDtypeStruct((depth, cc.shape[0], W), F32),
        compiler_params=pltpu.CompilerParams(dimension_semantics=("parallel", "parallel"), vmem_limit_bytes=VMEM_LIMIT),
        name="ada",
    )(cc, ada_w, ada_b.reshape(depth, 1, W))


def _mla_a_body(x_ref, mod_ref, g_ref, wa_ref, qan_ref, kvn_ref, cq_ref, ckv_ref, kr_ref):
    h = _norm_mod(x_ref[...], g_ref[...], mod_ref[0:1, :], mod_ref[1:2, :]).astype(BF16)
    acc = _dot(h, wa_ref[...])
    qa = acc[:, 0:Q_LORA]
    kv = acc[:, Q_LORA:Q_LORA + KV_LORA]
    cq_ref[...] = (qa * _rms(qa, Q_LORA) * qan_ref[...]).astype(BF16)
    ckv_ref[...] = (kv * _rms(kv, KV_LORA) * kvn_ref[...]).astype(BF16)
    kr_ref[...] = acc[:, Q_LORA + KV_LORA:]


def _mla_q_body(cq_ref, cos_ref, sin_ref, wqb_ref, gq_ref, q_ref):
    acc = _dot(cq_ref[...], wqb_ref[...])
    cos = cos_ref[...]
    sin = sin_ref[...]
    g = gq_ref[...]
    for h in range(MLA_HEADS):
        a = acc[:, QK_PAD * h:QK_PAD * (h + 1)]
        an = a * _rms(a, QK_HEAD) * g
        q_ref[:, QK_PAD * h:QK_PAD * h + LANES] = an[:, :LANES].astype(BF16)
        q_ref[:, QK_PAD * h + LANES:QK_PAD * (h + 1)] = _rope(an[:, LANES:], cos, sin).astype(BF16)


def _mla_kv_body(ckv_ref, kr_ref, cos_ref, sin_ref, wk_ref, wv_ref, gk_ref, k_ref, v_ref):
    ckv = ckv_ref[...]
    kn = _dot(ckv, wk_ref[...])
    vals = _dot(ckv, wv_ref[...]).astype(BF16)
    lane = lax.broadcasted_iota(jnp.int32, (vals.shape[0], LANES), 1)
    ones_col = jnp.where(lane == 0, 1.0, 0.0).astype(BF16)
    for h in range(MLA_HEADS):
        v_ref[:, V_PAD * h:V_PAD * h + V_HEAD] = vals[:, V_HEAD * h:V_HEAD * (h + 1)]
        v_ref[:, V_PAD * h + V_HEAD:V_PAD * (h + 1)] = ones_col
    kr = kr_ref[...]
    ss_rope = jnp.sum(kr * kr, axis=-1, keepdims=True)
    g = gk_ref[...]
    g_nope = g[:, :LANES]
    kr_rot = _rope(kr * g[:, LANES:], cos_ref[...], sin_ref[...])
    for h in range(MLA_HEADS):
        a = kn[:, LANES * h:LANES * (h + 1)]
        inv = lax.rsqrt((jnp.sum(a * a, axis=-1, keepdims=True) + ss_rope) * (1.0 / QK_HEAD) + NORM_EPS)
        k_ref[:, QK_PAD * h:QK_PAD * h + LANES] = (a * inv * g_nope).astype(BF16)
        k_ref[:, QK_PAD * h + LANES:QK_PAD * (h + 1)] = (kr_rot * inv).astype(BF16)


def _attn_body(*refs, nseg, sub):
    q_ref = refs[0]
    o_ref = refs[-1]
    n_sub = q_ref.shape[0] // sub

    def scores(j):
        q = q_ref[sub * j:sub * (j + 1), :]
        return [lax.dot_general(q, refs[1 + 2 * i][...], _NT, preferred_element_type=F32) for i in range(nseg)]

    def finish(j, ss):
        m = functools.reduce(jnp.maximum, [jnp.max(s, axis=-1, keepdims=True) for s in ss])
        acc = functools.reduce(lambda a, b: a + b,
                               [_dot(jnp.exp((ss[i] - m).astype(BF16)), refs[2 + 2 * i][...]) for i in range(nseg)])
        o_ref[sub * j:sub * (j + 1), :] = (acc[:, :V_HEAD] / acc[:, V_HEAD:V_HEAD + 1]).astype(o_ref.dtype)

    pending = scores(0)
    for j in range(n_sub):
        following = scores(j + 1) if j + 1 < n_sub else None
        finish(j, pending)
        pending = following


def _attention(q, k, v, lay, tq):
    B, T, L = lay.B, lay.T, lay.L
    nq = T // tq
    ctx0 = (B * T) // L
    params = pltpu.CompilerParams(dimension_semantics=("parallel", "parallel", "arbitrary"),
                                  vmem_limit_bytes=VMEM_LIMIT)
    o_lat = pl.pallas_call(
        functools.partial(_attn_body, nseg=2, sub=min(ATTN_SUB, tq)), grid=(B, MLA_HEADS, nq),
        in_specs=[pl.BlockSpec((tq, QK_PAD), lambda b, h, i: (b * nq + i, h)),
                  pl.BlockSpec((L, QK_PAD), lambda b, h, i: (ctx0 + b, h)),
                  pl.BlockSpec((L, V_PAD), lambda b, h, i: (ctx0 + b, h)),
                  pl.BlockSpec((T, QK_PAD), lambda b, h, i: (b, h)),
                  pl.BlockSpec((T, V_PAD), lambda b, h, i: (b, h))],
        out_specs=pl.BlockSpec((tq, V_HEAD), lambda b, h, i: (b * nq + i, h)),
        out_shape=jax.ShapeDtypeStruct((B * T, MLA_HEADS * V_HEAD), BF16),
        compiler_params=params, name="attn_latent",
    )(q, k, v, k, v)
    o_ctx = pl.pallas_call(
        functools.partial(_attn_body, nseg=1, sub=min(ATTN_SUB, L)), grid=(B, MLA_HEADS, 1),
        in_specs=[pl.BlockSpec((L, QK_PAD), lambda b, h, i: (ctx0 + b, h)),
                  pl.BlockSpec((L, QK_PAD), lambda b, h, i: (ctx0 + b, h)),
                  pl.BlockSpec((L, V_PAD), lambda b, h, i: (ctx0 + b, h))],
        out_specs=pl.BlockSpec((L, V_HEAD), lambda b, h, i: (b, h)),
        out_shape=jax.ShapeDtypeStruct((B * L, MLA_HEADS * V_HEAD), BF16),
        compiler_params=params, name="attn_ctx",
    )(q, k, v)
    return jnp.concatenate([o_lat, o_ctx], axis=0)


def _proj_resid_body(a_ref, x_ref, mod_ref, w_ref, o_ref):
    o_ref[...] = x_ref[...] + mod_ref[2:3, :] * _dot(a_ref[...], w_ref[...])


def _router_body(x_ref, mod_ref, g_ref, router_ref, h_ref, route_ref):
    hf = _norm_mod(x_ref[...], g_ref[...], mod_ref[3:4, :], mod_ref[4:5, :])
    h_ref[...] = hf
    logits = jnp.dot(hf, router_ref[...], precision=HIGHEST, preferred_element_type=F32)
    lane = lax.broadcasted_iota(jnp.int32, logits.shape, 1)
    lanef = lane.astype(F32)
    neg = jnp.float32(-1e30)
    lg = jnp.where(lane < N_EXPERTS, logits, neg)
    m1 = jnp.max(lg, axis=-1, keepdims=True)
    i1 = jnp.min(jnp.where(lg == m1, lanef, float(LANES)), axis=-1, keepdims=True)
    lg2 = jnp.where(lanef == i1, neg, lg)
    m2 = jnp.max(lg2, axis=-1, keepdims=True)
    i2 = jnp.min(jnp.where(lg2 == m2, lanef, float(LANES)), axis=-1, keepdims=True)
    e2 = jnp.exp(m2 - m1)
    w1 = 1.0 / (1.0 + e2)
    w2 = e2 / (1.0 + e2)
    route_ref[...] = jnp.where(lane == 0, i1, jnp.where(lane == 1, i2, jnp.where(lane == 2, w1, jnp.where(lane == 3, w2, 0.0))))


def _route_plan(route, n, tm_g):
    i1 = route[:, 0].astype(jnp.int32)
    i2 = route[:, 1].astype(jnp.int32)
    experts = jnp.arange(N_EXPERTS, dtype=jnp.int32)
    onehot = ((i1[:, None] == experts) | (i2[:, None] == experts)).astype(jnp.int32)
    rank = jnp.cumsum(onehot, axis=0) - onehot
    tiles_e = (jnp.sum(onehot, axis=0) + tm_g - 1) // tm_g
    tile_end = jnp.cumsum(tiles_e)
    start = (tile_end - tiles_e) * tm_g
    pos1 = start[i1] + jnp.take_along_axis(rank, i1[:, None], axis=1)[:, 0]
    pos2 = start[i2] + jnp.take_along_axis(rank, i2[:, None], axis=1)[:, 0]
    n_tiles = -(-2 * n // tm_g) + N_EXPERTS
    tile_ids = jnp.arange(n_tiles, dtype=jnp.int32)
    tile_expert = jnp.minimum(jnp.sum((tile_ids[:, None] >= tile_end[None, :]).astype(jnp.int32), axis=1), N_EXPERTS - 1)
    tok = jnp.arange(n, dtype=jnp.int32)
    src = jnp.zeros((n_tiles * tm_g,), jnp.int32).at[jnp.concatenate([pos1, pos2])].set(
        jnp.concatenate([tok, tok]), unique_indices=True, mode="promise_in_bounds")
    src = _pad_cols(src.reshape(n_tiles, tm_g), IDX_ROW)
    return pos1, pos2, src, tile_expert.astype(jnp.int32), tile_end[-1:].astype(jnp.int32)


def _row_copy(src, src_row, dst, dst_row, sem):
    return pltpu.make_async_copy(src.at[pl.ds(src_row, 1), :], dst.at[pl.ds(dst_row, 1), :], sem)


def _gffn_body(te_ref, nu_ref, src_hbm, h_hbm, w1_ref, w3_ref, w2_ref, ys_ref,
               idx, xg, acc_ref, isem, dsem, *, tm, tf):
    del te_ref
    m = pl.program_id(0)
    n_m = pl.num_programs(0)
    slot = m & 1
    other = 1 - slot
    n_used = nu_ref[0]
    used = m < n_used
    nf = w1_ref.shape[1] // tf
    per_f = tm // nf

    def fetch_idx(tile, s):
        fetch = pltpu.make_async_copy(src_hbm.at[tile], idx.at[s], isem)
        fetch.start()
        fetch.wait()

    def wait_slot(s):
        pltpu.make_async_copy(xg.at[s], xg.at[s], dsem.at[s]).wait()

    @pl.when(m == 0)
    def _():
        fetch_idx(0, 0)

        def issue(t, carry):
            _row_copy(h_hbm, idx[0, t], xg.at[0], t, dsem.at[0]).start()
            return carry

        lax.fori_loop(0, tm, issue, 0, unroll=DMA_UNROLL)

    @pl.when(m <= n_used)
    def _():
        wait_slot(slot)

    @pl.when(used)
    def _():
        fetch_idx(jnp.minimum(m + 1, n_m - 1), other)
        h = xg[slot].astype(BF16)
        for c in range(nf):
            for t in range(per_f * c, per_f * (c + 1)):
                _row_copy(h_hbm, idx[other, t], xg.at[other], t, dsem.at[other]).start()
            a1 = _dot(h, w1_ref[:, tf * c:tf * (c + 1)])
            a3 = _dot(h, w3_ref[:, tf * c:tf * (c + 1)])
            part = _dot((a1 * _sigmoid(a1) * a3).astype(BF16), w2_ref[tf * c:tf * (c + 1), :])
            if c == 0:
                acc_ref[...] = part
            else:
                acc_ref[...] += part
        ys_ref[...] = acc_ref[...]

    @pl.when(jnp.logical_not(used))
    def _():
        ys_ref[...] = jnp.zeros_like(ys_ref)

    @pl.when(used & (m == n_m - 1))
    def _():
        wait_slot(other)


def _gffn(tile_expert, n_used, src, h, w1, w3, w2, layer, tm_g, tf):
    _, E, D, F = w1.shape
    n_tiles = src.shape[0]
    any_spec = pl.BlockSpec(memory_space=pl.ANY)
    resident = pl.Buffered(1)
    grid_spec = pltpu.PrefetchScalarGridSpec(
        num_scalar_prefetch=2, grid=(n_tiles,),
        in_specs=[any_spec, any_spec,
                  pl.BlockSpec((None, None, D, F), lambda m, te, nu: (layer, te[m], 0, 0), pipeline_mode=resident),
                  pl.BlockSpec((None, None, D, F), lambda m, te, nu: (layer, te[m], 0, 0), pipeline_mode=resident),
                  pl.BlockSpec((None, None, F, D), lambda m, te, nu: (layer, te[m], 0, 0), pipeline_mode=resident)],
        out_specs=pl.BlockSpec((tm_g, D), lambda m, te, nu: (m, 0)),
        scratch_shapes=[pltpu.SMEM((2, IDX_ROW), jnp.int32), pltpu.VMEM((2, tm_g, D), F32),
                        pltpu.VMEM((tm_g, D), F32),
                        pltpu.SemaphoreType.DMA, pltpu.SemaphoreType.DMA((2,))])
    return pl.pallas_call(
        functools.partial(_gffn_body, tm=tm_g, tf=tf), grid_spec=grid_spec,
        out_shape=jax.ShapeDtypeStruct((n_tiles * tm_g, D), F32),
        compiler_params=pltpu.CompilerParams(dimension_semantics=("arbitrary",),
                                             vmem_limit_bytes=VMEM_LIMIT),
        name="moe_gffn",
    )(tile_expert, n_used, src, h, w1, w3, w2)


def _moe_combine_body(pos_hbm, ys_hbm, route_ref, x_ref, mod_ref, o_ref, idx, yg, isem, dsem, *, tm):
    i = pl.program_id(0)
    slot = i & 1

    def start_gather(tile, s):
        fetch = pltpu.make_async_copy(pos_hbm.at[tile], idx.at[s], isem)
        fetch.start()
        fetch.wait()

        def issue(t, carry):
            _row_copy(ys_hbm, idx[s, t], yg.at[s, 0], t, dsem.at[s]).start()
            _row_copy(ys_hbm, idx[s, tm + t], yg.at[s, 1], t, dsem.at[s]).start()
            return carry

        lax.fori_loop(0, tm, issue, 0, unroll=DMA_UNROLL)

    @pl.when(i == 0)
    def _():
        start_gather(0, 0)

    pltpu.make_async_copy(yg.at[slot], yg.at[slot], dsem.at[slot]).wait()

    @pl.when(i + 1 < pl.num_programs(0))
    def _():
        start_gather(i + 1, 1 - slot)

    route = route_ref[...]
    y = route[:, 2:3] * yg[slot, 0] + route[:, 3:4] * yg[slot, 1]
    o_ref[...] = x_ref[...] + mod_ref[5:6, :] * y


def _moe_combine(pos1, pos2, ys, route, xf, modtab, lay, tm, n_out):
    n, D = xf.shape
    pos = jnp.concatenate([pos1.reshape(n // tm, tm), pos2.reshape(n // tm, tm)], axis=1)
    any_spec = pl.BlockSpec(memory_space=pl.ANY)
    row = lambda i: (i, 0)
    return pl.pallas_call(
        functools.partial(_moe_combine_body, tm=tm), grid=(n_out // tm,),
        in_specs=[any_spec, any_spec, pl.BlockSpec((tm, LANES), row), pl.BlockSpec((tm, D), row), lay.mod_spec(tm)],
        out_specs=pl.BlockSpec((tm, D), row),
        out_shape=jax.ShapeDtypeStruct((n_out, D), F32),
        scratch_shapes=[pltpu.SMEM((2, 2 * tm), jnp.int32), pltpu.VMEM((2, 2, tm, D), F32),
                        pltpu.SemaphoreType.DMA, pltpu.SemaphoreType.DMA((2,))],
        compiler_params=pltpu.CompilerParams(dimension_semantics=("arbitrary",), vmem_limit_bytes=VMEM_LIMIT),
        name="moe_combine",
    )(pos, ys, route, xf, modtab)


def _ffn_body(x_ref, mod_ref, g_ref, w1_ref, w3_ref, w2_ref, o_ref, *, tf):
    x = x_ref[...]
    h = _norm_mod(x, g_ref[...], mod_ref[3:4, :], mod_ref[4:5, :]).astype(BF16)
    acc = None
    for c in range(w1_ref.shape[1] // tf):
        a1 = _dot(h, w1_ref[:, tf * c:tf * (c + 1)])
        a3 = _dot(h, w3_ref[:, tf * c:tf * (c + 1)])
        part = _dot((a1 * _sigmoid(a1) * a3).astype(BF16), w2_ref[tf * c:tf * (c + 1), :])
        acc = part if acc is None else acc + part
    o_ref[...] = x + mod_ref[5:6, :] * acc


def _rwkv_feat_body(*refs, vres, lay):
    if vres:
        (x_ref, vf_ref, xp_ref, xn_ref, mod_ref, g_ref, mix_ref, wr_ref, wk_ref, wv_ref, w1_ref, a1_ref, g1_ref,
         w2_ref, a2_ref, g2_ref, w0_ref, a0_ref, kkw_ref, ones_ref, v1_ref, v2_ref, v0_ref,
         r_ref, k_ref, lw_ref, ad_ref, kk_ref, vo_ref, gate_ref) = refs
    else:
        (x_ref, xp_ref, xn_ref, mod_ref, g_ref, mix_ref, wr_ref, wk_ref, wv_ref, w1_ref, a1_ref, g1_ref,
         w2_ref, a2_ref, g2_ref, w0_ref, a0_ref, kkw_ref, ones_ref,
         r_ref, k_ref, lw_ref, ad_ref, kk_ref, vo_ref, gate_ref) = refs
    tm = x_ref.shape[0]
    g = g_ref[...]
    shift = mod_ref[0:1, :]
    scale = mod_ref[1:2, :]
    h = _norm_mod(x_ref[...], g, shift, scale)
    h_before = _norm_mod(xp_ref[...], g, shift, scale)[HALO - 1:HALO, :]
    h_after = _norm_mod(xn_ref[...], g, shift, scale)[0:1, :]
    row = lax.broadcasted_iota(jnp.int32, (tm, 1), 0)
    gidx = pl.program_id(0) * tm + row
    in_lat = gidx < lay.n_lat
    pos = jnp.where(in_lat, gidx & (lay.T - 1), (gidx - lay.n_lat) & (lay.L - 1))
    last_pos = jnp.where(in_lat, lay.T - 1, lay.L - 1)
    prev = jnp.where(row == 0, h_before, pltpu.roll(h, 1, axis=0))
    prev = jnp.where(pos == 0, 0.0, prev)
    nxt = jnp.where(row == tm - 1, h_after, pltpu.roll(h, tm - 1, axis=0))
    nxt = jnp.where(pos == last_pos, 0.0, nxt)
    xx = 0.5 * (prev + nxt) - h

    def mixed(j):
        return (h + xx * mix_ref[j:j + 1, :]).astype(BF16)

    r_ref[...] = _dot(mixed(0), wr_ref[...]).astype(r_ref.dtype)
    tw = jnp.tanh(_dot(mixed(1), w1_ref[...]))
    dec = w0_ref[...] + _dot(tw.astype(BF16), w2_ref[...])
    lw_ref[...] = -DECAY_SCALE * _sigmoid(dec)
    k = _dot(mixed(2), wk_ref[...])
    k_ref[...] = k
    kkr = k * kkw_ref[...]
    ss = _head_sum(kkr * kkr, ones_ref[...])
    kk_ref[...] = (kkr / jnp.maximum(jnp.sqrt(ss), 1e-12)).astype(kk_ref.dtype)
    xv = mixed(3)
    v = _dot(xv, wv_ref[...])
    if vres:
        yv = _dot(xv, v1_ref[...])
        v = v + (vf_ref[...] - v) * _sigmoid(v0_ref[...] + _dot(yv.astype(BF16), v2_ref[...]))
    vo_ref[...] = v
    ya = _dot(mixed(4), a1_ref[...])
    ad_ref[...] = _sigmoid(a0_ref[...] + _dot(ya.astype(BF16), a2_ref[...])).astype(ad_ref.dtype)
    sgate = _sigmoid(_dot(mixed(5), g1_ref[...]))
    gate_ref[...] = _dot(sgate.astype(BF16), g2_ref[...]).astype(gate_ref.dtype)


def _scan_body(*refs):
    dir_refs = (refs[0:6], refs[6:12])
    ka_ref = refs[12]
    y_refs = refs[13:15]
    s_ref = refs[15]
    C = CHUNK
    D = ka_ref.shape[1]
    n_heads = D // C

    @pl.when(pl.program_id(1) == 0)
    def _():
        s_ref[...] = jnp.zeros_like(s_ref)

    ri = lax.broadcasted_iota(jnp.int32, (C, C), 0)
    ci = lax.broadcasted_iota(jnp.int32, (C, C), 1)
    xr = ri ^ ci
    eye = (ri == ci).astype(F32)
    r2 = lax.broadcasted_iota(jnp.int32, (C, 2 * C), 0)
    c2 = lax.broadcasted_iota(jnp.int32, (C, 2 * C), 1) & (C - 1)
    zeros_cc = jnp.zeros((C, C), BF16)
    ka = ka_ref[...]

    ar, bk, vh, bkp, dph, strict, incl, where = [], [], [], [], [], [], [], []
    for d, (lw_ref, ad_ref, kf_ref, kk_ref, v_ref, r_ref) in enumerate(dir_refs):
        sg = 1 if d == 0 else -1
        strict_d = (r2 - c2) * sg > 0
        incl_d = (r2 - c2) * sg >= 0
        lw = lw_ref[...]
        ad = ad_ref[...].astype(F32)
        kk = kk_ref[...].astype(F32)
        c = jnp.dot(((ri - ci) * sg >= 0).astype(F32), lw, precision=HIGHEST, preferred_element_type=F32)
        pc = c[C - 1:C, :] if d == 0 else c[0:1, :]
        kd = kf_ref[...] * (1.0 + (ad - 1.0) * ka)
        b = kk * ad
        einv = jnp.exp(-c)
        a_t = (-kk * jnp.exp(c - lw)).astype(BF16)
        r_t = (r_ref[...].astype(F32) * jnp.exp(c)).astype(BF16)
        b_t = (b * einv).astype(BF16)
        k_t = (kd * einv).astype(BF16)
        epc = jnp.exp(pc - c)
        b_p = (b * epc).astype(BF16)
        k_p = (kd * epc).astype(BF16)
        d_p = jnp.exp(pc)
        vb = v_ref[...].astype(BF16)
        for h in range(n_heads):
            sl = slice(C * h, C * (h + 1))
            ar.append(jnp.concatenate([a_t[:, sl], r_t[:, sl]], axis=0))
            bk.append(jnp.concatenate([b_t[:, sl], k_t[:, sl]], axis=0))
            vh.append(vb[:, sl])
            bkp.append(jnp.concatenate([b_p[:, sl], k_p[:, sl]], axis=0))
            dph.append(d_p[:, sl])
            strict.append(strict_d)
            incl.append(incl_d)
            where.append((d, h, sl))

    items = range(len(where))
    sc = [lax.dot_general(ar[i], bk[i], _NT, preferred_element_type=F32) for i in items]
    st = [s_ref[d, h] for (d, h, _) in where]
    ars = [lax.dot_general(ar[i], st[i].astype(BF16), _NT, preferred_element_type=F32) for i in items]
    top = [jnp.where(strict[i], sc[i][0:C], 0.0) for i in items]
    bot = [jnp.where(incl[i], sc[i][C:], 0.0).astype(BF16) for i in items]
    lab = [top[i][:, 0:C] for i in items]
    x = [ars[i][0:C] + _dot(top[i].astype(BF16), jnp.concatenate([zeros_cc, vh[i]], axis=0)) for i in items]
    t = [eye + jnp.where(xr == 1, lab[i], 0.0) for i in items]
    for lb in range(1, 6):
        level = (xr >> lb) == 1
        tb = [t[i].astype(BF16) for i in items]
        ot = [_dot(jnp.where(level, lab[i], 0.0).astype(BF16), tb[i]).astype(BF16) for i in items]
        t = [t[i] + _dot(tb[i], ot[i]) for i in items]
    u = [_dot(t[i].astype(BF16), x[i].astype(BF16)) for i in items]
    uv = [jnp.concatenate([u[i].astype(BF16), vh[i]], axis=0) for i in items]
    for i, (d, h, sl) in enumerate(where):
        y_refs[d][:, sl] = ars[i][C:] + _dot(bot[i], uv[i])
    for i, (d, h, sl) in enumerate(where):
        s_ref[d, h] = st[i] * dph[i] + lax.dot_general(uv[i], bkp[i], _TN, preferred_element_type=F32)


def _scan(lw, ad, kf, kk, v, r, k_a, lay):
    B, T, L, D = lay.B, lay.T, lay.L, lay.D
    nc_ctx = L // CHUNK
    nc_lat = T // CHUNK
    ctx_base = (B * T) // CHUNK

    def chunk_index(b, d, s):
        ctx_i = s if d == 0 else nc_ctx - 1 - s
        lat_i = s - nc_ctx if d == 0 else nc_lat - 1 - (s - nc_ctx)
        return jnp.where(s < nc_ctx, ctx_base + b * nc_ctx + ctx_i, b * nc_lat + lat_i)

    def spec(d, col):
        return pl.BlockSpec((CHUNK, D), lambda b, s: (chunk_index(b, d, s), col))

    in_specs, args = [], []
    for d in range(2):
        in_specs += [spec(d, d), spec(d, d), spec(d, 0), spec(d, 0), spec(d, 0), spec(d, 0)]
        args += [lw, ad, kf, kk, v, r]
    return pl.pallas_call(
        _scan_body, grid=(B, nc_ctx + nc_lat),
        in_specs=in_specs + [pl.BlockSpec((1, D), lambda b, s: (0, 0))],
        out_specs=[spec(0, 0), spec(1, 0)],
        out_shape=[jax.ShapeDtypeStruct((lay.n, D), F32), jax.ShapeDtypeStruct((lay.n, D), F32)],
        scratch_shapes=[pltpu.VMEM((2, D // CHUNK, CHUNK, CHUNK), F32)],
        compiler_params=pltpu.CompilerParams(dimension_semantics=("parallel", "arbitrary"),
                                             vmem_limit_bytes=VMEM_LIMIT),
        name="wkv_scan",
    )(*args, k_a)


def _rwkv_post_body(yf_ref, yr_ref, r_ref, kf_ref, ad_ref, v_ref, g_ref, x_ref, mod_ref,
                    ka_ref, rk_ref, lnw_ref, lnb_ref, ones_ref, wo_ref, o_ref):
    D = x_ref.shape[1]
    ones4 = ones_ref[...]
    y = yf_ref[...] + yr_ref[...]
    inv_n = 1.0 / RWKV_HEAD
    yc = y - _head_sum(y, ones4) * inv_n
    var = _head_sum(yc * yc, ones4) * inv_n
    yn = yc * lax.rsqrt(var + LN_X_EPS) * lnw_ref[...] + lnb_ref[...]
    kf = kf_ref[...]
    ad = ad_ref[...].astype(F32)
    ka = ka_ref[...]
    k_bonus = 0.5 * (kf * (1.0 + (ad[:, :D] - 1.0) * ka) + kf * (1.0 + (ad[:, D:] - 1.0) * ka))
    bonus = _head_sum(r_ref[...].astype(F32) * k_bonus * rk_ref[...], ones4) * v_ref[...]
    o = ((yn + bonus) * g_ref[...].astype(F32)).astype(BF16)
    o_ref[...] = x_ref[...] + mod_ref[2:3, :] * _dot(o, wo_ref[...])


def _rope_tables(T, tm):
    pos = jnp.arange(T)
    inv_freq = 1.0 / (ROPE_BASE ** (jnp.arange(AXIS_PAIRS, dtype=F32) / AXIS_PAIRS))
    ang_r = (pos // GRID_W).astype(F32)[:, None] * inv_freq[None, :]
    ang_c = (pos % GRID_W).astype(F32)[:, None] * inv_freq[None, :]
    pad1 = jnp.ones((T, LANES - QK_ROPE), F32)
    pad0 = jnp.zeros((T, LANES - QK_ROPE), F32)
    cos = jnp.concatenate([jnp.cos(ang_r), jnp.cos(ang_r), jnp.cos(ang_c), jnp.cos(ang_c), pad1], axis=1)
    sin = jnp.concatenate([-jnp.sin(ang_r), jnp.sin(ang_r), -jnp.sin(ang_c), jnp.sin(ang_c), pad0], axis=1)
    cos = jnp.concatenate([cos, jnp.ones((tm, LANES), F32)], axis=0)
    sin = jnp.concatenate([sin, jnp.zeros((tm, LANES), F32)], axis=0)
    return cos, sin


def _pad_cols(w, width):
    return jnp.pad(w, ((0, 0), (0, width - w.shape[1])))


def _pad_rows(w, height):
    return jnp.pad(w, ((0, height - w.shape[0]), (0, 0)))


def _block_diag2(w0, w1):
    z0 = jnp.zeros_like(w0)
    z1 = jnp.zeros_like(w1)
    return jnp.concatenate([jnp.concatenate([w0, z1], axis=1), jnp.concatenate([z0, w1], axis=1)], axis=0)


def kernel(x, c, ctx, c_ctx, ada_w, ada_b, norm1_g, norm2_g, mla_wqa, mla_qa_norm, mla_wqb, mla_wkva, mla_kva_norm, mla_wkvb, mla_q_norm, mla_k_norm, mla_wo, rwkv_mix, rwkv_wr, rwkv_wk, rwkv_wv, rwkv_wo, rwkv_w0, rwkv_w1, rwkv_w2, rwkv_a0, rwkv_a1, rwkv_a2, rwkv_g1, rwkv_g2, rwkv_k_k, rwkv_k_a, rwkv_r_k, rwkv_ln_w, rwkv_ln_b, rwkv_v0, rwkv_v1, rwkv_v2, ffn_w1, ffn_w3, ffn_w2, moe_router, moe_w1, moe_w3, moe_w2):
    B, T, D = x.shape
    L = ctx.shape[1]
    depth = ada_w.shape[0]
    lay = _Layout(B, T, L, D)
    n = lay.n
    tm = min(512, T)
    tm_small = min(256, T)
    assert T % tm == 0 and (B * L) % tm == 0 and T % GRID_W == 0 and L % CHUNK == 0 and B + 1 <= 16
    assert T & (T - 1) == 0 and L & (L - 1) == 0 and L % tm_small == 0

    ffn_b = [w.astype(BF16) for w in (ffn_w1, ffn_w3, ffn_w2)]
    moe_b = [w.astype(BF16) for w in (moe_w1, moe_w3, moe_w2)]
    xf = jnp.concatenate([x.reshape(B * T, D), ctx.reshape(B * L, D)], axis=0)
    cc = jnp.concatenate([c, c_ctx[None, :], jnp.zeros((16 - B - 1, D), F32)], axis=0)
    modall = _ada_call(cc, ada_w, ada_b)[:, :B + 1].reshape(depth, B + 1, 6, D)
    cos_t, sin_t = _rope_tables(T, tm)
    ones4 = jnp.kron(jnp.eye(4, dtype=F32), jnp.ones((RWKV_HEAD, RWKV_HEAD), F32)).astype(BF16)
    row1 = lambda z: z.reshape(1, -1)

    v_first = None
    for i in range(depth):
        j = i // 2
        modtab = modall[i]
        mod_t = (modtab, lay.mod_spec(tm))
        mod_s = (modtab, lay.mod_spec(tm_small))
        if i % 2 == 0:
            wa = jnp.concatenate([mla_wqa[j], _pad_cols(mla_wkva[j], KV_LORA + LANES)], axis=1).astype(BF16)
            wqb = mla_wqb[j].reshape(Q_LORA, MLA_HEADS, QK_HEAD)
            wqb = jnp.pad(wqb, ((0, 0), (0, 0), (0, QK_PAD - QK_HEAD))).reshape(Q_LORA, MLA_HEADS * QK_PAD).astype(BF16)
            wkvb = mla_wkvb[j].reshape(KV_LORA, MLA_HEADS, QK_NOPE + V_HEAD)
            wk = wkvb[:, :, :QK_NOPE].reshape(KV_LORA, MLA_HEADS * QK_NOPE).astype(BF16)
            wv = wkvb[:, :, QK_NOPE:].reshape(KV_LORA, MLA_HEADS * V_HEAD).astype(BF16)
            gq = _pad_cols(row1(mla_q_norm[j]) * SM_SCALE, QK_PAD)
            gk = _pad_cols(row1(mla_k_norm[j]), QK_PAD)
            cq, ckv, kr = _rowwise(
                _mla_a_body, "mla_a", n, tm, [xf], [mod_t],
                [row1(norm1_g[i]), wa, row1(mla_qa_norm[j]), row1(mla_kva_norm[j])],
                [(Q_LORA, BF16), (KV_LORA, BF16), (LANES, F32)])
            rope_in = [(cos_t, lay.rope_spec(tm)), (sin_t, lay.rope_spec(tm))]
            (q,) = _rowwise(_mla_q_body, "mla_q", n, tm, [cq], rope_in, [wqb, gq], [(MLA_HEADS * QK_PAD, BF16)])
            k, v = _rowwise(_mla_kv_body, "mla_kv", n, tm, [ckv, kr], rope_in, [wk, wv, gk],
                            [(MLA_HEADS * QK_PAD, BF16), (MLA_HEADS * V_PAD, BF16)])
            o = _attention(q, k, v, lay, tq=min(2048, T))
            (xf,) = _rowwise(_proj_resid_body, "mla_out", n, tm, [o, xf], [mod_t], [mla_wo[j].astype(BF16)], [(D, F32)])
        else:
            vres = j > 0
            mix = _pad_rows(rwkv_mix[j], 8)
            w1c = jnp.concatenate([rwkv_w1[j, 0], rwkv_w1[j, 1]], axis=1).astype(BF16)
            a1c = jnp.concatenate([rwkv_a1[j, 0], rwkv_a1[j, 1]], axis=1).astype(BF16)
            g1p = _pad_cols(rwkv_g1[j], 2 * LANES).astype(BF16)
            w2c = _block_diag2(rwkv_w2[j, 0], rwkv_w2[j, 1]).astype(BF16)
            a2c = _block_diag2(rwkv_a2[j, 0], rwkv_a2[j, 1]).astype(BF16)
            g2p = _pad_rows(rwkv_g2[j], 2 * LANES).astype(BF16)
            consts = [row1(norm1_g[i]), mix, rwkv_wr[j].astype(BF16), rwkv_wk[j].astype(BF16),
                      rwkv_wv[j].astype(BF16), w1c, a1c, g1p, w2c, a2c, g2p,
                      rwkv_w0[j].reshape(1, 2 * D), rwkv_a0[j].reshape(1, 2 * D), row1(rwkv_k_k[j]), ones4]
            rows = [xf]
            if vres:
                rows.append(v_first)
                consts += [_pad_cols(rwkv_v1[j - 1], LANES).astype(BF16), _pad_rows(rwkv_v2[j - 1], LANES).astype(BF16),
                           row1(rwkv_v0[j - 1])]
            halo_per_tile = tm_small // HALO
            before = pl.BlockSpec((HALO, D), lambda t: (jnp.maximum(t * halo_per_tile - 1, 0), 0))
            after = pl.BlockSpec((HALO, D), lambda t: (jnp.minimum((t + 1) * halo_per_tile, n // HALO - 1), 0))
            r, kf, lw, ad, kk, v_out, gate = _rowwise(
                functools.partial(_rwkv_feat_body, vres=vres, lay=lay), "rwkv_feat", n, tm_small,
                rows, [(xf, before), (xf, after), mod_s], consts,
                [(D, BF16), (D, F32), (2 * D, F32), (2 * D, BF16), (D, BF16), (D, F32), (D, BF16)])
            if j == 0:
                v_first = v_out
            k_a = row1(rwkv_k_a[j])
            y_fwd, y_rev = _scan(lw, ad, kf, kk, v_out, r, k_a, lay)
            (xf,) = _rowwise(
                _rwkv_post_body, "rwkv_out", n, tm_small, [y_fwd, y_rev, r, kf, ad, v_out, gate, xf], [mod_s],
                [k_a, row1(rwkv_r_k[j]), row1(rwkv_ln_w[j]), row1(rwkv_ln_b[j]), ones4, rwkv_wo[j].astype(BF16)],
                [(D, F32)])

        if i % 2 == 0:
            (xf,) = _rowwise(functools.partial(_ffn_body, tf=ffn_w1.shape[2] // 2), "dense_ffn", n, tm, [xf], [mod_t],
                             [row1(norm2_g[i]), ffn_b[0][j], ffn_b[1][j], ffn_b[2][j]], [(D, F32)])
        else:
            h, route = _rowwise(_router_body, "router", n, tm, [xf], [mod_t],
                                [row1(norm2_g[i]), _pad_cols(moe_router[j], LANES)], [(D, F32), (LANES, F32)])
            pos1, pos2, src, tile_expert, n_used = _route_plan(route, n, MOE_TM)
            ys = _gffn(tile_expert, n_used, src, h, moe_b[0], moe_b[1], moe_b[2], j, MOE_TM, tf=moe_w1.shape[3] // 4)
            n_out = lay.n_lat if i == depth - 1 else n
            xf = _moe_combine(pos1, pos2, ys, route, xf, modtab, lay, tm, n_out)
    return xf[:B * T].reshape(B, T, D)
```

```python
import functools

import jax
import jax.numpy as jnp
from jax import lax
from jax.experimental import pallas as pl
from jax.experimental.pallas import tpu as pltpu

F32 = jnp.float32
BF16 = jnp.bfloat16
HIGHEST = lax.Precision.HIGHEST

GRID_W = 64
NORM_EPS = 1e-6
MLA_HEADS = 8
QK_NOPE = 128
QK_ROPE = 64
V_HEAD = 128
QK_HEAD = QK_NOPE + QK_ROPE
Q_LORA = 384
KV_LORA = 256
ROPE_BASE = 10000.0
AXIS_PAIRS = QK_ROPE // 4
SM_SCALE = QK_HEAD ** -0.5
RWKV_HEAD = 64
LN_X_EPS = 64e-5
DECAY_SCALE = 0.6065306597126334
N_EXPERTS = 8

LANES = 128
QK_PAD = 2 * LANES
V_PAD = 2 * LANES
CHUNK = 64
MOE_TM = 512
ATTN_SUB = 512
DMA_UNROLL = 8
HALO = 8
VMEM_LIMIT = 56 * 1024 * 1024

_NT = (((1,), (1,)), ((), ()))
_TN = (((0,), (0,)), ((), ()))


def _dot(a, b):
    return jnp.dot(a, b, preferred_element_type=F32)


def _sigmoid(z):
    return 1.0 / (1.0 + jnp.exp(-z))


def _rms(z, width):
    return lax.rsqrt(jnp.sum(z * z, axis=-1, keepdims=True) * (1.0 / width) + NORM_EPS)


def _norm_mod(x, g, shift, scale):
    return (x * _rms(x, x.shape[-1])) * g * (1.0 + scale) + shift


def _rope(z, cos, sin):
    lane = lax.broadcasted_iota(jnp.int32, z.shape, 1)
    partner = jnp.where((lane & 16) == 0, pltpu.roll(z, LANES - 16, axis=1), pltpu.roll(z, 16, axis=1))
    return z * cos + partner * sin


def _head_sum(z, ones4):
    outs = []
    for q in range(z.shape[1] // 256):
        blk = z[:, 256 * q:256 * (q + 1)]
        hi = blk.astype(BF16)
        lo = (blk - hi.astype(F32)).astype(BF16)
        outs.append(_dot(hi, ones4) + _dot(lo, ones4))
    return jnp.concatenate(outs, axis=1)


class _Layout:
    def __init__(self, B, T, L, D):
        self.B, self.T, self.L, self.D = B, T, L, D
        self.n_lat = B * T
        self.n = B * T + B * L

    def mod_spec(self, tm):
        n_lat_tiles = self.n_lat // tm
        T, B, D = self.T, self.B, self.D
        return pl.BlockSpec((None, 6, D), lambda i: (jnp.where(i < n_lat_tiles, (i * tm) // T, B), 0, 0))

    def rope_spec(self, tm):
        n_lat_tiles = self.n_lat // tm
        tpb = self.T // tm
        return pl.BlockSpec((tm, LANES), lambda i: (jnp.where(i < n_lat_tiles, i % tpb, tpb), 0))


def _const_spec(arr):
    nd = arr.ndim
    return pl.BlockSpec(arr.shape, lambda i: (0,) * nd, pipeline_mode=pl.Buffered(1))


def _rowwise(body, name, n_rows, tm, row_ins, tile_ins, const_ins, outs):
    in_specs = [pl.BlockSpec((tm, a.shape[1]), lambda i: (i, 0)) for a in row_ins]
    in_specs += [spec for (_, spec) in tile_ins]
    in_specs += [_const_spec(a) for a in const_ins]
    out_specs = [pl.BlockSpec((tm, w), lambda i: (i, 0)) for (w, _) in outs]
    out_shape = [jax.ShapeDtypeStruct((n_rows, w), dt) for (w, dt) in outs]
    return pl.pallas_call(
        body, grid=(n_rows // tm,), in_specs=in_specs, out_specs=out_specs, out_shape=out_shape,
        compiler_params=pltpu.CompilerParams(dimension_semantics=("parallel",), vmem_limit_bytes=VMEM_LIMIT),
        name=name,
    )(*row_ins, *[a for (a, _) in tile_ins], *const_ins)


def _ada_body(c_ref, w_ref, b_ref, o_ref):
    c = c_ref[...]
    o_ref[...] = jnp.dot(c * _sigmoid(c), w_ref[...], precision=HIGHEST, preferred_element_type=F32) + b_ref[...]


def _ada_call(cc, ada_w, ada_b):
    depth, D, W = ada_w.shape
    tn = 1024
    return pl.pallas_call(
        _ada_body, grid=(depth, W // tn),
        in_specs=[pl.BlockSpec(cc.shape, lambda l, j: (0, 0)),
                  pl.BlockSpec((None, D, tn), lambda l, j: (l, 0, j)),
                  pl.BlockSpec((None, 1, tn), lambda l, j: (l, 0, j))],
        out_specs=pl.BlockSpec((None, cc.shape[0], tn), lambda l, j: (l, 0, j)),
        out_shape=jax.ShapeDtypeStruct((depth, cc.shape[0], W), F32),
        compiler_params=pltpu.CompilerParams(dimension_semantics=("parallel", "parallel"), vmem_limit_bytes=VMEM_LIMIT),
        name="ada",
    )(cc, ada_w, ada_b.reshape(depth, 1, W))


def _mla_a_body(x_ref, mod_ref, g_ref, wa_ref, qan_ref, kvn_ref, cq_ref, ckv_ref, kr_ref):
    h = _norm_mod(x_ref[...], g_ref[...], mod_ref[0:1, :], mod_ref[1:2, :]).astype(BF16)
    acc = _dot(h, wa_ref[...])
    qa = acc[:, 0:Q_LORA]
    kv = acc[:, Q_LORA:Q_LORA + KV_LORA]
    cq_ref[...] = (qa * _rms(qa, Q_LORA) * qan_ref[...]).astype(BF16)
    ckv_ref[...] = (kv * _rms(kv, KV_LORA) * kvn_ref[...]).astype(BF16)
    kr_ref[...] = acc[:, Q_LORA + KV_LORA:]


def _mla_q_body(cq_ref, cos_ref, sin_ref, wqb_ref, gq_ref, q_ref):
    acc = _dot(cq_ref[...], wqb_ref[...])
    cos = cos_ref[...]
    sin = sin_ref[...]
    g = gq_ref[...]
    for h in range(MLA_HEADS):
        a = acc[:, QK_PAD * h:QK_PAD * (h + 1)]
        an = a * _rms(a, QK_HEAD) * g
        q_ref[:, QK_PAD * h:QK_PAD * h + LANES] = an[:, :LANES].astype(BF16)
        q_ref[:, QK_PAD * h + LANES:QK_PAD * (h + 1)] = _rope(an[:, LANES:], cos, sin).astype(BF16)


def _mla_kv_body(ckv_ref, kr_ref, cos_ref, sin_ref, wk_ref, wv_ref, gk_ref, k_ref, v_ref):
    ckv = ckv_ref[...]
    kn = _dot(ckv, wk_ref[...])
    vals = _dot(ckv, wv_ref[...]).astype(BF16)
    lane = lax.broadcasted_iota(jnp.int32, (vals.shape[0], LANES), 1)
    ones_col = jnp.where(lane == 0, 1.0, 0.0).astype(BF16)
    for h in range(MLA_HEADS):
        v_ref[:, V_PAD * h:V_PAD * h + V_HEAD] = vals[:, V_HEAD * h:V_HEAD * (h + 1)]
        v_ref[:, V_PAD * h + V_HEAD:V_PAD * (h + 1)] = ones_col
    kr = kr_ref[...]
    ss_rope = jnp.sum(kr * kr, axis=-1, keepdims=True)
    g = gk_ref[...]
    g_nope = g[:, :LANES]
    kr_rot = _rope(kr * g[:, LANES:], cos_ref[...], sin_ref[...])
    for h in range(MLA_HEADS):
        a = kn[:, LANES * h:LANES * (h + 1)]
        inv = lax.rsqrt((jnp.sum(a * a, axis=-1, keepdims=True) + ss_rope) * (1.0 / QK_HEAD) + NORM_EPS)
        k_ref[:, QK_PAD * h:QK_PAD * h + LANES] = (a * inv * g_nope).astype(BF16)
        k_ref[:, QK_PAD * h + LANES:QK_PAD * (h + 1)] = (kr_rot * inv).astype(BF16)


def _attn_body(*refs, nseg, sub):
    q_ref = refs[0]
    o_ref = refs[-1]
    n_sub = q_ref.shape[0] // sub

    def scores(j):
        q = q_ref[sub * j:sub * (j + 1), :]
        return [lax.dot_general(q, refs[1 + 2 * i][...], _NT, preferred_element_type=F32) for i in range(nseg)]

    def finish(j, ss):
        m = functools.reduce(jnp.maximum, [jnp.max(s, axis=-1, keepdims=True) for s in ss])
        acc = functools.reduce(lambda a, b: a + b,
                               [_dot(jnp.exp((ss[i] - m).astype(BF16)), refs[2 + 2 * i][...]) for i in range(nseg)])
        o_ref[sub * j:sub * (j + 1), :] = (acc[:, :V_HEAD] / acc[:, V_HEAD:V_HEAD + 1]).astype(o_ref.dtype)

    pending = scores(0)
    for j in range(n_sub):
        following = scores(j + 1) if j + 1 < n_sub else None
        finish(j, pending)
        pending = following


def _attention(q, k, v, lay, tq):
    B, T, L = lay.B, lay.T, lay.L
    nq = T // tq
    ctx0 = (B * T) // L
    params = pltpu.CompilerParams(dimension_semantics=("parallel", "parallel", "arbitrary"),
                                  vmem_limit_bytes=VMEM_LIMIT)
    o_lat = pl.pallas_call(
        functools.partial(_attn_body, nseg=2, sub=min(ATTN_SUB, tq)), grid=(B, MLA_HEADS, nq),
        in_specs=[pl.BlockSpec((tq, QK_PAD), lambda b, h, i: (b * nq + i, h)),
                  pl.BlockSpec((L, QK_PAD), lambda b, h, i: (ctx0 + b, h)),
                  pl.BlockSpec((L, V_PAD), lambda b, h, i: (ctx0 + b, h)),
                  pl.BlockSpec((T, QK_PAD), lambda b, h, i: (b, h)),
                  pl.BlockSpec((T, V_PAD), lambda b, h, i: (b, h))],
        out_specs=pl.BlockSpec((tq, V_HEAD), lambda b, h, i: (b * nq + i, h)),
        out_shape=jax.ShapeDtypeStruct((B * T, MLA_HEADS * V_HEAD), BF16),
        compiler_params=params, name="attn_latent",
    )(q, k, v, k, v)
    o_ctx = pl.pallas_call(
        functools.partial(_attn_body, nseg=1, sub=min(ATTN_SUB, L)), grid=(B, MLA_HEADS, 1),
        in_specs=[pl.BlockSpec((L, QK_PAD), lambda b, h, i: (ctx0 + b, h)),
                  pl.BlockSpec((L, QK_PAD), lambda b, h, i: (ctx0 + b, h)),
                  pl.BlockSpec((L, V_PAD), lambda b, h, i: (ctx0 + b, h))],
        out_specs=pl.BlockSpec((L, V_HEAD), lambda b, h, i: (b, h)),
        out_shape=jax.ShapeDtypeStruct((B * L, MLA_HEADS * V_HEAD), BF16),
        compiler_params=params, name="attn_ctx",
    )(q, k, v)
    return jnp.concatenate([o_lat, o_ctx], axis=0)


def _proj_resid_body(a_ref, x_ref, mod_ref, w_ref, o_ref):
    o_ref[...] = x_ref[...] + mod_ref[2:3, :] * _dot(a_ref[...], w_ref[...])


def _router_body(x_ref, mod_ref, g_ref, router_ref, h_ref, route_ref):
    hf = _norm_mod(x_ref[...], g_ref[...], mod_ref[3:4, :], mod_ref[4:5, :])
    h_ref[...] = hf
    logits = jnp.dot(hf, router_ref[...], precision=HIGHEST, preferred_element_type=F32)
    lane = lax.broadcasted_iota(jnp.int32, logits.shape, 1)
    lanef = lane.astype(F32)
    neg = jnp.float32(-1e30)
    lg = jnp.where(lane < N_EXPERTS, logits, neg)
    m1 = jnp.max(lg, axis=-1, keepdims=True)
    i1 = jnp.min(jnp.where(lg == m1, lanef, float(LANES)), axis=-1, keepdims=True)
    lg2 = jnp.where(lanef == i1, neg, lg)
    m2 = jnp.max(lg2, axis=-1, keepdims=True)
    i2 = jnp.min(jnp.where(lg2 == m2, lanef, float(LANES)), axis=-1, keepdims=True)
    e2 = jnp.exp(m2 - m1)
    w1 = 1.0 / (1.0 + e2)
    w2 = e2 / (1.0 + e2)
    route_ref[...] = jnp.where(lane == 0, i1, jnp.where(lane == 1, i2, jnp.where(lane == 2, w1, jnp.where(lane == 3, w2, 0.0))))


def _route_plan(route, n, tm_g):
    i1 = route[:, 0].astype(jnp.int32)
    i2 = route[:, 1].astype(jnp.int32)
    experts = jnp.arange(N_EXPERTS, dtype=jnp.int32)
    onehot = ((i1[:, None] == experts) | (i2[:, None] == experts)).astype(jnp.int32)
    rank = jnp.cumsum(onehot, axis=0) - onehot
    tiles_e = (jnp.sum(onehot, axis=0) + tm_g - 1) // tm_g
    tile_end = jnp.cumsum(tiles_e)
    start = (tile_end - tiles_e) * tm_g
    pos1 = start[i1] + jnp.take_along_axis(rank, i1[:, None], axis=1)[:, 0]
    pos2 = start[i2] + jnp.take_along_axis(rank, i2[:, None], axis=1)[:, 0]
    n_tiles = -(-2 * n // tm_g) + N_EXPERTS
    tile_ids = jnp.arange(n_tiles, dtype=jnp.int32)
    tile_expert = jnp.minimum(jnp.sum((tile_ids[:, None] >= tile_end[None, :]).astype(jnp.int32), axis=1), N_EXPERTS - 1)
    tok = jnp.arange(n, dtype=jnp.int32)
    src = jnp.zeros((n_tiles * tm_g,), jnp.int32).at[jnp.concatenate([pos1, pos2])].set(
        jnp.concatenate([tok, tok]), unique_indices=True, mode="promise_in_bounds")
    return pos1, pos2, src, tile_expert.astype(jnp.int32), tile_end[-1:].astype(jnp.int32)


def _row_copy(src, src_row, dst, dst_row, sem):
    return pltpu.make_async_copy(src.at[pl.ds(src_row, 1), :], dst.at[pl.ds(dst_row, 1), :], sem)


def _gffn_body(te_ref, nu_ref, src_ref, h_hbm, w1_ref, w3_ref, w2_ref, ys_ref,
               xg, acc_ref, dsem, *, tm, tf):
    del te_ref
    m = pl.program_id(0)
    n_m = pl.num_programs(0)
    slot = m & 1
    other = 1 - slot
    n_used = nu_ref[0]
    used = m < n_used
    nf = w1_ref.shape[1] // tf
    per_f = tm // nf

    def wait_slot(s):
        pltpu.make_async_copy(xg.at[s], xg.at[s], dsem.at[s]).wait()

    @pl.when(m == 0)
    def _():
        def issue(t, carry):
            _row_copy(h_hbm, src_ref[t], xg.at[0], t, dsem.at[0]).start()
            return carry

        lax.fori_loop(0, tm, issue, 0, unroll=DMA_UNROLL)

    @pl.when(m <= n_used)
    def _():
        wait_slot(slot)

    @pl.when(used)
    def _():
        base = jnp.minimum(m + 1, n_m - 1) * tm
        h = xg[slot].astype(BF16)
        for c in range(nf):
            for t in range(per_f * c, per_f * (c + 1)):
                _row_copy(h_hbm, src_ref[base + t], xg.at[other], t, dsem.at[other]).start()
            a1 = _dot(h, w1_ref[:, tf * c:tf * (c + 1)])
            a3 = _dot(h, w3_ref[:, tf * c:tf * (c + 1)])
            part = _dot((a1 * _sigmoid(a1) * a3).astype(BF16), w2_ref[tf * c:tf * (c + 1), :])
            if c == 0:
                acc_ref[...] = part
            else:
                acc_ref[...] += part
        ys_ref[...] = acc_ref[...]

    @pl.when(jnp.logical_not(used))
    def _():
        ys_ref[...] = jnp.zeros_like(ys_ref)

    @pl.when(used & (m == n_m - 1))
    def _():
        wait_slot(other)


def _gffn(tile_expert, n_used, src, h, w1, w3, w2, layer, tm_g, tf):
    _, E, D, F = w1.shape
    n_tiles = src.shape[0] // tm_g
    resident = pl.Buffered(1)
    grid_spec = pltpu.PrefetchScalarGridSpec(
        num_scalar_prefetch=3, grid=(n_tiles,),
        in_specs=[pl.BlockSpec(memory_space=pl.ANY),
                  pl.BlockSpec((None, None, D, F), lambda m, te, nu, sr: (layer, te[m], 0, 0), pipeline_mode=resident),
                  pl.BlockSpec((None, None, D, F), lambda m, te, nu, sr: (layer, te[m], 0, 0), pipeline_mode=resident),
                  pl.BlockSpec((None, None, F, D), lambda m, te, nu, sr: (layer, te[m], 0, 0), pipeline_mode=resident)],
        out_specs=pl.BlockSpec((tm_g, D), lambda m, te, nu, sr: (m, 0)),
        scratch_shapes=[pltpu.VMEM((2, tm_g, D), F32), pltpu.VMEM((tm_g, D), F32), pltpu.SemaphoreType.DMA((2,))])
    return pl.pallas_call(
        functools.partial(_gffn_body, tm=tm_g, tf=tf), grid_spec=grid_spec,
        out_shape=jax.ShapeDtypeStruct((n_tiles * tm_g, D), F32),
        compiler_params=pltpu.CompilerParams(dimension_semantics=("arbitrary",),
                                             vmem_limit_bytes=VMEM_LIMIT),
        name="moe_gffn",
    )(tile_expert, n_used, src, h, w1, w3, w2)


def _moe_combine_body(pos_ref, ys_hbm, route_ref, x_ref, mod_ref, o_ref, yg, dsem, *, tm, n):
    i = pl.program_id(0)
    slot = i & 1

    def start_gather(tile, s):
        def issue(t, carry):
            tok = tile * tm + t
            _row_copy(ys_hbm, pos_ref[tok], yg.at[s, 0], t, dsem.at[s]).start()
            _row_copy(ys_hbm, pos_ref[n + tok], yg.at[s, 1], t, dsem.at[s]).start()
            return carry

        lax.fori_loop(0, tm, issue, 0, unroll=DMA_UNROLL)

    @pl.when(i == 0)
    def _():
        start_gather(0, 0)

    pltpu.make_async_copy(yg.at[slot], yg.at[slot], dsem.at[slot]).wait()

    @pl.when(i + 1 < pl.num_programs(0))
    def _():
        start_gather(i + 1, 1 - slot)

    route = route_ref[...]
    y = route[:, 2:3] * yg[slot, 0] + route[:, 3:4] * yg[slot, 1]
    o_ref[...] = x_ref[...] + mod_ref[5:6, :] * y


def _moe_combine(pos1, pos2, ys, route, xf, modtab, lay, tm, n_out):
    n, D = xf.shape
    n_lat_tiles = lay.n_lat // tm
    T, B = lay.T, lay.B
    row = lambda i, pos: (i, 0)
    grid_spec = pltpu.PrefetchScalarGridSpec(
        num_scalar_prefetch=1, grid=(n_out // tm,),
        in_specs=[pl.BlockSpec(memory_space=pl.ANY), pl.BlockSpec((tm, LANES), row), pl.BlockSpec((tm, D), row),
                  pl.BlockSpec((None, 6, D), lambda i, pos: (jnp.where(i < n_lat_tiles, (i * tm) // T, B), 0, 0))],
        out_specs=pl.BlockSpec((tm, D), row),
        scratch_shapes=[pltpu.VMEM((2, 2, tm, D), F32), pltpu.SemaphoreType.DMA((2,))])
    return pl.pallas_call(
        functools.partial(_moe_combine_body, tm=tm, n=n), grid_spec=grid_spec,
        out_shape=jax.ShapeDtypeStruct((n_out, D), F32),
        compiler_params=pltpu.CompilerParams(dimension_semantics=("arbitrary",), vmem_limit_bytes=VMEM_LIMIT),
        name="moe_combine",
    )(jnp.concatenate([pos1, pos2]), ys, route, xf, modtab)


def _ffn_body(x_ref, mod_ref, g_ref, w1_ref, w3_ref, w2_ref, o_ref, *, tf):
    x = x_ref[...]
    h = _norm_mod(x, g_ref[...], mod_ref[3:4, :], mod_ref[4:5, :]).astype(BF16)
    acc = None
    for c in range(w1_ref.shape[1] // tf):
        a1 = _dot(h, w1_ref[:, tf * c:tf * (c + 1)])
        a3 = _dot(h, w3_ref[:, tf * c:tf * (c + 1)])
        part = _dot((a1 * _sigmoid(a1) * a3).astype(BF16), w2_ref[tf * c:tf * (c + 1), :])
        acc = part if acc is None else acc + part
    o_ref[...] = x + mod_ref[5:6, :] * acc


def _rwkv_feat_body(*refs, vres, lay):
    if vres:
        (x_ref, vf_ref, xp_ref, xn_ref, mod_ref, g_ref, mix_ref, wr_ref, wk_ref, wv_ref, w1_ref, a1_ref, g1_ref,
         w2_ref, a2_ref, g2_ref, w0_ref, a0_ref, kkw_ref, ones_ref, v1_ref, v2_ref, v0_ref,
         r_ref, k_ref, lw_ref, ad_ref, kk_ref, vo_ref, gate_ref) = refs
    else:
        (x_ref, xp_ref, xn_ref, mod_ref, g_ref, mix_ref, wr_ref, wk_ref, wv_ref, w1_ref, a1_ref, g1_ref,
         w2_ref, a2_ref, g2_ref, w0_ref, a0_ref, kkw_ref, ones_ref,
         r_ref, k_ref, lw_ref, ad_ref, kk_ref, vo_ref, gate_ref) = refs
    tm = x_ref.shape[0]
    g = g_ref[...]
    shift = mod_ref[0:1, :]
    scale = mod_ref[1:2, :]
    h = _norm_mod(x_ref[...], g, shift, scale)
    h_before = _norm_mod(xp_ref[...], g, shift, scale)[HALO - 1:HALO, :]
    h_after = _norm_mod(xn_ref[...], g, shift, scale)[0:1, :]
    row = lax.broadcasted_iota(jnp.int32, (tm, 1), 0)
    gidx = pl.program_id(0) * tm + row
    in_lat = gidx < lay.n_lat
    pos = jnp.where(in_lat, gidx & (lay.T - 1), (gidx - lay.n_lat) & (lay.L - 1))
    last_pos = jnp.where(in_lat, lay.T - 1, lay.L - 1)
    prev = jnp.where(row == 0, h_before, pltpu.roll(h, 1, axis=0))
    prev = jnp.where(pos == 0, 0.0, prev)
    nxt = jnp.where(row == tm - 1, h_after, pltpu.roll(h, tm - 1, axis=0))
    nxt = jnp.where(pos == last_pos, 0.0, nxt)
    xx = 0.5 * (prev + nxt) - h

    def mixed(j):
        return (h + xx * mix_ref[j:j + 1, :]).astype(BF16)

    r_ref[...] = _dot(mixed(0), wr_ref[...]).astype(r_ref.dtype)
    tw = jnp.tanh(_dot(mixed(1), w1_ref[...]))
    dec = w0_ref[...] + _dot(tw.astype(BF16), w2_ref[...])
    lw_ref[...] = -DECAY_SCALE * _sigmoid(dec)
    k = _dot(mixed(2), wk_ref[...])
    k_ref[...] = k
    kkr = k * kkw_ref[...]
    ss = _head_sum(kkr * kkr, ones_ref[...])
    kk_ref[...] = (kkr / jnp.maximum(jnp.sqrt(ss), 1e-12)).astype(kk_ref.dtype)
    xv = mixed(3)
    v = _dot(xv, wv_ref[...])
    if vres:
        yv = _dot(xv, v1_ref[...])
        v = v + (vf_ref[...] - v) * _sigmoid(v0_ref[...] + _dot(yv.astype(BF16), v2_ref[...]))
    vo_ref[...] = v
    ya = _dot(mixed(4), a1_ref[...])
    ad_ref[...] = _sigmoid(a0_ref[...] + _dot(ya.astype(BF16), a2_ref[...])).astype(ad_ref.dtype)
    sgate = _sigmoid(_dot(mixed(5), g1_ref[...]))
    gate_ref[...] = _dot(sgate.astype(BF16), g2_ref[...]).astype(gate_ref.dtype)


def _scan_body(*refs):
    dir_refs = (refs[0:6], refs[6:12])
    ka_ref = refs[12]
    y_refs = refs[13:15]
    s_ref = refs[15]
    C = CHUNK
    D = ka_ref.shape[1]
    n_heads = D // C

    @pl.when(pl.program_id(1) == 0)
    def _():
        s_ref[...] = jnp.zeros_like(s_ref)

    ri = lax.broadcasted_iota(jnp.int32, (C, C), 0)
    ci = lax.broadcasted_iota(jnp.int32, (C, C), 1)
    xr = ri ^ ci
    eye = (ri == ci).astype(F32)
    r2 = lax.broadcasted_iota(jnp.int32, (C, 2 * C), 0)
    c2 = lax.broadcasted_iota(jnp.int32, (C, 2 * C), 1) & (C - 1)
    zeros_cc = jnp.zeros((C, C), BF16)
    ka = ka_ref[...]

    ar, bk, vh, bkp, dph, strict, incl, where = [], [], [], [], [], [], [], []
    for d, (lw_ref, ad_ref, kf_ref, kk_ref, v_ref, r_ref) in enumerate(dir_refs):
        sg = 1 if d == 0 else -1
        strict_d = (r2 - c2) * sg > 0
        incl_d = (r2 - c2) * sg >= 0
        lw = lw_ref[...]
        ad = ad_ref[...].astype(F32)
        kk = kk_ref[...].astype(F32)
        c = jnp.dot(((ri - ci) * sg >= 0).astype(F32), lw, precision=HIGHEST, preferred_element_type=F32)
        pc = c[C - 1:C, :] if d == 0 else c[0:1, :]
        kd = kf_ref[...] * (1.0 + (ad - 1.0) * ka)
        b = kk * ad
        einv = jnp.exp(-c)
        a_t = (-kk * jnp.exp(c - lw)).astype(BF16)
        r_t = (r_ref[...].astype(F32) * jnp.exp(c)).astype(BF16)
        b_t = (b * einv).astype(BF16)
        k_t = (kd * einv).astype(BF16)
        epc = jnp.exp(pc - c)
        b_p = (b * epc).astype(BF16)
        k_p = (kd * epc).astype(BF16)
        d_p = jnp.exp(pc)
        vb = v_ref[...].astype(BF16)
        for h in range(n_heads):
            sl = slice(C * h, C * (h + 1))
            ar.append(jnp.concatenate([a_t[:, sl], r_t[:, sl]], axis=0))
            bk.append(jnp.concatenate([b_t[:, sl], k_t[:, sl]], axis=0))
            vh.append(vb[:, sl])
            bkp.append(jnp.concatenate([b_p[:, sl], k_p[:, sl]], axis=0))
            dph.append(d_p[:, sl])
            strict.append(strict_d)
            incl.append(incl_d)
            where.append((d, h, sl))

    items = range(len(where))
    sc = [lax.dot_general(ar[i], bk[i], _NT, preferred_element_type=F32) for i in items]
    st = [s_ref[d, h] for (d, h, _) in where]
    ars = [lax.dot_general(ar[i], st[i].astype(BF16), _NT, preferred_element_type=F32) for i in items]
    top = [jnp.where(strict[i], sc[i][0:C], 0.0) for i in items]
    bot = [jnp.where(incl[i], sc[i][C:], 0.0).astype(BF16) for i in items]
    lab = [top[i][:, 0:C] for i in items]
    x = [ars[i][0:C] + _dot(top[i].astype(BF16), jnp.concatenate([zeros_cc, vh[i]], axis=0)) for i in items]
    t = [eye + jnp.where(xr == 1, lab[i], 0.0) for i in items]
    for lb in range(1, 6):
        level = (xr >> lb) == 1
        tb = [t[i].astype(BF16) for i in items]
        ot = [_dot(jnp.where(level, lab[i], 0.0).astype(BF16), tb[i]).astype(BF16) for i in items]
        t = [t[i] + _dot(tb[i], ot[i]) for i in items]
    u = [_dot(t[i].astype(BF16), x[i].astype(BF16)) for i in items]
    uv = [jnp.concatenate([u[i].astype(BF16), vh[i]], axis=0) for i in items]
    for i, (d, h, sl) in enumerate(where):
        y_refs[d][:, sl] = ars[i][C:] + _dot(bot[i], uv[i])
    for i, (d, h, sl) in enumerate(where):
        s_ref[d, h] = st[i] * dph[i] + lax.dot_general(uv[i], bkp[i], _TN, preferred_element_type=F32)


def _scan(lw, ad, kf, kk, v, r, k_a, lay):
    B, T, L, D = lay.B, lay.T, lay.L, lay.D
    nc_ctx = L // CHUNK
    nc_lat = T // CHUNK
    ctx_base = (B * T) // CHUNK

    def chunk_index(b, d, s):
        ctx_i = s if d == 0 else nc_ctx - 1 - s
        lat_i = s - nc_ctx if d == 0 else nc_lat - 1 - (s - nc_ctx)
        return jnp.where(s < nc_ctx, ctx_base + b * nc_ctx + ctx_i, b * nc_lat + lat_i)

    def spec(d, col):
        return pl.BlockSpec((CHUNK, D), lambda b, s: (chunk_index(b, d, s), col))

    in_specs, args = [], []
    for d in range(2):
        in_specs += [spec(d, d), spec(d, d), spec(d, 0), spec(d, 0), spec(d, 0), spec(d, 0)]
        args += [lw, ad, kf, kk, v, r]
    return pl.pallas_call(
        _scan_body, grid=(B, nc_ctx + nc_lat),
        in_specs=in_specs + [pl.BlockSpec((1, D), lambda b, s: (0, 0))],
        out_specs=[spec(0, 0), spec(1, 0)],
        out_shape=[jax.ShapeDtypeStruct((lay.n, D), F32), jax.ShapeDtypeStruct((lay.n, D), F32)],
        scratch_shapes=[pltpu.VMEM((2, D // CHUNK, CHUNK, CHUNK), F32)],
        compiler_params=pltpu.CompilerParams(dimension_semantics=("parallel", "arbitrary"),
                                             vmem_limit_bytes=VMEM_LIMIT),
        name="wkv_scan",
    )(*args, k_a)


def _rwkv_post_body(yf_ref, yr_ref, r_ref, kf_ref, ad_ref, v_ref, g_ref, x_ref, mod_ref,
                    ka_ref, rk_ref, lnw_ref, lnb_ref, ones_ref, wo_ref, o_ref):
    D = x_ref.shape[1]
    ones4 = ones_ref[...]
    y = yf_ref[...] + yr_ref[...]
    inv_n = 1.0 / RWKV_HEAD
    yc = y - _head_sum(y, ones4) * inv_n
    var = _head_sum(yc * yc, ones4) * inv_n
    yn = yc * lax.rsqrt(var + LN_X_EPS) * lnw_ref[...] + lnb_ref[...]
    kf = kf_ref[...]
    ad = ad_ref[...].astype(F32)
    ka = ka_ref[...]
    k_bonus = 0.5 * (kf * (1.0 + (ad[:, :D] - 1.0) * ka) + kf * (1.0 + (ad[:, D:] - 1.0) * ka))
    bonus = _head_sum(r_ref[...].astype(F32) * k_bonus * rk_ref[...], ones4) * v_ref[...]
    o = ((yn + bonus) * g_ref[...].astype(F32)).astype(BF16)
    o_ref[...] = x_ref[...] + mod_ref[2:3, :] * _dot(o, wo_ref[...])


def _rope_tables(T, tm):
    pos = jnp.arange(T)
    inv_freq = 1.0 / (ROPE_BASE ** (jnp.arange(AXIS_PAIRS, dtype=F32) / AXIS_PAIRS))
    ang_r = (pos // GRID_W).astype(F32)[:, None] * inv_freq[None, :]
    ang_c = (pos % GRID_W).astype(F32)[:, None] * inv_freq[None, :]
    pad1 = jnp.ones((T, LANES - QK_ROPE), F32)
    pad0 = jnp.zeros((T, LANES - QK_ROPE), F32)
    cos = jnp.concatenate([jnp.cos(ang_r), jnp.cos(ang_r), jnp.cos(ang_c), jnp.cos(ang_c), pad1], axis=1)
    sin = jnp.concatenate([-jnp.sin(ang_r), jnp.sin(ang_r), -jnp.sin(ang_c), jnp.sin(ang_c), pad0], axis=1)
    cos = jnp.concatenate([cos, jnp.ones((tm, LANES), F32)], axis=0)
    sin = jnp.concatenate([sin, jnp.zeros((tm, LANES), F32)], axis=0)
    return cos, sin


def _pad_cols(w, width):
    return jnp.pad(w, ((0, 0), (0, width - w.shape[1])))


def _pad_rows(w, height):
    return jnp.pad(w, ((0, height - w.shape[0]), (0, 0)))


def _block_diag2(w0, w1):
    z0 = jnp.zeros_like(w0)
    z1 = jnp.zeros_like(w1)
    return jnp.concatenate([jnp.concatenate([w0, z1], axis=1), jnp.concatenate([z0, w1], axis=1)], axis=0)


def kernel(x, c, ctx, c_ctx, ada_w, ada_b, norm1_g, norm2_g, mla_wqa, mla_qa_norm, mla_wqb, mla_wkva, mla_kva_norm, mla_wkvb, mla_q_norm, mla_k_norm, mla_wo, rwkv_mix, rwkv_wr, rwkv_wk, rwkv_wv, rwkv_wo, rwkv_w0, rwkv_w1, rwkv_w2, rwkv_a0, rwkv_a1, rwkv_a2, rwkv_g1, rwkv_g2, rwkv_k_k, rwkv_k_a, rwkv_r_k, rwkv_ln_w, rwkv_ln_b, rwkv_v0, rwkv_v1, rwkv_v2, ffn_w1, ffn_w3, ffn_w2, moe_router, moe_w1, moe_w3, moe_w2):
    B, T, D = x.shape
    L = ctx.shape[1]
    depth = ada_w.shape[0]
    lay = _Layout(B, T, L, D)
    n = lay.n
    tm = min(512, T)
    tm_small = min(256, T)
    assert T % tm == 0 and (B * L) % tm == 0 and T % GRID_W == 0 and L % CHUNK == 0 and B + 1 <= 16
    assert T & (T - 1) == 0 and L & (L - 1) == 0 and L % tm_small == 0

    ffn_b = [w.astype(BF16) for w in (ffn_w1, ffn_w3, ffn_w2)]
    moe_b = [w.astype(BF16) for w in (moe_w1, moe_w3, moe_w2)]
    xf = jnp.concatenate([x.reshape(B * T, D), ctx.reshape(B * L, D)], axis=0)
    cc = jnp.concatenate([c, c_ctx[None, :], jnp.zeros((16 - B - 1, D), F32)], axis=0)
    modall = _ada_call(cc, ada_w, ada_b)[:, :B + 1].reshape(depth, B + 1, 6, D)
    cos_t, sin_t = _rope_tables(T, tm)
    ones4 = jnp.kron(jnp.eye(4, dtype=F32), jnp.ones((RWKV_HEAD, RWKV_HEAD), F32)).astype(BF16)
    row1 = lambda z: z.reshape(1, -1)

    v_first = None
    for i in range(depth):
        j = i // 2
        modtab = modall[i]
        mod_t = (modtab, lay.mod_spec(tm))
        mod_s = (modtab, lay.mod_spec(tm_small))
        if i % 2 == 0:
            wa = jnp.concatenate([mla_wqa[j], _pad_cols(mla_wkva[j], KV_LORA + LANES)], axis=1).astype(BF16)
            wqb = mla_wqb[j].reshape(Q_LORA, MLA_HEADS, QK_HEAD)
            wqb = jnp.pad(wqb, ((0, 0), (0, 0), (0, QK_PAD - QK_HEAD))).reshape(Q_LORA, MLA_HEADS * QK_PAD).astype(BF16)
            wkvb = mla_wkvb[j].reshape(KV_LORA, MLA_HEADS, QK_NOPE + V_HEAD)
            wk = wkvb[:, :, :QK_NOPE].reshape(KV_LORA, MLA_HEADS * QK_NOPE).astype(BF16)
            wv = wkvb[:, :, QK_NOPE:].reshape(KV_LORA, MLA_HEADS * V_HEAD).astype(BF16)
            gq = _pad_cols(row1(mla_q_norm[j]) * SM_SCALE, QK_PAD)
            gk = _pad_cols(row1(mla_k_norm[j]), QK_PAD)
            cq, ckv, kr = _rowwise(
                _mla_a_body, "mla_a", n, tm, [xf], [mod_t],
                [row1(norm1_g[i]), wa, row1(mla_qa_norm[j]), row1(mla_kva_norm[j])],
                [(Q_LORA, BF16), (KV_LORA, BF16), (LANES, F32)])
            rope_in = [(cos_t, lay.rope_spec(tm)), (sin_t, lay.rope_spec(tm))]
            (q,) = _rowwise(_mla_q_body, "mla_q", n, tm, [cq], rope_in, [wqb, gq], [(MLA_HEADS * QK_PAD, BF16)])
            k, v = _rowwise(_mla_kv_body, "mla_kv", n, tm, [ckv, kr], rope_in, [wk, wv, gk],
                            [(MLA_HEADS * QK_PAD, BF16), (MLA_HEADS * V_PAD, BF16)])
            o = _attention(q, k, v, lay, tq=min(2048, T))
            (xf,) = _rowwise(_proj_resid_body, "mla_out", n, tm, [o, xf], [mod_t], [mla_wo[j].astype(BF16)], [(D, F32)])
        else:
            vres = j > 0
            mix = _pad_rows(rwkv_mix[j], 8)
            w1c = jnp.concatenate([rwkv_w1[j, 0], rwkv_w1[j, 1]], axis=1).astype(BF16)
            a1c = jnp.concatenate([rwkv_a1[j, 0], rwkv_a1[j, 1]], axis=1).astype(BF16)
            g1p = _pad_cols(rwkv_g1[j], 2 * LANES).astype(BF16)
            w2c = _block_diag2(rwkv_w2[j, 0], rwkv_w2[j, 1]).astype(BF16)
            a2c = _block_diag2(rwkv_a2[j, 0], rwkv_a2[j, 1]).astype(BF16)
            g2p = _pad_rows(rwkv_g2[j], 2 * LANES).astype(BF16)
            consts = [row1(norm1_g[i]), mix, rwkv_wr[j].astype(BF16), rwkv_wk[j].astype(BF16),
                      rwkv_wv[j].astype(BF16), w1c, a1c, g1p, w2c, a2c, g2p,
                      rwkv_w0[j].reshape(1, 2 * D), rwkv_a0[j].reshape(1, 2 * D), row1(rwkv_k_k[j]), ones4]
            rows = [xf]
            if vres:
                rows.append(v_first)
                consts += [_pad_cols(rwkv_v1[j - 1], LANES).astype(BF16), _pad_rows(rwkv_v2[j - 1], LANES).astype(BF16),
                           row1(rwkv_v0[j - 1])]
            halo_per_tile = tm_small // HALO
            before = pl.BlockSpec((HALO, D), lambda t: (jnp.maximum(t * halo_per_tile - 1, 0), 0))
            after = pl.BlockSpec((HALO, D), lambda t: (jnp.minimum((t + 1) * halo_per_tile, n // HALO - 1), 0))
            r, kf, lw, ad, kk, v_out, gate = _rowwise(
                functools.partial(_rwkv_feat_body, vres=vres, lay=lay), "rwkv_feat", n, tm_small,
                rows, [(xf, before), (xf, after), mod_s], consts,
                [(D, BF16), (D, F32), (2 * D, F32), (2 * D, BF16), (D, BF16), (D, F32), (D, BF16)])
            if j == 0:
                v_first = v_out
            k_a = row1(rwkv_k_a[j])
            y_fwd, y_rev = _scan(lw, ad, kf, kk, v_out, r, k_a, lay)
            (xf,) = _rowwise(
                _rwkv_post_body, "rwkv_out", n, tm_small, [y_fwd, y_rev, r, kf, ad, v_out, gate, xf], [mod_s],
                [k_a, row1(rwkv_r_k[j]), row1(rwkv_ln_w[j]), row1(rwkv_ln_b[j]), ones4, rwkv_wo[j].astype(BF16)],
                [(D, F32)])

        if i % 2 == 0:
            (xf,) = _rowwise(functools.partial(_ffn_body, tf=ffn_w1.shape[2] // 2), "dense_ffn", n, tm, [xf], [mod_t],
                             [row1(norm2_g[i]), ffn_b[0][j], ffn_b[1][j], ffn_b[2][j]], [(D, F32)])
        else:
            h, route = _rowwise(_router_body, "router", n, tm, [xf], [mod_t],
                                [row1(norm2_g[i]), _pad_cols(moe_router[j], LANES)], [(D, F32), (LANES, F32)])
            pos1, pos2, src, tile_expert, n_used = _route_plan(route, n, MOE_TM)
            ys = _gffn(tile_expert, n_used, src, h, moe_b[0], moe_b[1], moe_b[2], j, MOE_TM, tf=moe_w1.shape[3] // 4)
            n_out = lay.n_lat if i == depth - 1 else n
            xf = _moe_combine(pos1, pos2, ys, route, xf, modtab, lay, tm, n_out)
    return xf[:B * T].reshape(B, T, D)
```

```python
import functools

import jax
import jax.numpy as jnp
from jax import lax
from jax.experimental import pallas as pl
from jax.experimental.pallas import tpu as pltpu

F32 = jnp.float32
BF16 = jnp.bfloat16
HIGHEST = lax.Precision.HIGHEST

GRID_W = 64
NORM_EPS = 1e-6
MLA_HEADS = 8
QK_NOPE = 128
QK_ROPE = 64
V_HEAD = 128
QK_HEAD = QK_NOPE + QK_ROPE
Q_LORA = 384
KV_LORA = 256
ROPE_BASE = 10000.0
AXIS_PAIRS = QK_ROPE // 4
SM_SCALE = QK_HEAD ** -0.5
RWKV_HEAD = 64
LN_X_EPS = 64e-5
DECAY_SCALE = 0.6065306597126334
N_EXPERTS = 8

LANES = 128
QK_PAD = 2 * LANES
V_PAD = 2 * LANES
CHUNK = 64
MOE_TM = 512
ATTN_SUB = 512
DMA_UNROLL = 8
HALO = 8
VMEM_LIMIT = 56 * 1024 * 1024

_NT = (((1,), (1,)), ((), ()))
_TN = (((0,), (0,)), ((), ()))


def _dot(a, b):
    return jnp.dot(a, b, preferred_element_type=F32)


def _sigmoid(z):
    return 1.0 / (1.0 + jnp.exp(-z))


def _rms(z, width):
    return lax.rsqrt(jnp.sum(z * z, axis=-1, keepdims=True) * (1.0 / width) + NORM_EPS)


def _norm_mod(x, g, shift, scale):
    return (x * _rms(x, x.shape[-1])) * g * (1.0 + scale) + shift


def _rope(z, cos, sin):
    lane = lax.broadcasted_iota(jnp.int32, z.shape, 1)
    partner = jnp.where((lane & 16) == 0, pltpu.roll(z, LANES - 16, axis=1), pltpu.roll(z, 16, axis=1))
    return z * cos + partner * sin


def _head_sum(z, ones4):
    outs = []
    for q in range(z.shape[1] // 256):
        blk = z[:, 256 * q:256 * (q + 1)]
        hi = blk.astype(BF16)
        lo = (blk - hi.astype(F32)).astype(BF16)
        outs.append(_dot(hi, ones4) + _dot(lo, ones4))
    return jnp.concatenate(outs, axis=1)


class _Layout:
    def __init__(self, B, T, L, D):
        self.B, self.T, self.L, self.D = B, T, L, D
        self.n_lat = B * T
        self.n = B * T + B * L

    def mod_spec(self, tm):
        n_lat_tiles = self.n_lat // tm
        T, B, D = self.T, self.B, self.D
        return pl.BlockSpec((None, 6, D), lambda i: (jnp.where(i < n_lat_tiles, (i * tm) // T, B), 0, 0))

    def rope_spec(self, tm):
        n_lat_tiles = self.n_lat // tm
        tpb = self.T // tm
        return pl.BlockSpec((tm, LANES), lambda i: (jnp.where(i < n_lat_tiles, i % tpb, tpb), 0))


def _const_spec(arr):
    nd = arr.ndim
    return pl.BlockSpec(arr.shape, lambda i: (0,) * nd, pipeline_mode=pl.Buffered(1))


def _rowwise(body, name, n_rows, tm, row_ins, tile_ins, const_ins, outs):
    in_specs = [pl.BlockSpec((tm, a.shape[1]), lambda i: (i, 0)) for a in row_ins]
    in_specs += [spec for (_, spec) in tile_ins]
    in_specs += [_const_spec(a) for a in const_ins]
    out_specs = [pl.BlockSpec((tm, w), lambda i: (i, 0)) for (w, _) in outs]
    out_shape = [jax.ShapeDtypeStruct((n_rows, w), dt) for (w, dt) in outs]
    return pl.pallas_call(
        body, grid=(n_rows // tm,), in_specs=in_specs, out_specs=out_specs, out_shape=out_shape,
        compiler_params=pltpu.CompilerParams(dimension_semantics=("parallel",), vmem_limit_bytes=VMEM_LIMIT),
        name=name,
    )(*row_ins, *[a for (a, _) in tile_ins], *const_ins)


def _ada_body(c_ref, w_ref, b_ref, o_ref):
    c = c_ref[...]
    o_ref[...] = jnp.dot(c * _sigmoid(c), w_ref[...], precision=HIGHEST, preferred_element_type=F32) + b_ref[...]


def _ada_call(cc, ada_w, ada_b):
    depth, D, W = ada_w.shape
    tn = 1024
    return pl.pallas_call(
        _ada_body, grid=(depth, W // tn),
        in_specs=[pl.BlockSpec(cc.shape, lambda l, j: (0, 0)),
                  pl.BlockSpec((None, D, tn), lambda l, j: (l, 0, j)),
                  pl.BlockSpec((None, 1, tn), lambda l, j: (l, 0, j))],
        out_specs=pl.BlockSpec((None, cc.shape[0], tn), lambda l, j: (l, 0, j)),
        out_shape=jax.ShapeDtypeStruct((depth, cc.shape[0], W), F32),
        compiler_params=pltpu.CompilerParams(dimension_semantics=("parallel", "parallel"), vmem_limit_bytes=VMEM_LIMIT),
        name="ada",
    )(cc, ada_w, ada_b.reshape(depth, 1, W))


def _mla_a_body(x_ref, mod_ref, g_ref, wa_ref, qan_ref, kvn_ref, cq_ref, ckv_ref, kr_ref):
    h = _norm_mod(x_ref[...], g_ref[...], mod_ref[0:1, :], mod_ref[1:2, :]).astype(BF16)
    acc = _dot(h, wa_ref[...])
    qa = acc[:, 0:Q_LORA]
    kv = acc[:, Q_LORA:Q_LORA + KV_LORA]
    cq_ref[...] = (qa * _rms(qa, Q_LORA) * qan_ref[...]).astype(BF16)
    ckv_ref[...] = (kv * _rms(kv, KV_LORA) * kvn_ref[...]).astype(BF16)
    kr_ref[...] = acc[:, Q_LORA + KV_LORA:]


def _mla_q_body(cq_ref, cos_ref, sin_ref, wqb_ref, gq_ref, q_ref):
    acc = _dot(cq_ref[...], wqb_ref[...])
    cos = cos_ref[...]
    sin = sin_ref[...]
    g = gq_ref[...]
    for h in range(MLA_HEADS):
        a = acc[:, QK_PAD * h:QK_PAD * (h + 1)]
        an = a * _rms(a, QK_HEAD) * g
        q_ref[:, QK_PAD * h:QK_PAD * h + LANES] = an[:, :LANES].astype(BF16)
        q_ref[:, QK_PAD * h + LANES:QK_PAD * (h + 1)] = _rope(an[:, LANES:], cos, sin).astype(BF16)


def _mla_kv_body(ckv_ref, kr_ref, cos_ref, sin_ref, wk_ref, wv_ref, gk_ref, k_ref, v_ref):
    ckv = ckv_ref[...]
    kn = _dot(ckv, wk_ref[...])
    vals = _dot(ckv, wv_ref[...]).astype(BF16)
    lane = lax.broadcasted_iota(jnp.int32, (vals.shape[0], LANES), 1)
    ones_col = jnp.where(lane == 0, 1.0, 0.0).astype(BF16)
    for h in range(MLA_HEADS):
        v_ref[:, V_PAD * h:V_PAD * h + V_HEAD] = vals[:, V_HEAD * h:V_HEAD * (h + 1)]
        v_ref[:, V_PAD * h + V_HEAD:V_PAD * (h + 1)] = ones_col
    kr = kr_ref[...]
    ss_rope = jnp.sum(kr * kr, axis=-1, keepdims=True)
    g = gk_ref[...]
    g_nope = g[:, :LANES]
    kr_rot = _rope(kr * g[:, LANES:], cos_ref[...], sin_ref[...])
    for h in range(MLA_HEADS):
        a = kn[:, LANES * h:LANES * (h + 1)]
        inv = lax.rsqrt((jnp.sum(a * a, axis=-1, keepdims=True) + ss_rope) * (1.0 / QK_HEAD) + NORM_EPS)
        k_ref[:, QK_PAD * h:QK_PAD * h + LANES] = (a * inv * g_nope).astype(BF16)
        k_ref[:, QK_PAD * h + LANES:QK_PAD * (h + 1)] = (kr_rot * inv).astype(BF16)


def _attn_body(*refs, nseg, sub):
    q_ref = refs[0]
    o_ref = refs[-1]
    n_sub = q_ref.shape[0] // sub

    def scores(j):
        q = q_ref[sub * j:sub * (j + 1), :]
        return [lax.dot_general(q, refs[1 + 2 * i][...], _NT, preferred_element_type=F32) for i in range(nseg)]

    def finish(j, ss):
        m = functools.reduce(jnp.maximum, [jnp.max(s, axis=-1, keepdims=True) for s in ss])
        acc = functools.reduce(lambda a, b: a + b,
                               [_dot(jnp.exp((ss[i] - m).astype(BF16)), refs[2 + 2 * i][...]) for i in range(nseg)])
        o_ref[sub * j:sub * (j + 1), :] = (acc[:, :V_HEAD] / acc[:, V_HEAD:V_HEAD + 1]).astype(o_ref.dtype)

    pending = scores(0)
    for j in range(n_sub):
        following = scores(j + 1) if j + 1 < n_sub else None
        finish(j, pending)
        pending = following


def _attention(q, k, v, lay, tq):
    B, T, L = lay.B, lay.T, lay.L
    nq = T // tq
    ctx0 = (B * T) // L
    params = pltpu.CompilerParams(dimension_semantics=("parallel", "parallel", "arbitrary"),
                                  vmem_limit_bytes=VMEM_LIMIT)
    o_lat = pl.pallas_call(
        functools.partial(_attn_body, nseg=2, sub=min(ATTN_SUB, tq)), grid=(B, MLA_HEADS, nq),
        in_specs=[pl.BlockSpec((tq, QK_PAD), lambda b, h, i: (b * nq + i, h)),
                  pl.BlockSpec((L, QK_PAD), lambda b, h, i: (ctx0 + b, h)),
                  pl.BlockSpec((L, V_PAD), lambda b, h, i: (ctx0 + b, h)),
                  pl.BlockSpec((T, QK_PAD), lambda b, h, i: (b, h)),
                  pl.BlockSpec((T, V_PAD), lambda b, h, i: (b, h))],
        out_specs=pl.BlockSpec((tq, V_HEAD), lambda b, h, i: (b * nq + i, h)),
        out_shape=jax.ShapeDtypeStruct((lay.n, MLA_HEADS * V_HEAD), BF16),
        compiler_params=params, name="attn_latent",
    )(q, k, v, k, v)
    return pl.pallas_call(
        functools.partial(_attn_body, nseg=1, sub=min(ATTN_SUB, L)), grid=(B, MLA_HEADS, 1),
        in_specs=[pl.BlockSpec((L, QK_PAD), lambda b, h, i: (ctx0 + b, h)),
                  pl.BlockSpec((L, QK_PAD), lambda b, h, i: (ctx0 + b, h)),
                  pl.BlockSpec((L, V_PAD), lambda b, h, i: (ctx0 + b, h)),
                  pl.BlockSpec(memory_space=pl.ANY)],
        out_specs=pl.BlockSpec((L, V_HEAD), lambda b, h, i: (ctx0 + b, h)),
        out_shape=jax.ShapeDtypeStruct((lay.n, MLA_HEADS * V_HEAD), BF16),
        input_output_aliases={3: 0},
        compiler_params=params, name="attn_ctx",
    )(q, k, v, o_lat)


def _proj_resid_body(a_ref, x_ref, mod_ref, w_ref, o_ref):
    o_ref[...] = x_ref[...] + mod_ref[2:3, :] * _dot(a_ref[...], w_ref[...])


def _router_body(x_ref, mod_ref, g_ref, router_ref, h_ref, route_ref):
    hf = _norm_mod(x_ref[...], g_ref[...], mod_ref[3:4, :], mod_ref[4:5, :])
    h_ref[...] = hf
    logits = jnp.dot(hf, router_ref[...], precision=HIGHEST, preferred_element_type=F32)
    lane = lax.broadcasted_iota(jnp.int32, logits.shape, 1)
    lanef = lane.astype(F32)
    neg = jnp.float32(-1e30)
    lg = jnp.where(lane < N_EXPERTS, logits, neg)
    m1 = jnp.max(lg, axis=-1, keepdims=True)
    i1 = jnp.min(jnp.where(lg == m1, lanef, float(LANES)), axis=-1, keepdims=True)
    lg2 = jnp.where(lanef == i1, neg, lg)
    m2 = jnp.max(lg2, axis=-1, keepdims=True)
    i2 = jnp.min(jnp.where(lg2 == m2, lanef, float(LANES)), axis=-1, keepdims=True)
    e2 = jnp.exp(m2 - m1)
    w1 = 1.0 / (1.0 + e2)
    w2 = e2 / (1.0 + e2)
    route_ref[...] = jnp.where(lane == 0, i1, jnp.where(lane == 1, i2, jnp.where(lane == 2, w1, jnp.where(lane == 3, w2, 0.0))))


def _route_plan(route, n, tm_g):
    i1 = route[:, 0].astype(jnp.int32)
    i2 = route[:, 1].astype(jnp.int32)
    experts = jnp.arange(N_EXPERTS, dtype=jnp.int32)
    onehot = ((i1[:, None] == experts) | (i2[:, None] == experts)).astype(jnp.int32)
    rank = jnp.cumsum(onehot, axis=0) - onehot
    tiles_e = (jnp.sum(onehot, axis=0) + tm_g - 1) // tm_g
    tile_end = jnp.cumsum(tiles_e)
    start = (tile_end - tiles_e) * tm_g
    pos1 = start[i1] + jnp.take_along_axis(rank, i1[:, None], axis=1)[:, 0]
    pos2 = start[i2] + jnp.take_along_axis(rank, i2[:, None], axis=1)[:, 0]
    n_tiles = -(-2 * n // tm_g) + N_EXPERTS
    tile_ids = jnp.arange(n_tiles, dtype=jnp.int32)
    tile_expert = jnp.minimum(jnp.sum((tile_ids[:, None] >= tile_end[None, :]).astype(jnp.int32), axis=1), N_EXPERTS - 1)
    tok = jnp.arange(n, dtype=jnp.int32)
    src = jnp.zeros((n_tiles * tm_g,), jnp.int32).at[jnp.concatenate([pos1, pos2])].set(
        jnp.concatenate([tok, tok]), unique_indices=True, mode="promise_in_bounds")
    return pos1, pos2, src, tile_expert.astype(jnp.int32), tile_end[-1:].astype(jnp.int32)


def _row_copy(src, src_row, dst, dst_row, sem):
    return pltpu.make_async_copy(src.at[pl.ds(src_row, 1), :], dst.at[pl.ds(dst_row, 1), :], sem)


def _gffn_body(te_ref, nu_ref, src_ref, h_hbm, w1_ref, w3_ref, w2_ref, ys_ref,
               xg, acc_ref, dsem, *, tm, tf):
    del te_ref
    m = pl.program_id(0)
    n_m = pl.num_programs(0)
    slot = m & 1
    other = 1 - slot
    n_used = nu_ref[0]
    used = m < n_used
    nf = w1_ref.shape[1] // tf
    per_f = tm // nf

    def wait_slot(s):
        pltpu.make_async_copy(xg.at[s], xg.at[s], dsem.at[s]).wait()

    @pl.when(m == 0)
    def _():
        def issue(t, carry):
            _row_copy(h_hbm, src_ref[t], xg.at[0], t, dsem.at[0]).start()
            return carry

        lax.fori_loop(0, tm, issue, 0, unroll=DMA_UNROLL)

    @pl.when(m <= n_used)
    def _():
        wait_slot(slot)

    @pl.when(used)
    def _():
        base = jnp.minimum(m + 1, n_m - 1) * tm
        h = xg[slot].astype(BF16)
        for c in range(nf):
            for t in range(per_f * c, per_f * (c + 1)):
                _row_copy(h_hbm, src_ref[base + t], xg.at[other], t, dsem.at[other]).start()
            a1 = _dot(h, w1_ref[:, tf * c:tf * (c + 1)])
            a3 = _dot(h, w3_ref[:, tf * c:tf * (c + 1)])
            part = _dot((a1 * _sigmoid(a1) * a3).astype(BF16), w2_ref[tf * c:tf * (c + 1), :])
            if c == 0:
                acc_ref[...] = part
            else:
                acc_ref[...] += part
        ys_ref[...] = acc_ref[...]

    @pl.when(jnp.logical_not(used))
    def _():
        ys_ref[...] = jnp.zeros_like(ys_ref)

    @pl.when(used & (m == n_m - 1))
    def _():
        wait_slot(other)


def _gffn(tile_expert, n_used, src, h, w1, w3, w2, layer, tm_g, tf):
    _, E, D, F = w1.shape
    n_tiles = src.shape[0] // tm_g
    resident = pl.Buffered(1)
    grid_spec = pltpu.PrefetchScalarGridSpec(
        num_scalar_prefetch=3, grid=(n_tiles,),
        in_specs=[pl.BlockSpec(memory_space=pl.ANY),
                  pl.BlockSpec((None, None, D, F), lambda m, te, nu, sr: (layer, te[m], 0, 0), pipeline_mode=resident),
                  pl.BlockSpec((None, None, D, F), lambda m, te, nu, sr: (layer, te[m], 0, 0), pipeline_mode=resident),
                  pl.BlockSpec((None, None, F, D), lambda m, te, nu, sr: (layer, te[m], 0, 0), pipeline_mode=resident)],
        out_specs=pl.BlockSpec((tm_g, D), lambda m, te, nu, sr: (m, 0)),
        scratch_shapes=[pltpu.VMEM((2, tm_g, D), F32), pltpu.VMEM((tm_g, D), F32), pltpu.SemaphoreType.DMA((2,))])
    return pl.pallas_call(
        functools.partial(_gffn_body, tm=tm_g, tf=tf), grid_spec=grid_spec,
        out_shape=jax.ShapeDtypeStruct((n_tiles * tm_g, D), F32),
        compiler_params=pltpu.CompilerParams(dimension_semantics=("arbitrary",),
                                             vmem_limit_bytes=VMEM_LIMIT),
        name="moe_gffn",
    )(tile_expert, n_used, src, h, w1, w3, w2)


def _moe_combine_body(pos_ref, ys_hbm, route_ref, x_ref, mod_ref, o_ref, yg, dsem, *, tm, n):
    i = pl.program_id(0)
    slot = i & 1

    def start_gather(tile, s):
        def issue(t, carry):
            tok = tile * tm + t
            _row_copy(ys_hbm, pos_ref[tok], yg.at[s, 0], t, dsem.at[s]).start()
            _row_copy(ys_hbm, pos_ref[n + tok], yg.at[s, 1], t, dsem.at[s]).start()
            return carry

        lax.fori_loop(0, tm, issue, 0, unroll=DMA_UNROLL)

    @pl.when(i == 0)
    def _():
        start_gather(0, 0)

    pltpu.make_async_copy(yg.at[slot], yg.at[slot], dsem.at[slot]).wait()

    @pl.when(i + 1 < pl.num_programs(0))
    def _():
        start_gather(i + 1, 1 - slot)

    route = route_ref[...]
    y = route[:, 2:3] * yg[slot, 0] + route[:, 3:4] * yg[slot, 1]
    o_ref[...] = x_ref[...] + mod_ref[5:6, :] * y


def _moe_combine(pos1, pos2, ys, route, xf, modtab, lay, tm, n_out):
    n, D = xf.shape
    n_lat_tiles = lay.n_lat // tm
    T, B = lay.T, lay.B
    row = lambda i, pos: (i, 0)
    grid_spec = pltpu.PrefetchScalarGridSpec(
        num_scalar_prefetch=1, grid=(n_out // tm,),
        in_specs=[pl.BlockSpec(memory_space=pl.ANY), pl.BlockSpec((tm, LANES), row), pl.BlockSpec((tm, D), row),
                  pl.BlockSpec((None, 6, D), lambda i, pos: (jnp.where(i < n_lat_tiles, (i * tm) // T, B), 0, 0))],
        out_specs=pl.BlockSpec((tm, D), row),
        scratch_shapes=[pltpu.VMEM((2, 2, tm, D), F32), pltpu.SemaphoreType.DMA((2,))])
    return pl.pallas_call(
        functools.partial(_moe_combine_body, tm=tm, n=n), grid_spec=grid_spec,
        out_shape=jax.ShapeDtypeStruct((n_out, D), F32),
        compiler_params=pltpu.CompilerParams(dimension_semantics=("arbitrary",), vmem_limit_bytes=VMEM_LIMIT),
        name="moe_combine",
    )(jnp.concatenate([pos1, pos2]), ys, route, xf, modtab)


def _ffn_body(x_ref, mod_ref, g_ref, w1_ref, w3_ref, w2_ref, o_ref, *, tf):
    x = x_ref[...]
    h = _norm_mod(x, g_ref[...], mod_ref[3:4, :], mod_ref[4:5, :]).astype(BF16)
    acc = None
    for c in range(w1_ref.shape[1] // tf):
        a1 = _dot(h, w1_ref[:, tf * c:tf * (c + 1)])
        a3 = _dot(h, w3_ref[:, tf * c:tf * (c + 1)])
        part = _dot((a1 * _sigmoid(a1) * a3).astype(BF16), w2_ref[tf * c:tf * (c + 1), :])
        acc = part if acc is None else acc + part
    o_ref[...] = x + mod_ref[5:6, :] * acc


def _rwkv_feat_body(*refs, vres, lay):
    if vres:
        (x_ref, vf_ref, xp_ref, xn_ref, mod_ref, g_ref, mix_ref, wr_ref, wk_ref, wv_ref, w1_ref, a1_ref, g1_ref,
         w2_ref, a2_ref, g2_ref, w0_ref, a0_ref, kkw_ref, ones_ref, v1_ref, v2_ref, v0_ref,
         r_ref, k_ref, lw_ref, ad_ref, kk_ref, vo_ref, gate_ref) = refs
    else:
        (x_ref, xp_ref, xn_ref, mod_ref, g_ref, mix_ref, wr_ref, wk_ref, wv_ref, w1_ref, a1_ref, g1_ref,
         w2_ref, a2_ref, g2_ref, w0_ref, a0_ref, kkw_ref, ones_ref,
         r_ref, k_ref, lw_ref, ad_ref, kk_ref, vo_ref, gate_ref) = refs
    tm = x_ref.shape[0]
    g = g_ref[...]
    shift = mod_ref[0:1, :]
    scale = mod_ref[1:2, :]
    h = _norm_mod(x_ref[...], g, shift, scale)
    h_before = _norm_mod(xp_ref[...], g, shift, scale)[HALO - 1:HALO, :]
    h_after = _norm_mod(xn_ref[...], g, shift, scale)[0:1, :]
    row = lax.broadcasted_iota(jnp.int32, (tm, 1), 0)
    gidx = pl.program_id(0) * tm + row
    in_lat = gidx < lay.n_lat
    pos = jnp.where(in_lat, gidx & (lay.T - 1), (gidx - lay.n_lat) & (lay.L - 1))
    last_pos = jnp.where(in_lat, lay.T - 1, lay.L - 1)
    prev = jnp.where(row == 0, h_before, pltpu.roll(h, 1, axis=0))
    prev = jnp.where(pos == 0, 0.0, prev)
    nxt = jnp.where(row == tm - 1, h_after, pltpu.roll(h, tm - 1, axis=0))
    nxt = jnp.where(pos == last_pos, 0.0, nxt)
    xx = 0.5 * (prev + nxt) - h

    def mixed(j):
        return (h + xx * mix_ref[j:j + 1, :]).astype(BF16)

    r_ref[...] = _dot(mixed(0), wr_ref[...]).astype(r_ref.dtype)
    tw = jnp.tanh(_dot(mixed(1), w1_ref[...]))
    dec = w0_ref[...] + _dot(tw.astype(BF16), w2_ref[...])
    lw_ref[...] = -DECAY_SCALE * _sigmoid(dec)
    k = _dot(mixed(2), wk_ref[...])
    k_ref[...] = k
    kkr = k * kkw_ref[...]
    ss = _head_sum(kkr * kkr, ones_ref[...])
    kk_ref[...] = (kkr / jnp.maximum(jnp.sqrt(ss), 1e-12)).astype(kk_ref.dtype)
    xv = mixed(3)
    v = _dot(xv, wv_ref[...])
    if vres:
        yv = _dot(xv, v1_ref[...])
        v = v + (vf_ref[...] - v) * _sigmoid(v0_ref[...] + _dot(yv.astype(BF16), v2_ref[...]))
    vo_ref[...] = v
    ya = _dot(mixed(4), a1_ref[...])
    ad_ref[...] = _sigmoid(a0_ref[...] + _dot(ya.astype(BF16), a2_ref[...])).astype(ad_ref.dtype)
    sgate = _sigmoid(_dot(mixed(5), g1_ref[...]))
    gate_ref[...] = _dot(sgate.astype(BF16), g2_ref[...]).astype(gate_ref.dtype)


def _scan_body(*refs):
    dir_refs = (refs[0:6], refs[6:12])
    ka_ref = refs[12]
    y_refs = refs[13:15]
    s_ref = refs[15]
    C = CHUNK
    D = ka_ref.shape[1]
    n_heads = D // C

    @pl.when(pl.program_id(1) == 0)
    def _():
        s_ref[...] = jnp.zeros_like(s_ref)

    ri = lax.broadcasted_iota(jnp.int32, (C, C), 0)
    ci = lax.broadcasted_iota(jnp.int32, (C, C), 1)
    xr = ri ^ ci
    eye = (ri == ci).astype(F32)
    r2 = lax.broadcasted_iota(jnp.int32, (C, 2 * C), 0)
    c2 = lax.broadcasted_iota(jnp.int32, (C, 2 * C), 1) & (C - 1)
    ka = ka_ref[...]

    ar, bk, vh, bkp, dph, strict, incl, where = [], [], [], [], [], [], [], []
    for d, (lw_ref, ad_ref, kf_ref, kk_ref, v_ref, r_ref) in enumerate(dir_refs):
        sg = 1 if d == 0 else -1
        strict_d = (r2 - c2) * sg > 0
        incl_d = (r2 - c2) * sg >= 0
        lw = lw_ref[...]
        ad = ad_ref[...].astype(F32)
        kk = kk_ref[...].astype(F32)
        c = jnp.dot(((ri - ci) * sg >= 0).astype(F32), lw, precision=HIGHEST, preferred_element_type=F32)
        pc = c[C - 1:C, :] if d == 0 else c[0:1, :]
        kd = kf_ref[...] * (1.0 + (ad - 1.0) * ka)
        b = kk * ad
        einv = jnp.exp(-c)
        a_t = (-kk * jnp.exp(c - lw)).astype(BF16)
        r_t = (r_ref[...].astype(F32) * jnp.exp(c)).astype(BF16)
        b_t = (b * einv).astype(BF16)
        k_t = (kd * einv).astype(BF16)
        epc = jnp.exp(pc - c)
        b_p = (b * epc).astype(BF16)
        k_p = (kd * epc).astype(BF16)
        d_p = jnp.exp(pc)
        vb = v_ref[...].astype(BF16)
        for h in range(n_heads):
            sl = slice(C * h, C * (h + 1))
            ar.append(jnp.concatenate([a_t[:, sl], r_t[:, sl]], axis=0))
            bk.append(jnp.concatenate([b_t[:, sl], k_t[:, sl]], axis=0))
            vh.append(vb[:, sl])
            bkp.append(jnp.concatenate([b_p[:, sl], k_p[:, sl]], axis=0))
            dph.append(d_p[:, sl])
            strict.append(strict_d)
            incl.append(incl_d)
            where.append((d, h, sl))

    items = range(len(where))
    sc = [lax.dot_general(ar[i], bk[i], _NT, preferred_element_type=F32) for i in items]
    st = [s_ref[d, h] for (d, h, _) in where]
    ars = [lax.dot_general(ar[i], st[i].astype(BF16), _NT, preferred_element_type=F32) for i in items]
    top = [jnp.where(strict[i], sc[i][0:C], 0.0) for i in items]
    bot = [jnp.where(incl[i], sc[i][C:], 0.0).astype(BF16) for i in items]
    lab = [top[i][:, 0:C] for i in items]
    x = [ars[i][0:C] + _dot(top[i][:, C:].astype(BF16), vh[i]) for i in items]
    t = [eye + jnp.where(xr == 1, lab[i], 0.0) for i in items]
    for lb in range(1, 6):
        level = (xr >> lb) == 1
        tb = [t[i].astype(BF16) for i in items]
        ot = [_dot(jnp.where(level, lab[i], 0.0).astype(BF16), tb[i]).astype(BF16) for i in items]
        t = [t[i] + _dot(tb[i], ot[i]) for i in items]
    u = [_dot(t[i].astype(BF16), x[i].astype(BF16)) for i in items]
    uv = [jnp.concatenate([u[i].astype(BF16), vh[i]], axis=0) for i in items]
    for i, (d, h, sl) in enumerate(where):
        y_refs[d][:, sl] = ars[i][C:] + _dot(bot[i], uv[i])
    for i, (d, h, sl) in enumerate(where):
        s_ref[d, h] = st[i] * dph[i] + lax.dot_general(uv[i], bkp[i], _TN, preferred_element_type=F32)


def _scan(lw, ad, kf, kk, v, r, k_a, lay):
    B, T, L, D = lay.B, lay.T, lay.L, lay.D
    nc_ctx = L // CHUNK
    nc_lat = T // CHUNK
    ctx_base = (B * T) // CHUNK

    def chunk_index(b, d, s):
        ctx_i = s if d == 0 else nc_ctx - 1 - s
        lat_i = s - nc_ctx if d == 0 else nc_lat - 1 - (s - nc_ctx)
        return jnp.where(s < nc_ctx, ctx_base + b * nc_ctx + ctx_i, b * nc_lat + lat_i)

    def spec(d, col):
        return pl.BlockSpec((CHUNK, D), lambda b, s: (chunk_index(b, d, s), col))

    in_specs, args = [], []
    for d in range(2):
        in_specs += [spec(d, d), spec(d, d), spec(d, 0), spec(d, 0), spec(d, 0), spec(d, 0)]
        args += [lw, ad, kf, kk, v, r]
    return pl.pallas_call(
        _scan_body, grid=(B, nc_ctx + nc_lat),
        in_specs=in_specs + [pl.BlockSpec((1, D), lambda b, s: (0, 0))],
        out_specs=[spec(0, 0), spec(1, 0)],
        out_shape=[jax.ShapeDtypeStruct((lay.n, D), F32), jax.ShapeDtypeStruct((lay.n, D), F32)],
        scratch_shapes=[pltpu.VMEM((2, D // CHUNK, CHUNK, CHUNK), F32)],
        compiler_params=pltpu.CompilerParams(dimension_semantics=("parallel", "arbitrary"),
                                             vmem_limit_bytes=VMEM_LIMIT),
        name="wkv_scan",
    )(*args, k_a)


def _rwkv_post_body(yf_ref, yr_ref, r_ref, kf_ref, ad_ref, v_ref, g_ref, x_ref, mod_ref,
                    ka_ref, rk_ref, lnw_ref, lnb_ref, ones_ref, wo_ref, o_ref):
    D = x_ref.shape[1]
    ones4 = ones_ref[...]
    y = yf_ref[...] + yr_ref[...]
    inv_n = 1.0 / RWKV_HEAD
    yc = y - _head_sum(y, ones4) * inv_n
    var = _head_sum(yc * yc, ones4) * inv_n
    yn = yc * lax.rsqrt(var + LN_X_EPS) * lnw_ref[...] + lnb_ref[...]
    kf = kf_ref[...]
    ad = ad_ref[...].astype(F32)
    ka = ka_ref[...]
    k_bonus = 0.5 * (kf * (1.0 + (ad[:, :D] - 1.0) * ka) + kf * (1.0 + (ad[:, D:] - 1.0) * ka))
    bonus = _head_sum(r_ref[...].astype(F32) * k_bonus * rk_ref[...], ones4) * v_ref[...]
    o = ((yn + bonus) * g_ref[...].astype(F32)).astype(BF16)
    o_ref[...] = x_ref[...] + mod_ref[2:3, :] * _dot(o, wo_ref[...])


def _rope_tables(T, tm):
    pos = jnp.arange(T)
    inv_freq = 1.0 / (ROPE_BASE ** (jnp.arange(AXIS_PAIRS, dtype=F32) / AXIS_PAIRS))
    ang_r = (pos // GRID_W).astype(F32)[:, None] * inv_freq[None, :]
    ang_c = (pos % GRID_W).astype(F32)[:, None] * inv_freq[None, :]
    pad1 = jnp.ones((T, LANES - QK_ROPE), F32)
    pad0 = jnp.zeros((T, LANES - QK_ROPE), F32)
    cos = jnp.concatenate([jnp.cos(ang_r), jnp.cos(ang_r), jnp.cos(ang_c), jnp.cos(ang_c), pad1], axis=1)
    sin = jnp.concatenate([-jnp.sin(ang_r), jnp.sin(ang_r), -jnp.sin(ang_c), jnp.sin(ang_c), pad0], axis=1)
    cos = jnp.concatenate([cos, jnp.ones((tm, LANES), F32)], axis=0)
    sin = jnp.concatenate([sin, jnp.zeros((tm, LANES), F32)], axis=0)
    return cos, sin


def _pad_cols(w, width):
    return jnp.pad(w, ((0, 0), (0, width - w.shape[1])))


def _pad_rows(w, height):
    return jnp.pad(w, ((0, height - w.shape[0]), (0, 0)))


def _block_diag2(w0, w1):
    z0 = jnp.zeros_like(w0)
    z1 = jnp.zeros_like(w1)
    return jnp.concatenate([jnp.concatenate([w0, z1], axis=1), jnp.concatenate([z0, w1], axis=1)], axis=0)


def kernel(x, c, ctx, c_ctx, ada_w, ada_b, norm1_g, norm2_g, mla_wqa, mla_qa_norm, mla_wqb, mla_wkva, mla_kva_norm, mla_wkvb, mla_q_norm, mla_k_norm, mla_wo, rwkv_mix, rwkv_wr, rwkv_wk, rwkv_wv, rwkv_wo, rwkv_w0, rwkv_w1, rwkv_w2, rwkv_a0, rwkv_a1, rwkv_a2, rwkv_g1, rwkv_g2, rwkv_k_k, rwkv_k_a, rwkv_r_k, rwkv_ln_w, rwkv_ln_b, rwkv_v0, rwkv_v1, rwkv_v2, ffn_w1, ffn_w3, ffn_w2, moe_router, moe_w1, moe_w3, moe_w2):
    B, T, D = x.shape
    L = ctx.shape[1]
    depth = ada_w.shape[0]
    lay = _Layout(B, T, L, D)
    n = lay.n
    tm = min(512, T)
    tm_small = min(256, T)
    assert T % tm == 0 and (B * L) % tm == 0 and T % GRID_W == 0 and L % CHUNK == 0 and B + 1 <= 16
    assert T & (T - 1) == 0 and L & (L - 1) == 0 and L % tm_small == 0

    ffn_b = [w.astype(BF16) for w in (ffn_w1, ffn_w3, ffn_w2)]
    moe_b = [w.astype(BF16) for w in (moe_w1, moe_w3, moe_w2)]
    xf = jnp.concatenate([x.reshape(B * T, D), ctx.reshape(B * L, D)], axis=0)
    cc = jnp.concatenate([c, c_ctx[None, :], jnp.zeros((16 - B - 1, D), F32)], axis=0)
    modall = _ada_call(cc, ada_w, ada_b)[:, :B + 1].reshape(depth, B + 1, 6, D)
    cos_t, sin_t = _rope_tables(T, tm)
    ones4 = jnp.kron(jnp.eye(4, dtype=F32), jnp.ones((RWKV_HEAD, RWKV_HEAD), F32)).astype(BF16)
    row1 = lambda z: z.reshape(1, -1)

    v_first = None
    for i in range(depth):
        j = i // 2
        modtab = modall[i]
        mod_t = (modtab, lay.mod_spec(tm))
        mod_s = (modtab, lay.mod_spec(tm_small))
        if i % 2 == 0:
            wa = jnp.concatenate([mla_wqa[j], _pad_cols(mla_wkva[j], KV_LORA + LANES)], axis=1).astype(BF16)
            wqb = mla_wqb[j].reshape(Q_LORA, MLA_HEADS, QK_HEAD)
            wqb = jnp.pad(wqb, ((0, 0), (0, 0), (0, QK_PAD - QK_HEAD))).reshape(Q_LORA, MLA_HEADS * QK_PAD).astype(BF16)
            wkvb = mla_wkvb[j].reshape(KV_LORA, MLA_HEADS, QK_NOPE + V_HEAD)
            wk = wkvb[:, :, :QK_NOPE].reshape(KV_LORA, MLA_HEADS * QK_NOPE).astype(BF16)
            wv = wkvb[:, :, QK_NOPE:].reshape(KV_LORA, MLA_HEADS * V_HEAD).astype(BF16)
            gq = _pad_cols(row1(mla_q_norm[j]) * SM_SCALE, QK_PAD)
            gk = _pad_cols(row1(mla_k_norm[j]), QK_PAD)
            cq, ckv, kr = _rowwise(
                _mla_a_body, "mla_a", n, tm, [xf], [mod_t],
                [row1(norm1_g[i]), wa, row1(mla_qa_norm[j]), row1(mla_kva_norm[j])],
                [(Q_LORA, BF16), (KV_LORA, BF16), (LANES, F32)])
            rope_in = [(cos_t, lay.rope_spec(tm)), (sin_t, lay.rope_spec(tm))]
            (q,) = _rowwise(_mla_q_body, "mla_q", n, tm, [cq], rope_in, [wqb, gq], [(MLA_HEADS * QK_PAD, BF16)])
            k, v = _rowwise(_mla_kv_body, "mla_kv", n, tm, [ckv, kr], rope_in, [wk, wv, gk],
                            [(MLA_HEADS * QK_PAD, BF16), (MLA_HEADS * V_PAD, BF16)])
            o = _attention(q, k, v, lay, tq=min(2048, T))
            (xf,) = _rowwise(_proj_resid_body, "mla_out", n, tm, [o, xf], [mod_t], [mla_wo[j].astype(BF16)], [(D, F32)])
        else:
            vres = j > 0
            mix = _pad_rows(rwkv_mix[j], 8)
            w1c = jnp.concatenate([rwkv_w1[j, 0], rwkv_w1[j, 1]], axis=1).astype(BF16)
            a1c = jnp.concatenate([rwkv_a1[j, 0], rwkv_a1[j, 1]], axis=1).astype(BF16)
            g1p = _pad_cols(rwkv_g1[j], 2 * LANES).astype(BF16)
            w2c = _block_diag2(rwkv_w2[j, 0], rwkv_w2[j, 1]).astype(BF16)
            a2c = _block_diag2(rwkv_a2[j, 0], rwkv_a2[j, 1]).astype(BF16)
            g2p = _pad_rows(rwkv_g2[j], 2 * LANES).astype(BF16)
            consts = [row1(norm1_g[i]), mix, rwkv_wr[j].astype(BF16), rwkv_wk[j].astype(BF16),
                      rwkv_wv[j].astype(BF16), w1c, a1c, g1p, w2c, a2c, g2p,
                      rwkv_w0[j].reshape(1, 2 * D), rwkv_a0[j].reshape(1, 2 * D), row1(rwkv_k_k[j]), ones4]
            rows = [xf]
            if vres:
                rows.append(v_first)
                consts += [_pad_cols(rwkv_v1[j - 1], LANES).astype(BF16), _pad_rows(rwkv_v2[j - 1], LANES).astype(BF16),
                           row1(rwkv_v0[j - 1])]
            halo_per_tile = tm_small // HALO
            before = pl.BlockSpec((HALO, D), lambda t: (jnp.maximum(t * halo_per_tile - 1, 0), 0))
            after = pl.BlockSpec((HALO, D), lambda t: (jnp.minimum((t + 1) * halo_per_tile, n // HALO - 1), 0))
            r, kf, lw, ad, kk, v_out, gate = _rowwise(
                functools.partial(_rwkv_feat_body, vres=vres, lay=lay), "rwkv_feat", n, tm_small,
                rows, [(xf, before), (xf, after), mod_s], consts,
                [(D, BF16), (D, F32), (2 * D, F32), (2 * D, BF16), (D, BF16), (D, F32), (D, BF16)])
            if j == 0:
                v_first = v_out
            k_a = row1(rwkv_k_a[j])
            y_fwd, y_rev = _scan(lw, ad, kf, kk, v_out, r, k_a, lay)
            (xf,) = _rowwise(
                _rwkv_post_body, "rwkv_out", n, tm_small, [y_fwd, y_rev, r, kf, ad, v_out, gate, xf], [mod_s],
                [k_a, row1(rwkv_r_k[j]), row1(rwkv_ln_w[j]), row1(rwkv_ln_b[j]), ones4, rwkv_wo[j].astype(BF16)],
                [(D, F32)])

        if i % 2 == 0:
            (xf,) = _rowwise(functools.partial(_ffn_body, tf=ffn_w1.shape[2] // 2), "dense_ffn", n, tm, [xf], [mod_t],
                             [row1(norm2_g[i]), ffn_b[0][j], ffn_b[1][j], ffn_b[2][j]], [(D, F32)])
        else:
            h, route = _rowwise(_router_body, "router", n, tm, [xf], [mod_t],
                                [row1(norm2_g[i]), _pad_cols(moe_router[j], LANES)], [(D, F32), (LANES, F32)])
            pos1, pos2, src, tile_expert, n_used = _route_plan(route, n, MOE_TM)
            ys = _gffn(tile_expert, n_used, src, h, moe_b[0], moe_b[1], moe_b[2], j, MOE_TM, tf=moe_w1.shape[3] // 4)
            n_out = lay.n_lat if i == depth - 1 else n
            xf = _moe_combine(pos1, pos2, ys, route, xf, modtab, lay, tm, n_out)
    return xf[:B * T].reshape(B, T, D)
```

```python
import functools

import jax
import jax.numpy as jnp
from jax import lax
from jax.experimental import pallas as pl
from jax.experimental.pallas import tpu as pltpu

F32 = jnp.float32
BF16 = jnp.bfloat16
HIGHEST = lax.Precision.HIGHEST

GRID_W = 64
NORM_EPS = 1e-6
MLA_HEADS = 8
QK_NOPE = 128
QK_ROPE = 64
V_HEAD = 128
QK_HEAD = QK_NOPE + QK_ROPE
Q_LORA = 384
KV_LORA = 256
ROPE_BASE = 10000.0
AXIS_PAIRS = QK_ROPE // 4
SM_SCALE = QK_HEAD ** -0.5
RWKV_HEAD = 64
LN_X_EPS = 64e-5
DECAY_SCALE = 0.6065306597126334
N_EXPERTS = 8

LANES = 128
QK_PAD = 2 * LANES
V_PAD = 2 * LANES
CHUNK = 64
MOE_TM = 512
FFN_CHUNK = 256
ATTN_SUB = 512
DMA_UNROLL = 8
HALO = 8
VMEM_LIMIT = 56 * 1024 * 1024

_NT = (((1,), (1,)), ((), ()))
_TN = (((0,), (0,)), ((), ()))


def _dot(a, b):
    return jnp.dot(a, b, preferred_element_type=F32)


def _sigmoid(z):
    return 1.0 / (1.0 + jnp.exp(-z))


def _rms(z, width):
    return lax.rsqrt(jnp.sum(z * z, axis=-1, keepdims=True) * (1.0 / width) + NORM_EPS)


def _norm_mod(x, g, shift, scale):
    return (x * _rms(x, x.shape[-1])) * g * (1.0 + scale) + shift


def _rope(z, cos, sin):
    lane = lax.broadcasted_iota(jnp.int32, z.shape, 1)
    partner = jnp.where((lane & 16) == 0, pltpu.roll(z, LANES - 16, axis=1), pltpu.roll(z, 16, axis=1))
    return z * cos + partner * sin


def _head_sum(z, ones4):
    outs = []
    for q in range(z.shape[1] // 256):
        blk = z[:, 256 * q:256 * (q + 1)]
        hi = blk.astype(BF16)
        lo = (blk - hi.astype(F32)).astype(BF16)
        outs.append(_dot(hi, ones4) + _dot(lo, ones4))
    return jnp.concatenate(outs, axis=1)


class _Layout:
    def __init__(self, B, T, L, D):
        self.B, self.T, self.L, self.D = B, T, L, D
        self.n_lat = B * T
        self.n = B * T + B * L

    def mod_spec(self, tm):
        n_lat_tiles = self.n_lat // tm
        T, B, D = self.T, self.B, self.D
        return pl.BlockSpec((None, 6, D), lambda i: (jnp.where(i < n_lat_tiles, (i * tm) // T, B), 0, 0))

    def rope_spec(self, tm):
        n_lat_tiles = self.n_lat // tm
        tpb = self.T // tm
        return pl.BlockSpec((tm, LANES), lambda i: (jnp.where(i < n_lat_tiles, i % tpb, tpb), 0))


def _const_spec(arr):
    nd = arr.ndim
    return pl.BlockSpec(arr.shape, lambda i: (0,) * nd, pipeline_mode=pl.Buffered(1))


def _rowwise(body, name, n_rows, tm, row_ins, tile_ins, const_ins, outs):
    in_specs = [pl.BlockSpec((tm, a.shape[1]), lambda i: (i, 0)) for a in row_ins]
    in_specs += [spec for (_, spec) in tile_ins]
    in_specs += [_const_spec(a) for a in const_ins]
    out_specs = [pl.BlockSpec((tm, w), lambda i: (i, 0)) for (w, _) in outs]
    out_shape = [jax.ShapeDtypeStruct((n_rows, w), dt) for (w, dt) in outs]
    return pl.pallas_call(
        body, grid=(n_rows // tm,), in_specs=in_specs, out_specs=out_specs, out_shape=out_shape,
        compiler_params=pltpu.CompilerParams(dimension_semantics=("parallel",), vmem_limit_bytes=VMEM_LIMIT),
        name=name,
    )(*row_ins, *[a for (a, _) in tile_ins], *const_ins)


def _ada_body(c_ref, w_ref, b_ref, o_ref):
    c = c_ref[...]
    o_ref[...] = jnp.dot(c * _sigmoid(c), w_ref[...], precision=HIGHEST, preferred_element_type=F32) + b_ref[...]


def _ada_call(cc, ada_w, ada_b):
    depth, D, W = ada_w.shape
    tn = 1024
    return pl.pallas_call(
        _ada_body, grid=(depth, W // tn),
        in_specs=[pl.BlockSpec(cc.shape, lambda l, j: (0, 0)),
                  pl.BlockSpec((None, D, tn), lambda l, j: (l, 0, j)),
                  pl.BlockSpec((None, 1, tn), lambda l, j: (l, 0, j))],
        out_specs=pl.BlockSpec((None, cc.shape[0], tn), lambda l, j: (l, 0, j)),
        out_shape=jax.ShapeDtypeStruct((depth, cc.shape[0], W), F32),
        compiler_params=pltpu.CompilerParams(dimension_semantics=("parallel", "parallel"), vmem_limit_bytes=VMEM_LIMIT),
        name="ada",
    )(cc, ada_w, ada_b.reshape(depth, 1, W))


def _mla_a_body(x_ref, mod_ref, g_ref, wa_ref, qan_ref, kvn_ref, cq_ref, ckv_ref, kr_ref):
    h = _norm_mod(x_ref[...], g_ref[...], mod_ref[0:1, :], mod_ref[1:2, :]).astype(BF16)
    acc = _dot(h, wa_ref[...])
    qa = acc[:, 0:Q_LORA]
    kv = acc[:, Q_LORA:Q_LORA + KV_LORA]
    cq_ref[...] = (qa * _rms(qa, Q_LORA) * qan_ref[...]).astype(BF16)
    ckv_ref[...] = (kv * _rms(kv, KV_LORA) * kvn_ref[...]).astype(BF16)
    kr_ref[...] = acc[:, Q_LORA + KV_LORA:]


def _mla_q_body(cq_ref, cos_ref, sin_ref, wqb_ref, gq_ref, q_ref):
    acc = _dot(cq_ref[...], wqb_ref[...])
    cos = cos_ref[...]
    sin = sin_ref[...]
    g = gq_ref[...]
    for h in range(MLA_HEADS):
        a = acc[:, QK_PAD * h:QK_PAD * (h + 1)]
        an = a * _rms(a, QK_HEAD) * g
        q_ref[:, QK_PAD * h:QK_PAD * h + LANES] = an[:, :LANES].astype(BF16)
        q_ref[:, QK_PAD * h + LANES:QK_PAD * (h + 1)] = _rope(an[:, LANES:], cos, sin).astype(BF16)


def _mla_kv_body(ckv_ref, kr_ref, cos_ref, sin_ref, wk_ref, wv_ref, gk_ref, k_ref, v_ref):
    ckv = ckv_ref[...]
    kn = _dot(ckv, wk_ref[...])
    vals = _dot(ckv, wv_ref[...]).astype(BF16)
    lane = lax.broadcasted_iota(jnp.int32, (vals.shape[0], LANES), 1)
    ones_col = jnp.where(lane == 0, 1.0, 0.0).astype(BF16)
    for h in range(MLA_HEADS):
        v_ref[:, V_PAD * h:V_PAD * h + V_HEAD] = vals[:, V_HEAD * h:V_HEAD * (h + 1)]
        v_ref[:, V_PAD * h + V_HEAD:V_PAD * (h + 1)] = ones_col
    kr = kr_ref[...]
    ss_rope = jnp.sum(kr * kr, axis=-1, keepdims=True)
    g = gk_ref[...]
    g_nope = g[:, :LANES]
    kr_rot = _rope(kr * g[:, LANES:], cos_ref[...], sin_ref[...])
    for h in range(MLA_HEADS):
        a = kn[:, LANES * h:LANES * (h + 1)]
        inv = lax.rsqrt((jnp.sum(a * a, axis=-1, keepdims=True) + ss_rope) * (1.0 / QK_HEAD) + NORM_EPS)
        k_ref[:, QK_PAD * h:QK_PAD * h + LANES] = (a * inv * g_nope).astype(BF16)
        k_ref[:, QK_PAD * h + LANES:QK_PAD * (h + 1)] = (kr_rot * inv).astype(BF16)


def _attn_body(*refs, nseg, sub):
    q_ref = refs[0]
    o_ref = refs[-1]
    n_sub = q_ref.shape[0] // sub

    def scores(j):
        q = q_ref[sub * j:sub * (j + 1), :]
        return [lax.dot_general(q, refs[1 + 2 * i][...], _NT, preferred_element_type=F32) for i in range(nseg)]

    def finish(j, ss):
        m = functools.reduce(jnp.maximum, [jnp.max(s, axis=-1, keepdims=True) for s in ss])
        acc = functools.reduce(lambda a, b: a + b,
                               [_dot(jnp.exp((ss[i] - m).astype(BF16)), refs[2 + 2 * i][...]) for i in range(nseg)])
        o_ref[sub * j:sub * (j + 1), :] = (acc[:, :V_HEAD] / acc[:, V_HEAD:V_HEAD + 1]).astype(o_ref.dtype)

    pending = scores(0)
    for j in range(n_sub):
        following = scores(j + 1) if j + 1 < n_sub else None
        finish(j, pending)
        pending = following


def _attention(q, k, v, lay, tq):
    B, T, L = lay.B, lay.T, lay.L
    nq = T // tq
    ctx0 = (B * T) // L
    params = pltpu.CompilerParams(dimension_semantics=("parallel", "parallel", "arbitrary"),
                                  vmem_limit_bytes=VMEM_LIMIT)
    o_lat = pl.pallas_call(
        functools.partial(_attn_body, nseg=2, sub=min(ATTN_SUB, tq)), grid=(B, MLA_HEADS, nq),
        in_specs=[pl.BlockSpec((tq, QK_PAD), lambda b, h, i: (b * nq + i, h)),
                  pl.BlockSpec((L, QK_PAD), lambda b, h, i: (ctx0 + b, h)),
                  pl.BlockSpec((L, V_PAD), lambda b, h, i: (ctx0 + b, h)),
                  pl.BlockSpec((T, QK_PAD), lambda b, h, i: (b, h)),
                  pl.BlockSpec((T, V_PAD), lambda b, h, i: (b, h))],
        out_specs=pl.BlockSpec((tq, V_HEAD), lambda b, h, i: (b * nq + i, h)),
        out_shape=jax.ShapeDtypeStruct((lay.n, MLA_HEADS * V_HEAD), BF16),
        compiler_params=params, name="attn_latent",
    )(q, k, v, k, v)
    return pl.pallas_call(
        functools.partial(_attn_body, nseg=1, sub=min(ATTN_SUB, L)), grid=(B, MLA_HEADS, 1),
        in_specs=[pl.BlockSpec((L, QK_PAD), lambda b, h, i: (ctx0 + b, h)),
                  pl.BlockSpec((L, QK_PAD), lambda b, h, i: (ctx0 + b, h)),
                  pl.BlockSpec((L, V_PAD), lambda b, h, i: (ctx0 + b, h)),
                  pl.BlockSpec(memory_space=pl.ANY)],
        out_specs=pl.BlockSpec((L, V_HEAD), lambda b, h, i: (ctx0 + b, h)),
        out_shape=jax.ShapeDtypeStruct((lay.n, MLA_HEADS * V_HEAD), BF16),
        input_output_aliases={3: 0},
        compiler_params=params, name="attn_ctx",
    )(q, k, v, o_lat)


def _proj_resid_body(a_ref, x_ref, mod_ref, w_ref, o_ref):
    o_ref[...] = x_ref[...] + mod_ref[2:3, :] * _dot(a_ref[...], w_ref[...])


def _router_body(x_ref, mod_ref, g_ref, router_ref, h_ref, route_ref):
    hf = _norm_mod(x_ref[...], g_ref[...], mod_ref[3:4, :], mod_ref[4:5, :])
    h_ref[...] = hf
    logits = jnp.dot(hf, router_ref[...], precision=HIGHEST, preferred_element_type=F32)
    lane = lax.broadcasted_iota(jnp.int32, logits.shape, 1)
    lanef = lane.astype(F32)
    neg = jnp.float32(-1e30)
    lg = jnp.where(lane < N_EXPERTS, logits, neg)
    m1 = jnp.max(lg, axis=-1, keepdims=True)
    i1 = jnp.min(jnp.where(lg == m1, lanef, float(LANES)), axis=-1, keepdims=True)
    lg2 = jnp.where(lanef == i1, neg, lg)
    m2 = jnp.max(lg2, axis=-1, keepdims=True)
    i2 = jnp.min(jnp.where(lg2 == m2, lanef, float(LANES)), axis=-1, keepdims=True)
    e2 = jnp.exp(m2 - m1)
    w1 = 1.0 / (1.0 + e2)
    w2 = e2 / (1.0 + e2)
    route_ref[...] = jnp.where(lane == 0, i1, jnp.where(lane == 1, i2, jnp.where(lane == 2, w1, jnp.where(lane == 3, w2, 0.0))))


def _route_plan(route, n, tm_g):
    i1 = route[:, 0].astype(jnp.int32)
    i2 = route[:, 1].astype(jnp.int32)
    experts = jnp.arange(N_EXPERTS, dtype=jnp.int32)
    onehot = ((i1[:, None] == experts) | (i2[:, None] == experts)).astype(jnp.int32)
    rank = jnp.cumsum(onehot, axis=0) - onehot
    tiles_e = (jnp.sum(onehot, axis=0) + tm_g - 1) // tm_g
    tile_end = jnp.cumsum(tiles_e)
    start = (tile_end - tiles_e) * tm_g
    pos1 = start[i1] + jnp.take_along_axis(rank, i1[:, None], axis=1)[:, 0]
    pos2 = start[i2] + jnp.take_along_axis(rank, i2[:, None], axis=1)[:, 0]
    n_tiles = -(-2 * n // tm_g) + N_EXPERTS
    tile_ids = jnp.arange(n_tiles, dtype=jnp.int32)
    tile_expert = jnp.minimum(jnp.sum((tile_ids[:, None] >= tile_end[None, :]).astype(jnp.int32), axis=1), N_EXPERTS - 1)
    tok = jnp.arange(n, dtype=jnp.int32)
    src = jnp.zeros((n_tiles * tm_g,), jnp.int32).at[jnp.concatenate([pos1, pos2])].set(
        jnp.concatenate([tok, tok]), unique_indices=True, mode="promise_in_bounds")
    return pos1, pos2, src, tile_expert.astype(jnp.int32), tile_end[-1:].astype(jnp.int32)


def _row_copy(src, src_row, dst, dst_row, sem):
    return pltpu.make_async_copy(src.at[pl.ds(src_row, 1), :], dst.at[pl.ds(dst_row, 1), :], sem)


def _gffn_body(te_ref, nu_ref, src_ref, h_hbm, w1_ref, w3_ref, w2_ref, ys_ref,
               xg, acc_ref, dsem, *, tm, tf):
    del te_ref
    m = pl.program_id(0)
    n_m = pl.num_programs(0)
    slot = m & 1
    other = 1 - slot
    n_used = nu_ref[0]
    used = m < n_used
    nf = w1_ref.shape[1] // tf

    def wait_slot(s):
        pltpu.make_async_copy(xg.at[s], xg.at[s], dsem.at[s]).wait()

    @pl.when(m == 0)
    def _():
        def issue(t, carry):
            _row_copy(h_hbm, src_ref[t], xg.at[0], t, dsem.at[0]).start()
            return carry

        lax.fori_loop(0, tm, issue, 0, unroll=DMA_UNROLL)

    @pl.when(m <= n_used)
    def _():
        wait_slot(slot)

    @pl.when(used)
    def _():
        base = jnp.minimum(m + 1, n_m - 1) * tm
        h = xg[slot].astype(BF16)
        for c in range(nf):
            for t in range(tm * c // nf, tm * (c + 1) // nf):
                _row_copy(h_hbm, src_ref[base + t], xg.at[other], t, dsem.at[other]).start()
            a1 = _dot(h, w1_ref[:, tf * c:tf * (c + 1)])
            a3 = _dot(h, w3_ref[:, tf * c:tf * (c + 1)])
            part = _dot((a1 * _sigmoid(a1) * a3).astype(BF16), w2_ref[tf * c:tf * (c + 1), :])
            if c == 0:
                acc_ref[...] = part
            else:
                acc_ref[...] += part
        ys_ref[...] = acc_ref[...]

    @pl.when(jnp.logical_not(used))
    def _():
        ys_ref[...] = jnp.zeros_like(ys_ref)

    @pl.when(used & (m == n_m - 1))
    def _():
        wait_slot(other)


def _gffn(tile_expert, n_used, src, h, w1, w3, w2, layer, tm_g, tf):
    _, E, D, F = w1.shape
    n_tiles = src.shape[0] // tm_g
    resident = pl.Buffered(1)
    grid_spec = pltpu.PrefetchScalarGridSpec(
        num_scalar_prefetch=3, grid=(n_tiles,),
        in_specs=[pl.BlockSpec(memory_space=pl.ANY),
                  pl.BlockSpec((None, None, D, F), lambda m, te, nu, sr: (layer, te[m], 0, 0), pipeline_mode=resident),
                  pl.BlockSpec((None, None, D, F), lambda m, te, nu, sr: (layer, te[m], 0, 0), pipeline_mode=resident),
                  pl.BlockSpec((None, None, F, D), lambda m, te, nu, sr: (layer, te[m], 0, 0), pipeline_mode=resident)],
        out_specs=pl.BlockSpec((tm_g, D), lambda m, te, nu, sr: (m, 0)),
        scratch_shapes=[pltpu.VMEM((2, tm_g, D), F32), pltpu.VMEM((tm_g, D), F32), pltpu.SemaphoreType.DMA((2,))])
    return pl.pallas_call(
        functools.partial(_gffn_body, tm=tm_g, tf=tf), grid_spec=grid_spec,
        out_shape=jax.ShapeDtypeStruct((n_tiles * tm_g, D), F32),
        compiler_params=pltpu.CompilerParams(dimension_semantics=("arbitrary",),
                                             vmem_limit_bytes=VMEM_LIMIT),
        name="moe_gffn",
    )(tile_expert, n_used, src, h, w1, w3, w2)


def _moe_combine_body(pos_ref, ys_hbm, route_ref, x_ref, mod_ref, o_ref, yg, dsem, *, tm, n):
    i = pl.program_id(0)
    slot = i & 1

    def start_gather(tile, s):
        def issue(t, carry):
            tok = tile * tm + t
            _row_copy(ys_hbm, pos_ref[tok], yg.at[s, 0], t, dsem.at[s]).start()
            _row_copy(ys_hbm, pos_ref[n + tok], yg.at[s, 1], t, dsem.at[s]).start()
            return carry

        lax.fori_loop(0, tm, issue, 0, unroll=DMA_UNROLL)

    @pl.when(i == 0)
    def _():
        start_gather(0, 0)

    pltpu.make_async_copy(yg.at[slot], yg.at[slot], dsem.at[slot]).wait()

    @pl.when(i + 1 < pl.num_programs(0))
    def _():
        start_gather(i + 1, 1 - slot)

    route = route_ref[...]
    y = route[:, 2:3] * yg[slot, 0] + route[:, 3:4] * yg[slot, 1]
    o_ref[...] = x_ref[...] + mod_ref[5:6, :] * y


def _moe_combine(pos1, pos2, ys, route, xf, modtab, lay, tm, n_out):
    n, D = xf.shape
    n_lat_tiles = lay.n_lat // tm
    T, B = lay.T, lay.B
    row = lambda i, pos: (i, 0)
    grid_spec = pltpu.PrefetchScalarGridSpec(
        num_scalar_prefetch=1, grid=(n_out // tm,),
        in_specs=[pl.BlockSpec(memory_space=pl.ANY), pl.BlockSpec((tm, LANES), row), pl.BlockSpec((tm, D), row),
                  pl.BlockSpec((None, 6, D), lambda i, pos: (jnp.where(i < n_lat_tiles, (i * tm) // T, B), 0, 0))],
        out_specs=pl.BlockSpec((tm, D), row),
        scratch_shapes=[pltpu.VMEM((2, 2, tm, D), F32), pltpu.SemaphoreType.DMA((2,))])
    return pl.pallas_call(
        functools.partial(_moe_combine_body, tm=tm, n=n), grid_spec=grid_spec,
        out_shape=jax.ShapeDtypeStruct((n_out, D), F32),
        compiler_params=pltpu.CompilerParams(dimension_semantics=("arbitrary",), vmem_limit_bytes=VMEM_LIMIT),
        name="moe_combine",
    )(jnp.concatenate([pos1, pos2]), ys, route, xf, modtab)


def _ffn_body(x_ref, mod_ref, g_ref, w1_ref, w3_ref, w2_ref, o_ref, *, tf):
    x = x_ref[...]
    h = _norm_mod(x, g_ref[...], mod_ref[3:4, :], mod_ref[4:5, :]).astype(BF16)
    acc = None
    for c in range(w1_ref.shape[1] // tf):
        a1 = _dot(h, w1_ref[:, tf * c:tf * (c + 1)])
        a3 = _dot(h, w3_ref[:, tf * c:tf * (c + 1)])
        part = _dot((a1 * _sigmoid(a1) * a3).astype(BF16), w2_ref[tf * c:tf * (c + 1), :])
        acc = part if acc is None else acc + part
    o_ref[...] = x + mod_ref[5:6, :] * acc


def _rwkv_feat_body(*refs, vres, lay):
    if vres:
        (x_ref, vf_ref, xp_ref, xn_ref, mod_ref, g_ref, mix_ref, wr_ref, wk_ref, wv_ref, w1_ref, a1_ref, g1_ref,
         w2_ref, a2_ref, g2_ref, w0_ref, a0_ref, kkw_ref, ones_ref, v1_ref, v2_ref, v0_ref,
         r_ref, k_ref, lw_ref, ad_ref, kk_ref, vo_ref, gate_ref) = refs
    else:
        (x_ref, xp_ref, xn_ref, mod_ref, g_ref, mix_ref, wr_ref, wk_ref, wv_ref, w1_ref, a1_ref, g1_ref,
         w2_ref, a2_ref, g2_ref, w0_ref, a0_ref, kkw_ref, ones_ref,
         r_ref, k_ref, lw_ref, ad_ref, kk_ref, vo_ref, gate_ref) = refs
    tm = x_ref.shape[0]
    g = g_ref[...]
    shift = mod_ref[0:1, :]
    scale = mod_ref[1:2, :]
    h = _norm_mod(x_ref[...], g, shift, scale)
    h_before = _norm_mod(xp_ref[...], g, shift, scale)[HALO - 1:HALO, :]
    h_after = _norm_mod(xn_ref[...], g, shift, scale)[0:1, :]
    row = lax.broadcasted_iota(jnp.int32, (tm, 1), 0)
    gidx = pl.program_id(0) * tm + row
    in_lat = gidx < lay.n_lat
    pos = jnp.where(in_lat, gidx & (lay.T - 1), (gidx - lay.n_lat) & (lay.L - 1))
    last_pos = jnp.where(in_lat, lay.T - 1, lay.L - 1)
    prev = jnp.where(row == 0, h_before, pltpu.roll(h, 1, axis=0))
    prev = jnp.where(pos == 0, 0.0, prev)
    nxt = jnp.where(row == tm - 1, h_after, pltpu.roll(h, tm - 1, axis=0))
    nxt = jnp.where(pos == last_pos, 0.0, nxt)
    xx = 0.5 * (prev + nxt) - h

    def mixed(j):
        return (h + xx * mix_ref[j:j + 1, :]).astype(BF16)

    r_ref[...] = _dot(mixed(0), wr_ref[...]).astype(r_ref.dtype)
    tw = jnp.tanh(_dot(mixed(1), w1_ref[...]))
    dec = w0_ref[...] + _dot(tw.astype(BF16), w2_ref[...])
    lw_ref[...] = -DECAY_SCALE * _sigmoid(dec)
    k = _dot(mixed(2), wk_ref[...])
    k_ref[...] = k
    kkr = k * kkw_ref[...]
    ss = _head_sum(kkr * kkr, ones_ref[...])
    kk_ref[...] = (kkr / jnp.maximum(jnp.sqrt(ss), 1e-12)).astype(kk_ref.dtype)
    xv = mixed(3)
    v = _dot(xv, wv_ref[...])
    if vres:
        yv = _dot(xv, v1_ref[...])
        v = v + (vf_ref[...] - v) * _sigmoid(v0_ref[...] + _dot(yv.astype(BF16), v2_ref[...]))
    vo_ref[...] = v
    ya = _dot(mixed(4), a1_ref[...])
    ad_ref[...] = _sigmoid(a0_ref[...] + _dot(ya.astype(BF16), a2_ref[...])).astype(ad_ref.dtype)
    sgate = _sigmoid(_dot(mixed(5), g1_ref[...]))
    gate_ref[...] = _dot(sgate.astype(BF16), g2_ref[...]).astype(gate_ref.dtype)


def _scan_body(*refs):
    dir_refs = (refs[0:6], refs[6:12])
    ka_ref = refs[12]
    y_refs = refs[13:15]
    s_ref = refs[15]
    C = CHUNK
    D = ka_ref.shape[1]
    n_heads = D // C

    @pl.when(pl.program_id(1) == 0)
    def _():
        s_ref[...] = jnp.zeros_like(s_ref)

    ri = lax.broadcasted_iota(jnp.int32, (C, C), 0)
    ci = lax.broadcasted_iota(jnp.int32, (C, C), 1)
    xr = ri ^ ci
    eye = (ri == ci).astype(F32)
    r2 = lax.broadcasted_iota(jnp.int32, (C, 2 * C), 0)
    c2 = lax.broadcasted_iota(jnp.int32, (C, 2 * C), 1) & (C - 1)
    ka = ka_ref[...]

    ar, bk, vh, bkp, dph, strict, incl, where = [], [], [], [], [], [], [], []
    for d, (lw_ref, ad_ref, kf_ref, kk_ref, v_ref, r_ref) in enumerate(dir_refs):
        sg = 1 if d == 0 else -1
        strict_d = (r2 - c2) * sg > 0
        incl_d = (r2 - c2) * sg >= 0
        lw = lw_ref[...]
        ad = ad_ref[...].astype(F32)
        kk = kk_ref[...].astype(F32)
        c = jnp.dot(((ri - ci) * sg >= 0).astype(F32), lw, precision=HIGHEST, preferred_element_type=F32)
        pc = c[C - 1:C, :] if d == 0 else c[0:1, :]
        kd = kf_ref[...] * (1.0 + (ad - 1.0) * ka)
        b = kk * ad
        einv = jnp.exp(-c)
        a_t = (-kk * jnp.exp(c - lw)).astype(BF16)
        r_t = (r_ref[...].astype(F32) * jnp.exp(c)).astype(BF16)
        b_t = (b * einv).astype(BF16)
        k_t = (kd * einv).astype(BF16)
        epc = jnp.exp(pc - c)
        b_p = (b * epc).astype(BF16)
        k_p = (kd * epc).astype(BF16)
        d_p = jnp.exp(pc)
        vb = v_ref[...].astype(BF16)
        for h in range(n_heads):
            sl = slice(C * h, C * (h + 1))
            ar.append(jnp.concatenate([a_t[:, sl], r_t[:, sl]], axis=0))
            bk.append(jnp.concatenate([b_t[:, sl], k_t[:, sl]], axis=0))
            vh.append(vb[:, sl])
            bkp.append(jnp.concatenate([b_p[:, sl], k_p[:, sl]], axis=0))
            dph.append(d_p[:, sl])
            strict.append(strict_d)
            incl.append(incl_d)
            where.append((d, h, sl))

    items = range(len(where))
    sc = [lax.dot_general(ar[i], bk[i], _NT, preferred_element_type=F32) for i in items]
    st = [s_ref[d, h] for (d, h, _) in where]
    ars = [lax.dot_general(ar[i], st[i].astype(BF16), _NT, preferred_element_type=F32) for i in items]
    top = [jnp.where(strict[i], sc[i][0:C], 0.0) for i in items]
    bot = [jnp.where(incl[i], sc[i][C:], 0.0).astype(BF16) for i in items]
    lab = [top[i][:, 0:C] for i in items]
    x = [ars[i][0:C] + _dot(top[i][:, C:].astype(BF16), vh[i]) for i in items]
    t = [eye + jnp.where(xr == 1, lab[i], 0.0) for i in items]
    for lb in range(1, 6):
        level = (xr >> lb) == 1
        tb = [t[i].astype(BF16) for i in items]
        ot = [_dot(jnp.where(level, lab[i], 0.0).astype(BF16), tb[i]).astype(BF16) for i in items]
        t = [t[i] + _dot(tb[i], ot[i]) for i in items]
    u = [_dot(t[i].astype(BF16), x[i].astype(BF16)) for i in items]
    uv = [jnp.concatenate([u[i].astype(BF16), vh[i]], axis=0) for i in items]
    for i, (d, h, sl) in enumerate(where):
        y_refs[d][:, sl] = ars[i][C:] + _dot(bot[i], uv[i])
    for i, (d, h, sl) in enumerate(where):
        s_ref[d, h] = st[i] * dph[i] + lax.dot_general(uv[i], bkp[i], _TN, preferred_element_type=F32)


def _scan(lw, ad, kf, kk, v, r, k_a, lay):
    B, T, L, D = lay.B, lay.T, lay.L, lay.D
    nc_ctx = L // CHUNK
    nc_lat = T // CHUNK
    ctx_base = (B * T) // CHUNK

    def chunk_index(b, d, s):
        ctx_i = s if d == 0 else nc_ctx - 1 - s
        lat_i = s - nc_ctx if d == 0 else nc_lat - 1 - (s - nc_ctx)
        return jnp.where(s < nc_ctx, ctx_base + b * nc_ctx + ctx_i, b * nc_lat + lat_i)

    def spec(d, col):
        return pl.BlockSpec((CHUNK, D), lambda b, s: (chunk_index(b, d, s), col))

    in_specs, args = [], []
    for d in range(2):
        in_specs += [spec(d, d), spec(d, d), spec(d, 0), spec(d, 0), spec(d, 0), spec(d, 0)]
        args += [lw, ad, kf, kk, v, r]
    return pl.pallas_call(
        _scan_body, grid=(B, nc_ctx + nc_lat),
        in_specs=in_specs + [pl.BlockSpec((1, D), lambda b, s: (0, 0))],
        out_specs=[spec(0, 0), spec(1, 0)],
        out_shape=[jax.ShapeDtypeStruct((lay.n, D), F32), jax.ShapeDtypeStruct((lay.n, D), F32)],
        scratch_shapes=[pltpu.VMEM((2, D // CHUNK, CHUNK, CHUNK), F32)],
        compiler_params=pltpu.CompilerParams(dimension_semantics=("parallel", "arbitrary"),
                                             vmem_limit_bytes=VMEM_LIMIT),
        name="wkv_scan",
    )(*args, k_a)


def _rwkv_post_body(yf_ref, yr_ref, r_ref, kf_ref, ad_ref, v_ref, g_ref, x_ref, mod_ref,
                    ka_ref, rk_ref, lnw_ref, lnb_ref, ones_ref, wo_ref, o_ref):
    D = x_ref.shape[1]
    ones4 = ones_ref[...]
    y = yf_ref[...] + yr_ref[...]
    inv_n = 1.0 / RWKV_HEAD
    yc = y - _head_sum(y, ones4) * inv_n
    var = _head_sum(yc * yc, ones4) * inv_n
    yn = yc * lax.rsqrt(var + LN_X_EPS) * lnw_ref[...] + lnb_ref[...]
    kf = kf_ref[...]
    ad = ad_ref[...].astype(F32)
    ka = ka_ref[...]
    k_bonus = 0.5 * (kf * (1.0 + (ad[:, :D] - 1.0) * ka) + kf * (1.0 + (ad[:, D:] - 1.0) * ka))
    bonus = _head_sum(r_ref[...].astype(F32) * k_bonus * rk_ref[...], ones4) * v_ref[...]
    o = ((yn + bonus) * g_ref[...].astype(F32)).astype(BF16)
    o_ref[...] = x_ref[...] + mod_ref[2:3, :] * _dot(o, wo_ref[...])


def _rope_tables(T, tm):
    pos = jnp.arange(T)
    inv_freq = 1.0 / (ROPE_BASE ** (jnp.arange(AXIS_PAIRS, dtype=F32) / AXIS_PAIRS))
    ang_r = (pos // GRID_W).astype(F32)[:, None] * inv_freq[None, :]
    ang_c = (pos % GRID_W).astype(F32)[:, None] * inv_freq[None, :]
    pad1 = jnp.ones((T, LANES - QK_ROPE), F32)
    pad0 = jnp.zeros((T, LANES - QK_ROPE), F32)
    cos = jnp.concatenate([jnp.cos(ang_r), jnp.cos(ang_r), jnp.cos(ang_c), jnp.cos(ang_c), pad1], axis=1)
    sin = jnp.concatenate([-jnp.sin(ang_r), jnp.sin(ang_r), -jnp.sin(ang_c), jnp.sin(ang_c), pad0], axis=1)
    cos = jnp.concatenate([cos, jnp.ones((tm, LANES), F32)], axis=0)
    sin = jnp.concatenate([sin, jnp.zeros((tm, LANES), F32)], axis=0)
    return cos, sin


def _pad_cols(w, width):
    return jnp.pad(w, ((0, 0), (0, width - w.shape[1])))


def _pad_rows(w, height):
    return jnp.pad(w, ((0, height - w.shape[0]), (0, 0)))


def _block_diag2(w0, w1):
    z0 = jnp.zeros_like(w0)
    z1 = jnp.zeros_like(w1)
    return jnp.concatenate([jnp.concatenate([w0, z1], axis=1), jnp.concatenate([z0, w1], axis=1)], axis=0)


def kernel(x, c, ctx, c_ctx, ada_w, ada_b, norm1_g, norm2_g, mla_wqa, mla_qa_norm, mla_wqb, mla_wkva, mla_kva_norm, mla_wkvb, mla_q_norm, mla_k_norm, mla_wo, rwkv_mix, rwkv_wr, rwkv_wk, rwkv_wv, rwkv_wo, rwkv_w0, rwkv_w1, rwkv_w2, rwkv_a0, rwkv_a1, rwkv_a2, rwkv_g1, rwkv_g2, rwkv_k_k, rwkv_k_a, rwkv_r_k, rwkv_ln_w, rwkv_ln_b, rwkv_v0, rwkv_v1, rwkv_v2, ffn_w1, ffn_w3, ffn_w2, moe_router, moe_w1, moe_w3, moe_w2):
    B, T, D = x.shape
    L = ctx.shape[1]
    depth = ada_w.shape[0]
    lay = _Layout(B, T, L, D)
    n = lay.n
    tm = min(512, T)
    tm_small = min(256, T)
    assert T % tm == 0 and (B * L) % tm == 0 and T % GRID_W == 0 and L % CHUNK == 0 and B + 1 <= 16
    assert T & (T - 1) == 0 and L & (L - 1) == 0 and L % tm_small == 0

    ffn_b = [w.astype(BF16) for w in (ffn_w1, ffn_w3, ffn_w2)]
    moe_b = [w.astype(BF16) for w in (moe_w1, moe_w3, moe_w2)]
    xf = jnp.concatenate([x.reshape(B * T, D), ctx.reshape(B * L, D)], axis=0)
    cc = jnp.concatenate([c, c_ctx[None, :], jnp.zeros((16 - B - 1, D), F32)], axis=0)
    modall = _ada_call(cc, ada_w, ada_b)[:, :B + 1].reshape(depth, B + 1, 6, D)
    cos_t, sin_t = _rope_tables(T, tm)
    ones4 = jnp.kron(jnp.eye(4, dtype=F32), jnp.ones((RWKV_HEAD, RWKV_HEAD), F32)).astype(BF16)
    row1 = lambda z: z.reshape(1, -1)

    v_first = None
    for i in range(depth):
        j = i // 2
        modtab = modall[i]
        mod_t = (modtab, lay.mod_spec(tm))
        mod_s = (modtab, lay.mod_spec(tm_small))
        if i % 2 == 0:
            wa = jnp.concatenate([mla_wqa[j], _pad_cols(mla_wkva[j], KV_LORA + LANES)], axis=1).astype(BF16)
            wqb = mla_wqb[j].reshape(Q_LORA, MLA_HEADS, QK_HEAD)
            wqb = jnp.pad(wqb, ((0, 0), (0, 0), (0, QK_PAD - QK_HEAD))).reshape(Q_LORA, MLA_HEADS * QK_PAD).astype(BF16)
            wkvb = mla_wkvb[j].reshape(KV_LORA, MLA_HEADS, QK_NOPE + V_HEAD)
            wk = wkvb[:, :, :QK_NOPE].reshape(KV_LORA, MLA_HEADS * QK_NOPE).astype(BF16)
            wv = wkvb[:, :, QK_NOPE:].reshape(KV_LORA, MLA_HEADS * V_HEAD).astype(BF16)
            gq = _pad_cols(row1(mla_q_norm[j]) * SM_SCALE, QK_PAD)
            gk = _pad_cols(row1(mla_k_norm[j]), QK_PAD)
            cq, ckv, kr = _rowwise(
                _mla_a_body, "mla_a", n, tm, [xf], [mod_t],
                [row1(norm1_g[i]), wa, row1(mla_qa_norm[j]), row1(mla_kva_norm[j])],
                [(Q_LORA, BF16), (KV_LORA, BF16), (LANES, F32)])
            rope_in = [(cos_t, lay.rope_spec(tm)), (sin_t, lay.rope_spec(tm))]
            (q,) = _rowwise(_mla_q_body, "mla_q", n, tm, [cq], rope_in, [wqb, gq], [(MLA_HEADS * QK_PAD, BF16)])
            k, v = _rowwise(_mla_kv_body, "mla_kv", n, tm, [ckv, kr], rope_in, [wk, wv, gk],
                            [(MLA_HEADS * QK_PAD, BF16), (MLA_HEADS * V_PAD, BF16)])
            o = _attention(q, k, v, lay, tq=min(2048, T))
            (xf,) = _rowwise(_proj_resid_body, "mla_out", n, tm, [o, xf], [mod_t], [mla_wo[j].astype(BF16)], [(D, F32)])
        else:
            vres = j > 0
            mix = _pad_rows(rwkv_mix[j], 8)
            w1c = jnp.concatenate([rwkv_w1[j, 0], rwkv_w1[j, 1]], axis=1).astype(BF16)
            a1c = jnp.concatenate([rwkv_a1[j, 0], rwkv_a1[j, 1]], axis=1).astype(BF16)
            g1p = _pad_cols(rwkv_g1[j], 2 * LANES).astype(BF16)
            w2c = _block_diag2(rwkv_w2[j, 0], rwkv_w2[j, 1]).astype(BF16)
            a2c = _block_diag2(rwkv_a2[j, 0], rwkv_a2[j, 1]).astype(BF16)
            g2p = _pad_rows(rwkv_g2[j], 2 * LANES).astype(BF16)
            consts = [row1(norm1_g[i]), mix, rwkv_wr[j].astype(BF16), rwkv_wk[j].astype(BF16),
                      rwkv_wv[j].astype(BF16), w1c, a1c, g1p, w2c, a2c, g2p,
                      rwkv_w0[j].reshape(1, 2 * D), rwkv_a0[j].reshape(1, 2 * D), row1(rwkv_k_k[j]), ones4]
            rows = [xf]
            if vres:
                rows.append(v_first)
                consts += [_pad_cols(rwkv_v1[j - 1], LANES).astype(BF16), _pad_rows(rwkv_v2[j - 1], LANES).astype(BF16),
                           row1(rwkv_v0[j - 1])]
            halo_per_tile = tm_small // HALO
            before = pl.BlockSpec((HALO, D), lambda t: (jnp.maximum(t * halo_per_tile - 1, 0), 0))
            after = pl.BlockSpec((HALO, D), lambda t: (jnp.minimum((t + 1) * halo_per_tile, n // HALO - 1), 0))
            r, kf, lw, ad, kk, v_out, gate = _rowwise(
                functools.partial(_rwkv_feat_body, vres=vres, lay=lay), "rwkv_feat", n, tm_small,
                rows, [(xf, before), (xf, after), mod_s], consts,
                [(D, BF16), (D, F32), (2 * D, F32), (2 * D, BF16), (D, BF16), (D, F32), (D, BF16)])
            if j == 0:
                v_first = v_out
            k_a = row1(rwkv_k_a[j])
            y_fwd, y_rev = _scan(lw, ad, kf, kk, v_out, r, k_a, lay)
            (xf,) = _rowwise(
                _rwkv_post_body, "rwkv_out", n, tm_small, [y_fwd, y_rev, r, kf, ad, v_out, gate, xf], [mod_s],
                [k_a, row1(rwkv_r_k[j]), row1(rwkv_ln_w[j]), row1(rwkv_ln_b[j]), ones4, rwkv_wo[j].astype(BF16)],
                [(D, F32)])

        if i % 2 == 0:
            (xf,) = _rowwise(functools.partial(_ffn_body, tf=FFN_CHUNK), "dense_ffn", n, tm, [xf], [mod_t],
                             [row1(norm2_g[i]), ffn_b[0][j], ffn_b[1][j], ffn_b[2][j]], [(D, F32)])
        else:
            h, route = _rowwise(_router_body, "router", n, tm, [xf], [mod_t],
                                [row1(norm2_g[i]), _pad_cols(moe_router[j], LANES)], [(D, F32), (LANES, F32)])
            pos1, pos2, src, tile_expert, n_used = _route_plan(route, n, MOE_TM)
            ys = _gffn(tile_expert, n_used, src, h, moe_b[0], moe_b[1], moe_b[2], j, MOE_TM, tf=FFN_CHUNK)
            n_out = lay.n_lat if i == depth - 1 else n
            xf = _moe_combine(pos1, pos2, ys, route, xf, modtab, lay, tm, n_out)
    return xf[:B * T].reshape(B, T, D)
```

```python
import functools

import jax
import jax.numpy as jnp
from jax import lax
from jax.experimental import pallas as pl
from jax.experimental.pallas import tpu as pltpu

F32 = jnp.float32
BF16 = jnp.bfloat16
HIGHEST = lax.Precision.HIGHEST

GRID_W = 64
NORM_EPS = 1e-6
MLA_HEADS = 8
QK_NOPE = 128
QK_ROPE = 64
V_HEAD = 128
QK_HEAD = QK_NOPE + QK_ROPE
Q_LORA = 384
KV_LORA = 256
ROPE_BASE = 10000.0
AXIS_PAIRS = QK_ROPE // 4
SM_SCALE = QK_HEAD ** -0.5
RWKV_HEAD = 64
LN_X_EPS = 64e-5
DECAY_SCALE = 0.6065306597126334
N_EXPERTS = 8

LANES = 128
QK_PAD = 2 * LANES
V_PAD = 2 * LANES
CHUNK = 64
MOE_TM = 512
FFN_CHUNK = 256
ATTN_SUB = 512
DMA_UNROLL = 8
HALO = 8
VMEM_LIMIT = 56 * 1024 * 1024

_NT = (((1,), (1,)), ((), ()))
_TN = (((0,), (0,)), ((), ()))


def _dot(a, b):
    return jnp.dot(a, b, preferred_element_type=F32)


def _sigmoid(z):
    return 1.0 / (1.0 + jnp.exp(-z))


def _rms(z, width):
    return lax.rsqrt(jnp.sum(z * z, axis=-1, keepdims=True) * (1.0 / width) + NORM_EPS)


def _norm_mod(x, g, shift, scale):
    return (x * _rms(x, x.shape[-1])) * g * (1.0 + scale) + shift


def _rope(z, cos, sin):
    lane = lax.broadcasted_iota(jnp.int32, z.shape, 1)
    partner = jnp.where((lane & 16) == 0, pltpu.roll(z, LANES - 16, axis=1), pltpu.roll(z, 16, axis=1))
    return z * cos + partner * sin


def _head_sum(z, ones4):
    outs = []
    for q in range(z.shape[1] // 256):
        blk = z[:, 256 * q:256 * (q + 1)]
        hi = blk.astype(BF16)
        lo = (blk - hi.astype(F32)).astype(BF16)
        outs.append(_dot(hi, ones4) + _dot(lo, ones4))
    return jnp.concatenate(outs, axis=1)


class _Layout:
    def __init__(self, B, T, L, D):
        self.B, self.T, self.L, self.D = B, T, L, D
        self.n_lat = B * T
        self.n = B * T + B * L

    def mod_spec(self, tm):
        n_lat_tiles = self.n_lat // tm
        T, B, D = self.T, self.B, self.D
        return pl.BlockSpec((None, 6, D), lambda i: (jnp.where(i < n_lat_tiles, (i * tm) // T, B), 0, 0))

    def rope_spec(self, tm):
        n_lat_tiles = self.n_lat // tm
        tpb = self.T // tm
        return pl.BlockSpec((tm, LANES), lambda i: (jnp.where(i < n_lat_tiles, i % tpb, tpb), 0))


def _const_spec(arr):
    nd = arr.ndim
    return pl.BlockSpec(arr.shape, lambda i: (0,) * nd, pipeline_mode=pl.Buffered(1))


def _rowwise(body, name, n_rows, tm, row_ins, tile_ins, const_ins, outs):
    in_specs = [pl.BlockSpec((tm, a.shape[1]), lambda i: (i, 0)) for a in row_ins]
    in_specs += [spec for (_, spec) in tile_ins]
    in_specs += [_const_spec(a) for a in const_ins]
    out_specs = [pl.BlockSpec((tm, w), lambda i: (i, 0)) for (w, _) in outs]
    out_shape = [jax.ShapeDtypeStruct((n_rows, w), dt) for (w, dt) in outs]
    return pl.pallas_call(
        body, grid=(n_rows // tm,), in_specs=in_specs, out_specs=out_specs, out_shape=out_shape,
        compiler_params=pltpu.CompilerParams(dimension_semantics=("parallel",), vmem_limit_bytes=VMEM_LIMIT),
        name=name,
    )(*row_ins, *[a for (a, _) in tile_ins], *const_ins)


def _ada_body(c_ref, w_ref, b_ref, o_ref):
    c = c_ref[...]
    o_ref[...] = jnp.dot(c * _sigmoid(c), w_ref[...], precision=HIGHEST, preferred_element_type=F32) + b_ref[...]


def _ada_call(cc, ada_w, ada_b):
    depth, D, W = ada_w.shape
    tn = 1024
    return pl.pallas_call(
        _ada_body, grid=(depth, W // tn),
        in_specs=[pl.BlockSpec(cc.shape, lambda l, j: (0, 0)),
                  pl.BlockSpec((None, D, tn), lambda l, j: (l, 0, j)),
                  pl.BlockSpec((None, 1, tn), lambda l, j: (l, 0, j))],
        out_specs=pl.BlockSpec((None, cc.shape[0], tn), lambda l, j: (l, 0, j)),
        out_shape=jax.ShapeDtypeStruct((depth, cc.shape[0], W), F32),
        compiler_params=pltpu.CompilerParams(dimension_semantics=("parallel", "parallel"), vmem_limit_bytes=VMEM_LIMIT),
        name="ada",
    )(cc, ada_w, ada_b.reshape(depth, 1, W))


def _mla_a_body(x_ref, mod_ref, g_ref, wa_ref, qan_ref, kvn_ref, cq_ref, ckv_ref, kr_ref):
    h = _norm_mod(x_ref[...], g_ref[...], mod_ref[0:1, :], mod_ref[1:2, :]).astype(BF16)
    acc = _dot(h, wa_ref[...])
    qa = acc[:, 0:Q_LORA]
    kv = acc[:, Q_LORA:Q_LORA + KV_LORA]
    cq_ref[...] = (qa * _rms(qa, Q_LORA) * qan_ref[...]).astype(BF16)
    ckv_ref[...] = (kv * _rms(kv, KV_LORA) * kvn_ref[...]).astype(BF16)
    kr_ref[...] = acc[:, Q_LORA + KV_LORA:]


def _mla_q_body(cq_ref, cos_ref, sin_ref, wqb_ref, gq_ref, q_ref):
    acc = _dot(cq_ref[...], wqb_ref[...])
    cos = cos_ref[...]
    sin = sin_ref[...]
    g = gq_ref[...]
    for h in range(MLA_HEADS):
        a = acc[:, QK_PAD * h:QK_PAD * (h + 1)]
        an = a * _rms(a, QK_HEAD) * g
        q_ref[:, QK_PAD * h:QK_PAD * h + LANES] = an[:, :LANES].astype(BF16)
        q_ref[:, QK_PAD * h + LANES:QK_PAD * (h + 1)] = _rope(an[:, LANES:], cos, sin).astype(BF16)


def _mla_kv_body(ckv_ref, kr_ref, cos_ref, sin_ref, wk_ref, wv_ref, gk_ref, k_ref, v_ref):
    ckv = ckv_ref[...]
    kn = _dot(ckv, wk_ref[...])
    vals = _dot(ckv, wv_ref[...]).astype(BF16)
    lane = lax.broadcasted_iota(jnp.int32, (vals.shape[0], LANES), 1)
    ones_col = jnp.where(lane == 0, 1.0, 0.0).astype(BF16)
    for h in range(MLA_HEADS):
        v_ref[:, V_PAD * h:V_PAD * h + V_HEAD] = vals[:, V_HEAD * h:V_HEAD * (h + 1)]
        v_ref[:, V_PAD * h + V_HEAD:V_PAD * (h + 1)] = ones_col
    kr = kr_ref[...]
    ss_rope = jnp.sum(kr * kr, axis=-1, keepdims=True)
    g = gk_ref[...]
    g_nope = g[:, :LANES]
    kr_rot = _rope(kr * g[:, LANES:], cos_ref[...], sin_ref[...])
    for h in range(MLA_HEADS):
        a = kn[:, LANES * h:LANES * (h + 1)]
        inv = lax.rsqrt((jnp.sum(a * a, axis=-1, keepdims=True) + ss_rope) * (1.0 / QK_HEAD) + NORM_EPS)
        k_ref[:, QK_PAD * h:QK_PAD * h + LANES] = (a * inv * g_nope).astype(BF16)
        k_ref[:, QK_PAD * h + LANES:QK_PAD * (h + 1)] = (kr_rot * inv).astype(BF16)


def _attn_body(*refs, nseg, sub):
    q_ref = refs[0]
    o_ref = refs[-1]
    n_sub = q_ref.shape[0] // sub

    def scores(j):
        q = q_ref[sub * j:sub * (j + 1), :]
        return [lax.dot_general(q, refs[1 + 2 * i][...], _NT, preferred_element_type=F32) for i in range(nseg)]

    def finish(j, ss):
        m = functools.reduce(jnp.maximum, [jnp.max(s, axis=-1, keepdims=True) for s in ss])
        acc = functools.reduce(lambda a, b: a + b,
                               [_dot(jnp.exp((ss[i] - m).astype(BF16)), refs[2 + 2 * i][...]) for i in range(nseg)])
        o_ref[sub * j:sub * (j + 1), :] = (acc[:, :V_HEAD] / acc[:, V_HEAD:V_HEAD + 1]).astype(o_ref.dtype)

    pending = scores(0)
    for j in range(n_sub):
        following = scores(j + 1) if j + 1 < n_sub else None
        finish(j, pending)
        pending = following


def _attention(q, k, v, lay, tq):
    B, T, L = lay.B, lay.T, lay.L
    nq = T // tq
    ctx0 = (B * T) // L
    params = pltpu.CompilerParams(dimension_semantics=("parallel", "parallel", "arbitrary"),
                                  vmem_limit_bytes=VMEM_LIMIT)
    o_lat = pl.pallas_call(
        functools.partial(_attn_body, nseg=2, sub=min(ATTN_SUB, tq)), grid=(B, MLA_HEADS, nq),
        in_specs=[pl.BlockSpec((tq, QK_PAD), lambda b, h, i: (b * nq + i, h)),
                  pl.BlockSpec((L, QK_PAD), lambda b, h, i: (ctx0 + b, h)),
                  pl.BlockSpec((L, V_PAD), lambda b, h, i: (ctx0 + b, h)),
                  pl.BlockSpec((T, QK_PAD), lambda b, h, i: (b, h)),
                  pl.BlockSpec((T, V_PAD), lambda b, h, i: (b, h))],
        out_specs=pl.BlockSpec((tq, V_HEAD), lambda b, h, i: (b * nq + i, h)),
        out_shape=jax.ShapeDtypeStruct((lay.n, MLA_HEADS * V_HEAD), BF16),
        compiler_params=params, name="attn_latent",
    )(q, k, v, k, v)
    return pl.pallas_call(
        functools.partial(_attn_body, nseg=1, sub=min(ATTN_SUB, L)), grid=(B, MLA_HEADS, 1),
        in_specs=[pl.BlockSpec((L, QK_PAD), lambda b, h, i: (ctx0 + b, h)),
                  pl.BlockSpec((L, QK_PAD), lambda b, h, i: (ctx0 + b, h)),
                  pl.BlockSpec((L, V_PAD), lambda b, h, i: (ctx0 + b, h)),
                  pl.BlockSpec(memory_space=pl.ANY)],
        out_specs=pl.BlockSpec((L, V_HEAD), lambda b, h, i: (ctx0 + b, h)),
        out_shape=jax.ShapeDtypeStruct((lay.n, MLA_HEADS * V_HEAD), BF16),
        input_output_aliases={3: 0},
        compiler_params=params, name="attn_ctx",
    )(q, k, v, o_lat)


def _proj_resid_body(a_ref, x_ref, mod_ref, w_ref, o_ref):
    o_ref[...] = x_ref[...] + mod_ref[2:3, :] * _dot(a_ref[...], w_ref[...])


def _router_body(x_ref, mod_ref, g_ref, router_ref, h_ref, route_ref):
    hf = _norm_mod(x_ref[...], g_ref[...], mod_ref[3:4, :], mod_ref[4:5, :])
    h_ref[...] = hf
    logits = jnp.dot(hf, router_ref[...], precision=HIGHEST, preferred_element_type=F32)
    lane = lax.broadcasted_iota(jnp.int32, logits.shape, 1)
    lanef = lane.astype(F32)
    neg = jnp.float32(-1e30)
    lg = jnp.where(lane < N_EXPERTS, logits, neg)
    m1 = jnp.max(lg, axis=-1, keepdims=True)
    i1 = jnp.min(jnp.where(lg == m1, lanef, float(LANES)), axis=-1, keepdims=True)
    lg2 = jnp.where(lanef == i1, neg, lg)
    m2 = jnp.max(lg2, axis=-1, keepdims=True)
    i2 = jnp.min(jnp.where(lg2 == m2, lanef, float(LANES)), axis=-1, keepdims=True)
    e2 = jnp.exp(m2 - m1)
    w1 = 1.0 / (1.0 + e2)
    w2 = e2 / (1.0 + e2)
    route_ref[...] = jnp.where(lane == 0, i1, jnp.where(lane == 1, i2, jnp.where(lane == 2, w1, jnp.where(lane == 3, w2, 0.0))))


def _route_plan(route, n, tm_g):
    i1 = route[:, 0].astype(jnp.int32)
    i2 = route[:, 1].astype(jnp.int32)
    experts = jnp.arange(N_EXPERTS, dtype=jnp.int32)
    onehot = ((i1[:, None] == experts) | (i2[:, None] == experts)).astype(jnp.int32)
    rank = jnp.cumsum(onehot, axis=0) - onehot
    tiles_e = (jnp.sum(onehot, axis=0) + tm_g - 1) // tm_g
    tile_end = jnp.cumsum(tiles_e)
    start = (tile_end - tiles_e) * tm_g
    pos1 = start[i1] + jnp.take_along_axis(rank, i1[:, None], axis=1)[:, 0]
    pos2 = start[i2] + jnp.take_along_axis(rank, i2[:, None], axis=1)[:, 0]
    n_tiles = -(-2 * n // tm_g) + N_EXPERTS
    tile_ids = jnp.arange(n_tiles, dtype=jnp.int32)
    tile_expert = jnp.minimum(jnp.sum((tile_ids[:, None] >= tile_end[None, :]).astype(jnp.int32), axis=1), N_EXPERTS - 1)
    tok = jnp.arange(n, dtype=jnp.int32)
    src = jnp.zeros((n_tiles * tm_g,), jnp.int32).at[jnp.concatenate([pos1, pos2])].set(
        jnp.concatenate([tok, tok]), unique_indices=True, mode="promise_in_bounds")
    return pos1, pos2, src, tile_expert.astype(jnp.int32), tile_end[-1:].astype(jnp.int32)


def _row_copy(src, src_row, dst, dst_row, sem):
    return pltpu.make_async_copy(src.at[pl.ds(src_row, 1), :], dst.at[pl.ds(dst_row, 1), :], sem)


def _gffn_body(te_ref, nu_ref, src_ref, h_hbm, w1_ref, w3_ref, w2_ref, ys_ref,
               xg, acc_ref, dsem, *, tm, tf):
    del te_ref
    m = pl.program_id(0)
    n_m = pl.num_programs(0)
    slot = m & 1
    other = 1 - slot
    n_used = nu_ref[0]
    used = m < n_used
    nf = w1_ref.shape[1] // tf

    def wait_slot(s):
        pltpu.make_async_copy(xg.at[s], xg.at[s], dsem.at[s]).wait()

    @pl.when(m == 0)
    def _():
        def issue(t, carry):
            _row_copy(h_hbm, src_ref[t], xg.at[0], t, dsem.at[0]).start()
            return carry

        lax.fori_loop(0, tm, issue, 0, unroll=DMA_UNROLL)

    @pl.when(m <= n_used)
    def _():
        wait_slot(slot)

    @pl.when(used)
    def _():
        base = jnp.minimum(m + 1, n_m - 1) * tm
        h = xg[slot].astype(BF16)
        for c in range(nf):
            for t in range(tm * c // nf, tm * (c + 1) // nf):
                _row_copy(h_hbm, src_ref[base + t], xg.at[other], t, dsem.at[other]).start(priority=t % 2)
            a1 = _dot(h, w1_ref[:, tf * c:tf * (c + 1)])
            a3 = _dot(h, w3_ref[:, tf * c:tf * (c + 1)])
            part = _dot((a1 * _sigmoid(a1) * a3).astype(BF16), w2_ref[tf * c:tf * (c + 1), :])
            if c == 0:
                acc_ref[...] = part
            else:
                acc_ref[...] += part
        ys_ref[...] = acc_ref[...]

    @pl.when(jnp.logical_not(used))
    def _():
        ys_ref[...] = jnp.zeros_like(ys_ref)

    @pl.when(used & (m == n_m - 1))
    def _():
        wait_slot(other)


def _gffn(tile_expert, n_used, src, h, w1, w3, w2, layer, tm_g, tf):
    _, E, D, F = w1.shape
    n_tiles = src.shape[0] // tm_g
    resident = pl.Buffered(1)
    grid_spec = pltpu.PrefetchScalarGridSpec(
        num_scalar_prefetch=3, grid=(n_tiles,),
        in_specs=[pl.BlockSpec(memory_space=pl.ANY),
                  pl.BlockSpec((None, None, D, F), lambda m, te, nu, sr: (layer, te[m], 0, 0), pipeline_mode=resident),
                  pl.BlockSpec((None, None, D, F), lambda m, te, nu, sr: (layer, te[m], 0, 0), pipeline_mode=resident),
                  pl.BlockSpec((None, None, F, D), lambda m, te, nu, sr: (layer, te[m], 0, 0), pipeline_mode=resident)],
        out_specs=pl.BlockSpec((tm_g, D), lambda m, te, nu, sr: (m, 0)),
        scratch_shapes=[pltpu.VMEM((2, tm_g, D), F32), pltpu.VMEM((tm_g, D), F32), pltpu.SemaphoreType.DMA((2,))])
    return pl.pallas_call(
        functools.partial(_gffn_body, tm=tm_g, tf=tf), grid_spec=grid_spec,
        out_shape=jax.ShapeDtypeStruct((n_tiles * tm_g, D), F32),
        compiler_params=pltpu.CompilerParams(dimension_semantics=("arbitrary",),
                                             vmem_limit_bytes=VMEM_LIMIT),
        name="moe_gffn",
    )(tile_expert, n_used, src, h, w1, w3, w2)


def _moe_combine_body(pos_ref, ys_hbm, route_ref, x_ref, mod_ref, o_ref, yg, dsem, *, tm, n):
    i = pl.program_id(0)
    slot = i & 1

    def start_gather(tile, s):
        def issue(t, carry):
            tok = tile * tm + t
            _row_copy(ys_hbm, pos_ref[tok], yg.at[s, 0], t, dsem.at[s]).start(priority=0)
            _row_copy(ys_hbm, pos_ref[n + tok], yg.at[s, 1], t, dsem.at[s]).start(priority=1)
            return carry

        lax.fori_loop(0, tm, issue, 0, unroll=DMA_UNROLL)

    @pl.when(i == 0)
    def _():
        start_gather(0, 0)

    pltpu.make_async_copy(yg.at[slot], yg.at[slot], dsem.at[slot]).wait()

    @pl.when(i + 1 < pl.num_programs(0))
    def _():
        start_gather(i + 1, 1 - slot)

    route = route_ref[...]
    y = route[:, 2:3] * yg[slot, 0] + route[:, 3:4] * yg[slot, 1]
    o_ref[...] = x_ref[...] + mod_ref[5:6, :] * y


def _moe_combine(pos1, pos2, ys, route, xf, modtab, lay, tm, n_out):
    n, D = xf.shape
    n_lat_tiles = lay.n_lat // tm
    T, B = lay.T, lay.B
    row = lambda i, pos: (i, 0)
    grid_spec = pltpu.PrefetchScalarGridSpec(
        num_scalar_prefetch=1, grid=(n_out // tm,),
        in_specs=[pl.BlockSpec(memory_space=pl.ANY), pl.BlockSpec((tm, LANES), row), pl.BlockSpec((tm, D), row),
                  pl.BlockSpec((None, 6, D), lambda i, pos: (jnp.where(i < n_lat_tiles, (i * tm) // T, B), 0, 0))],
        out_specs=pl.BlockSpec((tm, D), row),
        scratch_shapes=[pltpu.VMEM((2, 2, tm, D), F32), pltpu.SemaphoreType.DMA((2,))])
    return pl.pallas_call(
        functools.partial(_moe_combine_body, tm=tm, n=n), grid_spec=grid_spec,
        out_shape=jax.ShapeDtypeStruct((n_out, D), F32),
        compiler_params=pltpu.CompilerParams(dimension_semantics=("arbitrary",), vmem_limit_bytes=VMEM_LIMIT),
        name="moe_combine",
    )(jnp.concatenate([pos1, pos2]), ys, route, xf, modtab)


def _ffn_body(x_ref, mod_ref, g_ref, w1_ref, w3_ref, w2_ref, o_ref, *, tf):
    x = x_ref[...]
    h = _norm_mod(x, g_ref[...], mod_ref[3:4, :], mod_ref[4:5, :]).astype(BF16)
    acc = None
    for c in range(w1_ref.shape[1] // tf):
        a1 = _dot(h, w1_ref[:, tf * c:tf * (c + 1)])
        a3 = _dot(h, w3_ref[:, tf * c:tf * (c + 1)])
        part = _dot((a1 * _sigmoid(a1) * a3).astype(BF16), w2_ref[tf * c:tf * (c + 1), :])
        acc = part if acc is None else acc + part
    o_ref[...] = x + mod_ref[5:6, :] * acc


def _rwkv_feat_body(*refs, vres, lay):
    if vres:
        (x_ref, vf_ref, xp_ref, xn_ref, mod_ref, g_ref, mix_ref, wr_ref, wk_ref, wv_ref, w1_ref, a1_ref, g1_ref,
         w2_ref, a2_ref, g2_ref, w0_ref, a0_ref, kkw_ref, ones_ref, v1_ref, v2_ref, v0_ref,
         r_ref, k_ref, lw_ref, ad_ref, kk_ref, vo_ref, gate_ref) = refs
    else:
        (x_ref, xp_ref, xn_ref, mod_ref, g_ref, mix_ref, wr_ref, wk_ref, wv_ref, w1_ref, a1_ref, g1_ref,
         w2_ref, a2_ref, g2_ref, w0_ref, a0_ref, kkw_ref, ones_ref,
         r_ref, k_ref, lw_ref, ad_ref, kk_ref, vo_ref, gate_ref) = refs
    tm = x_ref.shape[0]
    g = g_ref[...]
    shift = mod_ref[0:1, :]
    scale = mod_ref[1:2, :]
    h = _norm_mod(x_ref[...], g, shift, scale)
    h_before = _norm_mod(xp_ref[...], g, shift, scale)[HALO - 1:HALO, :]
    h_after = _norm_mod(xn_ref[...], g, shift, scale)[0:1, :]
    row = lax.broadcasted_iota(jnp.int32, (tm, 1), 0)
    gidx = pl.program_id(0) * tm + row
    in_lat = gidx < lay.n_lat
    pos = jnp.where(in_lat, gidx & (lay.T - 1), (gidx - lay.n_lat) & (lay.L - 1))
    last_pos = jnp.where(in_lat, lay.T - 1, lay.L - 1)
    prev = jnp.where(row == 0, h_before, pltpu.roll(h, 1, axis=0))
    prev = jnp.where(pos == 0, 0.0, prev)
    nxt = jnp.where(row == tm - 1, h_after, pltpu.roll(h, tm - 1, axis=0))
    nxt = jnp.where(pos == last_pos, 0.0, nxt)
    xx = 0.5 * (prev + nxt) - h

    def mixed(j):
        return (h + xx * mix_ref[j:j + 1, :]).astype(BF16)

    r_ref[...] = _dot(mixed(0), wr_ref[...]).astype(r_ref.dtype)
    tw = jnp.tanh(_dot(mixed(1), w1_ref[...]))
    dec = w0_ref[...] + _dot(tw.astype(BF16), w2_ref[...])
    lw_ref[...] = -DECAY_SCALE * _sigmoid(dec)
    k = _dot(mixed(2), wk_ref[...])
    k_ref[...] = k
    kkr = k * kkw_ref[...]
    ss = _head_sum(kkr * kkr, ones_ref[...])
    kk_ref[...] = (kkr / jnp.maximum(jnp.sqrt(ss), 1e-12)).astype(kk_ref.dtype)
    xv = mixed(3)
    v = _dot(xv, wv_ref[...])
    if vres:
        yv = _dot(xv, v1_ref[...])
        v = v + (vf_ref[...] - v) * _sigmoid(v0_ref[...] + _dot(yv.astype(BF16), v2_ref[...]))
    vo_ref[...] = v
    ya = _dot(mixed(4), a1_ref[...])
    ad_ref[...] = _sigmoid(a0_ref[...] + _dot(ya.astype(BF16), a2_ref[...])).astype(ad_ref.dtype)
    sgate = _sigmoid(_dot(mixed(5), g1_ref[...]))
    gate_ref[...] = _dot(sgate.astype(BF16), g2_ref[...]).astype(gate_ref.dtype)


def _scan_body(*refs):
    dir_refs = (refs[0:6], refs[6:12])
    ka_ref = refs[12]
    y_refs = refs[13:15]
    s_ref = refs[15]
    C = CHUNK
    D = ka_ref.shape[1]
    n_heads = D // C

    @pl.when(pl.program_id(1) == 0)
    def _():
        s_ref[...] = jnp.zeros_like(s_ref)

    ri = lax.broadcasted_iota(jnp.int32, (C, C), 0)
    ci = lax.broadcasted_iota(jnp.int32, (C, C), 1)
    xr = ri ^ ci
    eye = (ri == ci).astype(F32)
    r2 = lax.broadcasted_iota(jnp.int32, (C, 2 * C), 0)
    c2 = lax.broadcasted_iota(jnp.int32, (C, 2 * C), 1) & (C - 1)
    ka = ka_ref[...]

    ar, bk, vh, bkp, dph, strict, incl, where = [], [], [], [], [], [], [], []
    for d, (lw_ref, ad_ref, kf_ref, kk_ref, v_ref, r_ref) in enumerate(dir_refs):
        sg = 1 if d == 0 else -1
        strict_d = (r2 - c2) * sg > 0
        incl_d = (r2 - c2) * sg >= 0
        lw = lw_ref[...]
        ad = ad_ref[...].astype(F32)
        kk = kk_ref[...].astype(F32)
        c = jnp.dot(((ri - ci) * sg >= 0).astype(F32), lw, precision=HIGHEST, preferred_element_type=F32)
        pc = c[C - 1:C, :] if d == 0 else c[0:1, :]
        kd = kf_ref[...] * (1.0 + (ad - 1.0) * ka)
        b = kk * ad
        einv = jnp.exp(-c)
        a_t = (-kk * jnp.exp(c - lw)).astype(BF16)
        r_t = (r_ref[...].astype(F32) * jnp.exp(c)).astype(BF16)
        b_t = (b * einv).astype(BF16)
        k_t = (kd * einv).astype(BF16)
        epc = jnp.exp(pc - c)
        b_p = (b * epc).astype(BF16)
        k_p = (kd * epc).astype(BF16)
        d_p = jnp.exp(pc)
        vb = v_ref[...].astype(BF16)
        for h in range(n_heads):
            sl = slice(C * h, C * (h + 1))
            ar.append(jnp.concatenate([a_t[:, sl], r_t[:, sl]], axis=0))
            bk.append(jnp.concatenate([b_t[:, sl], k_t[:, sl]], axis=0))
            vh.append(vb[:, sl])
            bkp.append(jnp.concatenate([b_p[:, sl], k_p[:, sl]], axis=0))
            dph.append(d_p[:, sl])
            strict.append(strict_d)
            incl.append(incl_d)
            where.append((d, h, sl))

    items = range(len(where))
    sc = [lax.dot_general(ar[i], bk[i], _NT, preferred_element_type=F32) for i in items]
    st = [s_ref[d, h] for (d, h, _) in where]
    ars = [lax.dot_general(ar[i], st[i].astype(BF16), _NT, preferred_element_type=F32) for i in items]
    top = [jnp.where(strict[i], sc[i][0:C], 0.0) for i in items]
    bot = [jnp.where(incl[i], sc[i][C:], 0.0).astype(BF16) for i in items]
    lab = [top[i][:, 0:C] for i in items]
    x = [ars[i][0:C] + _dot(top[i][:, C:].astype(BF16), vh[i]) for i in items]
    t = [eye + jnp.where(xr == 1, lab[i], 0.0) for i in items]
    for lb in range(1, 6):
        level = (xr >> lb) == 1
        tb = [t[i].astype(BF16) for i in items]
        ot = [_dot(jnp.where(level, lab[i], 0.0).astype(BF16), tb[i]).astype(BF16) for i in items]
        t = [t[i] + _dot(tb[i], ot[i]) for i in items]
    u = [_dot(t[i].astype(BF16), x[i].astype(BF16)) for i in items]
    uv = [jnp.concatenate([u[i].astype(BF16), vh[i]], axis=0) for i in items]
    for i, (d, h, sl) in enumerate(where):
        y_refs[d][:, sl] = ars[i][C:] + _dot(bot[i], uv[i])
    for i, (d, h, sl) in enumerate(where):
        s_ref[d, h] = st[i] * dph[i] + lax.dot_general(uv[i], bkp[i], _TN, preferred_element_type=F32)


def _scan(lw, ad, kf, kk, v, r, k_a, lay):
    B, T, L, D = lay.B, lay.T, lay.L, lay.D
    nc_ctx = L // CHUNK
    nc_lat = T // CHUNK
    ctx_base = (B * T) // CHUNK

    def chunk_index(b, d, s):
        ctx_i = s if d == 0 else nc_ctx - 1 - s
        lat_i = s - nc_ctx if d == 0 else nc_lat - 1 - (s - nc_ctx)
        return jnp.where(s < nc_ctx, ctx_base + b * nc_ctx + ctx_i, b * nc_lat + lat_i)

    def spec(d, col):
        return pl.BlockSpec((CHUNK, D), lambda b, s: (chunk_index(b, d, s), col))

    in_specs, args = [], []
    for d in range(2):
        in_specs += [spec(d, d), spec(d, d), spec(d, 0), spec(d, 0), spec(d, 0), spec(d, 0)]
        args += [lw, ad, kf, kk, v, r]
    return pl.pallas_call(
        _scan_body, grid=(B, nc_ctx + nc_lat),
        in_specs=in_specs + [pl.BlockSpec((1, D), lambda b, s: (0, 0))],
        out_specs=[spec(0, 0), spec(1, 0)],
        out_shape=[jax.ShapeDtypeStruct((lay.n, D), F32), jax.ShapeDtypeStruct((lay.n, D), F32)],
        scratch_shapes=[pltpu.VMEM((2, D // CHUNK, CHUNK, CHUNK), F32)],
        compiler_params=pltpu.CompilerParams(dimension_semantics=("parallel", "arbitrary"),
                                             vmem_limit_bytes=VMEM_LIMIT),
        name="wkv_scan",
    )(*args, k_a)


def _rwkv_post_body(yf_ref, yr_ref, r_ref, kf_ref, ad_ref, v_ref, g_ref, x_ref, mod_ref,
                    ka_ref, rk_ref, lnw_ref, lnb_ref, ones_ref, wo_ref, o_ref):
    D = x_ref.shape[1]
    ones4 = ones_ref[...]
    y = yf_ref[...] + yr_ref[...]
    inv_n = 1.0 / RWKV_HEAD
    yc = y - _head_sum(y, ones4) * inv_n
    var = _head_sum(yc * yc, ones4) * inv_n
    yn = yc * lax.rsqrt(var + LN_X_EPS) * lnw_ref[...] + lnb_ref[...]
    kf = kf_ref[...]
    ad = ad_ref[...].astype(F32)
    ka = ka_ref[...]
    k_bonus = 0.5 * (kf * (1.0 + (ad[:, :D] - 1.0) * ka) + kf * (1.0 + (ad[:, D:] - 1.0) * ka))
    bonus = _head_sum(r_ref[...].astype(F32) * k_bonus * rk_ref[...], ones4) * v_ref[...]
    o = ((yn + bonus) * g_ref[...].astype(F32)).astype(BF16)
    o_ref[...] = x_ref[...] + mod_ref[2:3, :] * _dot(o, wo_ref[...])


def _rope_tables(T, tm):
    pos = jnp.arange(T)
    inv_freq = 1.0 / (ROPE_BASE ** (jnp.arange(AXIS_PAIRS, dtype=F32) / AXIS_PAIRS))
    ang_r = (pos // GRID_W).astype(F32)[:, None] * inv_freq[None, :]
    ang_c = (pos % GRID_W).astype(F32)[:, None] * inv_freq[None, :]
    pad1 = jnp.ones((T, LANES - QK_ROPE), F32)
    pad0 = jnp.zeros((T, LANES - QK_ROPE), F32)
    cos = jnp.concatenate([jnp.cos(ang_r), jnp.cos(ang_r), jnp.cos(ang_c), jnp.cos(ang_c), pad1], axis=1)
    sin = jnp.concatenate([-jnp.sin(ang_r), jnp.sin(ang_r), -jnp.sin(ang_c), jnp.sin(ang_c), pad0], axis=1)
    cos = jnp.concatenate([cos, jnp.ones((tm, LANES), F32)], axis=0)
    sin = jnp.concatenate([sin, jnp.zeros((tm, LANES), F32)], axis=0)
    return cos, sin


def _pad_cols(w, width):
    return jnp.pad(w, ((0, 0), (0, width - w.shape[1])))


def _pad_rows(w, height):
    return jnp.pad(w, ((0, height - w.shape[0]), (0, 0)))


def _block_diag2(w0, w1):
    z0 = jnp.zeros_like(w0)
    z1 = jnp.zeros_like(w1)
    return jnp.concatenate([jnp.concatenate([w0, z1], axis=1), jnp.concatenate([z0, w1], axis=1)], axis=0)


def kernel(x, c, ctx, c_ctx, ada_w, ada_b, norm1_g, norm2_g, mla_wqa, mla_qa_norm, mla_wqb, mla_wkva, mla_kva_norm, mla_wkvb, mla_q_norm, mla_k_norm, mla_wo, rwkv_mix, rwkv_wr, rwkv_wk, rwkv_wv, rwkv_wo, rwkv_w0, rwkv_w1, rwkv_w2, rwkv_a0, rwkv_a1, rwkv_a2, rwkv_g1, rwkv_g2, rwkv_k_k, rwkv_k_a, rwkv_r_k, rwkv_ln_w, rwkv_ln_b, rwkv_v0, rwkv_v1, rwkv_v2, ffn_w1, ffn_w3, ffn_w2, moe_router, moe_w1, moe_w3, moe_w2):
    B, T, D = x.shape
    L = ctx.shape[1]
    depth = ada_w.shape[0]
    lay = _Layout(B, T, L, D)
    n = lay.n
    tm = min(512, T)
    tm_small = min(256, T)
    assert T % tm == 0 and (B * L) % tm == 0 and T % GRID_W == 0 and L % CHUNK == 0 and B + 1 <= 16
    assert T & (T - 1) == 0 and L & (L - 1) == 0 and L % tm_small == 0

    ffn_b = [w.astype(BF16) for w in (ffn_w1, ffn_w3, ffn_w2)]
    moe_b = [w.astype(BF16) for w in (moe_w1, moe_w3, moe_w2)]
    xf = jnp.concatenate([x.reshape(B * T, D), ctx.reshape(B * L, D)], axis=0)
    cc = jnp.concatenate([c, c_ctx[None, :], jnp.zeros((16 - B - 1, D), F32)], axis=0)
    modall = _ada_call(cc, ada_w, ada_b)[:, :B + 1].reshape(depth, B + 1, 6, D)
    cos_t, sin_t = _rope_tables(T, tm)
    ones4 = jnp.kron(jnp.eye(4, dtype=F32), jnp.ones((RWKV_HEAD, RWKV_HEAD), F32)).astype(BF16)
    row1 = lambda z: z.reshape(1, -1)

    v_first = None
    for i in range(depth):
        j = i // 2
        modtab = modall[i]
        mod_t = (modtab, lay.mod_spec(tm))
        mod_s = (modtab, lay.mod_spec(tm_small))
        if i % 2 == 0:
            wa = jnp.concatenate([mla_wqa[j], _pad_cols(mla_wkva[j], KV_LORA + LANES)], axis=1).astype(BF16)
            wqb = mla_wqb[j].reshape(Q_LORA, MLA_HEADS, QK_HEAD)
            wqb = jnp.pad(wqb, ((0, 0), (0, 0), (0, QK_PAD - QK_HEAD))).reshape(Q_LORA, MLA_HEADS * QK_PAD).astype(BF16)
            wkvb = mla_wkvb[j].reshape(KV_LORA, MLA_HEADS, QK_NOPE + V_HEAD)
            wk = wkvb[:, :, :QK_NOPE].reshape(KV_LORA, MLA_HEADS * QK_NOPE).astype(BF16)
            wv = wkvb[:, :, QK_NOPE:].reshape(KV_LORA, MLA_HEADS * V_HEAD).astype(BF16)
            gq = _pad_cols(row1(mla_q_norm[j]) * SM_SCALE, QK_PAD)
            gk = _pad_cols(row1(mla_k_norm[j]), QK_PAD)
            cq, ckv, kr = _rowwise(
                _mla_a_body, "mla_a", n, tm, [xf], [mod_t],
                [row1(norm1_g[i]), wa, row1(mla_qa_norm[j]), row1(mla_kva_norm[j])],
                [(Q_LORA, BF16), (KV_LORA, BF16), (LANES, F32)])
            rope_in = [(cos_t, lay.rope_spec(tm)), (sin_t, lay.rope_spec(tm))]
            (q,) = _rowwise(_mla_q_body, "mla_q", n, tm, [cq], rope_in, [wqb, gq], [(MLA_HEADS * QK_PAD, BF16)])
            k, v = _rowwise(_mla_kv_body, "mla_kv", n, tm, [ckv, kr], rope_in, [wk, wv, gk],
                            [(MLA_HEADS * QK_PAD, BF16), (MLA_HEADS * V_PAD, BF16)])
            o = _attention(q, k, v, lay, tq=min(2048, T))
            (xf,) = _rowwise(_proj_resid_body, "mla_out", n, tm, [o, xf], [mod_t], [mla_wo[j].astype(BF16)], [(D, F32)])
        else:
            vres = j > 0
            mix = _pad_rows(rwkv_mix[j], 8)
            w1c = jnp.concatenate([rwkv_w1[j, 0], rwkv_w1[j, 1]], axis=1).astype(BF16)
            a1c = jnp.concatenate([rwkv_a1[j, 0], rwkv_a1[j, 1]], axis=1).astype(BF16)
            g1p = _pad_cols(rwkv_g1[j], 2 * LANES).astype(BF16)
            w2c = _block_diag2(rwkv_w2[j, 0], rwkv_w2[j, 1]).astype(BF16)
            a2c = _block_diag2(rwkv_a2[j, 0], rwkv_a2[j, 1]).astype(BF16)
            g2p = _pad_rows(rwkv_g2[j], 2 * LANES).astype(BF16)
            consts = [row1(norm1_g[i]), mix, rwkv_wr[j].astype(BF16), rwkv_wk[j].astype(BF16),
                      rwkv_wv[j].astype(BF16), w1c, a1c, g1p, w2c, a2c, g2p,
                      rwkv_w0[j].reshape(1, 2 * D), rwkv_a0[j].reshape(1, 2 * D), row1(rwkv_k_k[j]), ones4]
            rows = [xf]
            if vres:
                rows.append(v_first)
                consts += [_pad_cols(rwkv_v1[j - 1], LANES).astype(BF16), _pad_rows(rwkv_v2[j - 1], LANES).astype(BF16),
                           row1(rwkv_v0[j - 1])]
            halo_per_tile = tm_small // HALO
            before = pl.BlockSpec((HALO, D), lambda t: (jnp.maximum(t * halo_per_tile - 1, 0), 0))
            after = pl.BlockSpec((HALO, D), lambda t: (jnp.minimum((t + 1) * halo_per_tile, n // HALO - 1), 0))
            r, kf, lw, ad, kk, v_out, gate = _rowwise(
                functools.partial(_rwkv_feat_body, vres=vres, lay=lay), "rwkv_feat", n, tm_small,
                rows, [(xf, before), (xf, after), mod_s], consts,
                [(D, BF16), (D, F32), (2 * D, F32), (2 * D, BF16), (D, BF16), (D, F32), (D, BF16)])
            if j == 0:
                v_first = v_out
            k_a = row1(rwkv_k_a[j])
            y_fwd, y_rev = _scan(lw, ad, kf, kk, v_out, r, k_a, lay)
            (xf,) = _rowwise(
                _rwkv_post_body, "rwkv_out", n, tm_small, [y_fwd, y_rev, r, kf, ad, v_out, gate, xf], [mod_s],
                [k_a, row1(rwkv_r_k[j]), row1(rwkv_ln_w[j]), row1(rwkv_ln_b[j]), ones4, rwkv_wo[j].astype(BF16)],
                [(D, F32)])

        if i % 2 == 0:
            (xf,) = _rowwise(functools.partial(_ffn_body, tf=FFN_CHUNK), "dense_ffn", n, tm, [xf], [mod_t],
                             [row1(norm2_g[i]), ffn_b[0][j], ffn_b[1][j], ffn_b[2][j]], [(D, F32)])
        else:
            h, route = _rowwise(_router_body, "router", n, tm, [xf], [mod_t],
                                [row1(norm2_g[i]), _pad_cols(moe_router[j], LANES)], [(D, F32), (LANES, F32)])
            pos1, pos2, src, tile_expert, n_used = _route_plan(route, n, MOE_TM)
            ys = _gffn(tile_expert, n_used, src, h, moe_b[0], moe_b[1], moe_b[2], j, MOE_TM, tf=FFN_CHUNK)
            n_out = lay.n_lat if i == depth - 1 else n
            xf = _moe_combine(pos1, pos2, ys, route, xf, modtab, lay, tm, n_out)
    return xf[:B * T].reshape(B, T, D)
```
